```python
import jax
import jax.numpy as jnp
from jax import lax
import numpy as np

D_MODEL = 1024
BATCH = 8
SEQ = 2048
DEPTH = 1

CHUNK = 64
HEAD_DIM = 64
N_HEADS_A = 8
N_HEADS_B = 8
WIDTH_A = N_HEADS_A * HEAD_DIM
WIDTH_B = N_HEADS_B * HEAD_DIM
BAND_CHUNKS = 9
MAX_REL = 128
ROPE_THETA = 500000.0
ROPE_DIM = HEAD_DIM // 4
N_IDX_HEADS = 8
IDX_DIM = 64
TOPK_MAX = 256
QBLOCK_B = CHUNK
N_EXPERTS = 64
N_EXPERT_GROUPS = 8
EXPERTS_PER_GROUP = N_EXPERTS // N_EXPERT_GROUPS
TOPK_GROUPS = 4
TOPK_EXPERTS = 8
D_EXPERT = 256
D_SHARED = 256
ROUTED_SCALE = 1.0
N_MOD = 6
DEEPNORM_ALPHA = (2.0 * DEPTH) ** 0.25
DEEPNORM_BETA = (8.0 * DEPTH) ** -0.25
LN_EPS = 1e-5
IN_SPLITS = (WIDTH_A, WIDTH_A, WIDTH_A, WIDTH_B, WIDTH_B, WIDTH_B,
             N_IDX_HEADS * IDX_DIM, IDX_DIM, N_IDX_HEADS, D_MODEL, D_MODEL)
D_IN = sum(IN_SPLITS)

kernel_name = 'hybrid_chunk_band_dsa_moe_block'


def layer_norm(z, gain=None, bias=None):
    zf = z.astype(jnp.float32)
    mu = zf.mean(-1, keepdims=True)
    var = jnp.square(zf - mu).mean(-1, keepdims=True)
    y = (zf - mu) * lax.rsqrt(var + LN_EPS)
    if gain is not None:
        y = y * gain.astype(jnp.float32) + bias.astype(jnp.float32)
    return y.astype(z.dtype)


def split_cols(z, sizes):
    outs, start = [], 0
    for n in sizes:
        outs.append(z[..., start:start + n])
        start += n
    return outs


def rotary_tables(positions, dtype):
    inv_freq = ROPE_THETA ** (-jnp.arange(0, ROPE_DIM, 2, dtype=jnp.float32) / ROPE_DIM)
    ang = positions.astype(jnp.float32)[..., None] * inv_freq
    return jnp.cos(ang).astype(dtype), jnp.sin(ang).astype(dtype)


def apply_rope(z, cos, sin):
    half = ROPE_DIM // 2
    z1, z2, rest = z[..., :half], z[..., half:ROPE_DIM], z[..., ROPE_DIM:]
    return jnp.concatenate([z1 * cos - z2 * sin, z2 * cos + z1 * sin, rest], axis=-1)


def chunk_band_attention(q, k, v, rel_bias):
    b, s, h, dh = q.shape
    n_chunks = s // CHUNK
    pad = (BAND_CHUNKS - 1) * CHUNK
    band = BAND_CHUNKS * CHUNK
    kp = jnp.pad(k, ((0, 0), (pad, 0), (0, 0), (0, 0)))
    vp = jnp.pad(v, ((0, 0), (pad, 0), (0, 0), (0, 0)))
    q_band_pos = jnp.arange(CHUNK)[:, None] + pad
    k_band_pos = jnp.arange(band)[None, :]
    rel_idx = jnp.clip(q_band_pos - k_band_pos, -MAX_REL, MAX_REL) + MAX_REL
    bias = rel_bias[:, rel_idx].astype(jnp.float32)
    scale = dh ** -0.5

    def one_chunk(ci):
        start = ci * CHUNK
        qc = lax.dynamic_slice_in_dim(q, start, CHUNK, axis=1)
        kc = lax.dynamic_slice_in_dim(kp, start, band, axis=1)
        vc = lax.dynamic_slice_in_dim(vp, start, band, axis=1)
        sc = jnp.einsum('bihd,bjhd->bhij', qc, kc).astype(jnp.float32) * scale + bias
        valid = (start - pad + jnp.arange(band)) >= 0
        sc = jnp.where(valid, sc, -jnp.inf)
        p = jax.nn.softmax(sc, axis=-1).astype(vc.dtype)
        return jnp.einsum('bhij,bjhd->bihd', p, vc)

    out = lax.map(one_chunk, jnp.arange(n_chunks))
    return out.transpose(1, 0, 2, 3, 4).reshape(b, s, h, dh)


def indexer_sparse_attention(q, k, v, q_idx, k_idx, w_idx):
    b, s, h, dh = q.shape
    k_sel = min(TOPK_MAX, s // 4)
    n_blocks = s // QBLOCK_B
    key_pos = jnp.arange(s)
    scale = dh ** -0.5
    idx_scale = IDX_DIM ** -0.5

    def one_block(bi):
        start = bi * QBLOCK_B
        qb = lax.dynamic_slice_in_dim(q, start, QBLOCK_B, axis=1)
        qib = lax.dynamic_slice_in_dim(q_idx, start, QBLOCK_B, axis=1)
        wb = lax.dynamic_slice_in_dim(w_idx, start, QBLOCK_B, axis=1)
        t_pos = start + jnp.arange(QBLOCK_B)
        limit = (t_pos // CHUNK + 1) * CHUNK
        admissible = key_pos[None, :] < limit[:, None]
        logits = jax.nn.relu(jnp.einsum('bthd,bsd->bths', qib, k_idx).astype(jnp.float32) * idx_scale)
        score = jnp.einsum('bths,bth->bts', logits, wb.astype(jnp.float32))
        score = jnp.where(admissible, score, -jnp.inf)
        top_val, top_idx = lax.top_k(score, k_sel)
        sel_valid = jnp.isfinite(top_val)
        kg = jax.vmap(lambda kk, ii: kk[ii])(k, top_idx)
        vg = jax.vmap(lambda vv, ii: vv[ii])(v, top_idx)
        sc = jnp.einsum('bthd,btkhd->bthk', qb, kg).astype(jnp.float32) * scale
        sc = jnp.where(sel_valid[:, :, None, :], sc, -jnp.inf)
        p = jax.nn.softmax(sc, axis=-1).astype(vg.dtype)
        return jnp.einsum('bthk,btkhd->bthd', p, vg)

    out = lax.map(one_block, jnp.arange(n_blocks))
    return out.transpose(1, 0, 2, 3, 4).reshape(b, s, h, dh)


def hybrid_mixer(u, cos, sin, w_in, rel_bias, idx_k_norm_g, idx_k_norm_b, w_branch_a, w_branch_b, w_out):
    b, s, _ = u.shape
    proj = jnp.dot(u, w_in)
    qa, ka, va, qb, kb, vb, qi, ki, wi, ga, gb = split_cols(proj, IN_SPLITS)
    oa = chunk_band_attention(qa.reshape(b, s, N_HEADS_A, HEAD_DIM),
                              ka.reshape(b, s, N_HEADS_A, HEAD_DIM),
                              va.reshape(b, s, N_HEADS_A, HEAD_DIM), rel_bias)
    cos4, sin4 = cos[:, :, None, :], sin[:, :, None, :]
    qb = apply_rope(qb.reshape(b, s, N_HEADS_B, HEAD_DIM), cos4, sin4)
    kb = apply_rope(kb.reshape(b, s, N_HEADS_B, HEAD_DIM), cos4, sin4)
    vb = vb.reshape(b, s, N_HEADS_B, HEAD_DIM)
    qi = apply_rope(qi.reshape(b, s, N_IDX_HEADS, IDX_DIM), cos4, sin4)
    ki = apply_rope(layer_norm(ki, idx_k_norm_g, idx_k_norm_b), cos, sin)
    wi = wi * (N_IDX_HEADS ** -0.5)
    ob = indexer_sparse_attention(qb, kb, vb, qi, ki, wi)
    ya = jnp.dot(oa.reshape(b, s, WIDTH_A), w_branch_a)
    yb = jnp.dot(ob.reshape(b, s, WIDTH_B), w_branch_b)
    merged = jax.nn.sigmoid(ga) * ya + jax.nn.sigmoid(gb) * yb
    return jnp.dot(merged, w_out)


def moe_ffn(u, w_router, router_bias, w_exp_gate, w_exp_up, w_exp_down, w_sh_gate, w_sh_up, w_sh_down):
    b, s, d = u.shape
    t = u.reshape(b * s, d)
    aff = jax.nn.sigmoid(jnp.dot(t, w_router).astype(jnp.float32))
    biased = aff + router_bias.astype(jnp.float32)
    grouped = biased.reshape(-1, N_EXPERT_GROUPS, EXPERTS_PER_GROUP)
    group_score = lax.top_k(grouped, 2)[0].sum(-1)
    _, top_groups = lax.top_k(group_score, TOPK_GROUPS)
    group_mask = jax.nn.one_hot(top_groups, N_EXPERT_GROUPS, dtype=jnp.float32).sum(1) > 0
    expert_mask = jnp.repeat(group_mask, EXPERTS_PER_GROUP, axis=1)
    _, top_e = lax.top_k(jnp.where(expert_mask, biased, -jnp.inf), TOPK_EXPERTS)
    top_aff = jnp.take_along_axis(aff, top_e, axis=1)
    top_w = top_aff / top_aff.sum(-1, keepdims=True) * ROUTED_SCALE
    combine = jnp.einsum('tk,tke->te', top_w,
                         jax.nn.one_hot(top_e, N_EXPERTS, dtype=jnp.float32)).astype(u.dtype)
    y = jnp.dot(jax.nn.silu(jnp.dot(t, w_sh_gate)) * jnp.dot(t, w_sh_up), w_sh_down)
    for g in range(N_EXPERT_GROUPS):
        sl = slice(g * EXPERTS_PER_GROUP, (g + 1) * EXPERTS_PER_GROUP)
        hg = jax.nn.silu(jnp.einsum('td,edf->tef', t, w_exp_gate[sl])) * jnp.einsum('td,edf->tef', t, w_exp_up[sl])
        y = y + jnp.einsum('tef,efd->td', hg * combine[:, sl, None], w_exp_down[sl])
    return y.reshape(b, s, d)


def setup_inputs(seed: int = 0) -> dict:
    key = jax.random.key(seed)
    ks = jax.random.split(key, 24)
    f32 = jnp.float32
    L, D = DEPTH, D_MODEL

    def nrm(k, shape, scale):
        return jax.random.normal(k, shape, f32) * scale

    x = nrm(ks[0], (BATCH, SEQ, D), 1.0)
    c = nrm(ks[1], (BATCH, D), 1.0)
    offsets = jax.random.randint(ks[2], (BATCH, 1), 0, 4096, dtype=jnp.int32)
    positions = offsets + jnp.arange(SEQ, dtype=jnp.int32)[None, :]
    return {
        'x': x,
        'c': c,
        'positions': positions,
        'w_ada': nrm(ks[3], (L, D, N_MOD * D), 0.5 * D ** -0.5),
        'b_ada': nrm(ks[4], (L, N_MOD * D), 0.01),
        'w_in': nrm(ks[5], (L, D, D_IN), D ** -0.5),
        'rel_bias': nrm(ks[6], (L, N_HEADS_A, 2 * MAX_REL + 1), 0.1),
        'idx_k_norm_g': 1.0 + nrm(ks[7], (L, IDX_DIM), 0.02),
        'idx_k_norm_b': nrm(ks[8], (L, IDX_DIM), 0.02),
        'w_branch_a': nrm(ks[9], (L, WIDTH_A, D), WIDTH_A ** -0.5),
        'w_branch_b': nrm(ks[10], (L, WIDTH_B, D), WIDTH_B ** -0.5),
        'w_out': nrm(ks[11], (L, D, D), DEEPNORM_BETA * D ** -0.5),
        'ln1_g': 1.0 + nrm(ks[12], (L, D), 0.02),
        'ln1_b': nrm(ks[13], (L, D), 0.02),
        'w_router': nrm(ks[14], (L, D, N_EXPERTS), D ** -0.5),
        'router_bias': nrm(ks[15], (L, N_EXPERTS), 0.01),
        'w_exp_gate': nrm(ks[16], (L, N_EXPERTS, D, D_EXPERT), D ** -0.5),
        'w_exp_up': nrm(ks[17], (L, N_EXPERTS, D, D_EXPERT), D ** -0.5),
        'w_exp_down': nrm(ks[18], (L, N_EXPERTS, D_EXPERT, D), DEEPNORM_BETA * D_EXPERT ** -0.5),
        'w_sh_gate': nrm(ks[19], (L, D, D_SHARED), D ** -0.5),
        'w_sh_up': nrm(ks[20], (L, D, D_SHARED), D ** -0.5),
        'w_sh_down': nrm(ks[21], (L, D_SHARED, D), DEEPNORM_BETA * D_SHARED ** -0.5),
        'ln2_g': 1.0 + nrm(ks[22], (L, D), 0.02),
        'ln2_b': nrm(ks[23], (L, D), 0.02),
    }


def reference(x, c, positions, w_ada, b_ada, w_in, rel_bias, idx_k_norm_g, idx_k_norm_b,
              w_branch_a, w_branch_b, w_out, ln1_g, ln1_b, w_router, router_bias,
              w_exp_gate, w_exp_up, w_exp_down, w_sh_gate, w_sh_up, w_sh_down, ln2_g, ln2_b):
    cos, sin = rotary_tables(positions, x.dtype)
    c_act = jax.nn.silu(c)
    for l in range(DEPTH):
        mod = jnp.dot(c_act, w_ada[l]) + b_ada[l]
        sh1, sc1, g1, sh2, sc2, g2 = jnp.split(mod[:, None, :], N_MOD, axis=-1)
        u = layer_norm(x) * (1 + sc1) + sh1
        mix = hybrid_mixer(u, cos, sin, w_in[l], rel_bias[l], idx_k_norm_g[l], idx_k_norm_b[l],
                           w_branch_a[l], w_branch_b[l], w_out[l])
        x = layer_norm(DEEPNORM_ALPHA * x + g1 * mix, ln1_g[l], ln1_b[l])
        u2 = layer_norm(x) * (1 + sc2) + sh2
        ffn = moe_ffn(u2, w_router[l], router_bias[l], w_exp_gate[l], w_exp_up[l], w_exp_down[l],
                      w_sh_gate[l], w_sh_up[l], w_sh_down[l])
        x = layer_norm(DEEPNORM_ALPHA * x + g2 * ffn, ln2_g[l], ln2_b[l])
    return x
```

```python
import functools

import jax
import jax.numpy as jnp
import numpy as np
from jax import lax
from jax.experimental import pallas as pl
from jax.experimental.pallas import tpu as pltpu

F32 = jnp.float32
BF16 = jnp.bfloat16
NEG_INF = float("-inf")

CHUNK = 64
HEAD_DIM = 64
N_HEADS = 8
WIDTH = N_HEADS * HEAD_DIM
BAND_CHUNKS = 9
BAND_PAD = (BAND_CHUNKS - 1) * CHUNK
MAX_REL = 128
ROPE_THETA = 500000.0
ROPE_DIM = HEAD_DIM // 4
ROPE_HALF = ROPE_DIM // 2
N_IDX_HEADS = 8
IDX_DIM = 64
TOPK_MAX = 256
N_EXPERTS = 64
N_GROUPS = 8
GROUP_SIZE = N_EXPERTS // N_GROUPS
TOPK_GROUPS = 4
TOPK_EXPERTS = 8
ROUTED_SCALE = 1.0
N_MOD = 6
LN_EPS = 1e-5
LANES = 128

VMEM_LIMIT = 56 * 1024 * 1024

EXPERTS_PER_STEP = 4


def _cparams(sem):
    return pltpu.CompilerParams(dimension_semantics=sem, vmem_limit_bytes=VMEM_LIMIT)


def _ln(z):
    mu = jnp.mean(z, axis=-1, keepdims=True)
    zc = z - mu
    var = jnp.mean(zc * zc, axis=-1, keepdims=True)
    return zc * lax.rsqrt(var + LN_EPS)


def _sigmoid(z):
    return 1.0 / (1.0 + jnp.exp(-z))


def _dot(a, b):
    return jnp.dot(a, b, preferred_element_type=F32)


def _dot_nt(a, b):
    return lax.dot_general(a, b, (((1,), (1,)), ((), ())), preferred_element_type=F32)


def _split_bf16(z):
    hi = z.astype(BF16)
    lo = (z - hi.astype(F32)).astype(BF16)
    return hi, lo


def _ada_kernel(c_ref, w_ref, b_ref, o_ref):
    c = c_ref[...]
    ca = c * _sigmoid(c)
    ca_hi, ca_lo = _split_bf16(ca)
    w = w_ref[...]
    w_hi, w_lo = _split_bf16(w)
    acc = _dot(ca_hi, w_hi) + _dot(ca_lo, w_hi) + _dot(ca_hi, w_lo)
    o_ref[...] = acc + b_ref[...]


def _ada(c, w_ada, b_ada):
    bsz, d = c.shape
    n = w_ada.shape[1]
    tn = 1024
    return pl.pallas_call(
        _ada_kernel,
        out_shape=jax.ShapeDtypeStruct((bsz, n), F32),
        grid=(n // tn,),
        in_specs=[
            pl.BlockSpec((bsz, d), lambda j: (0, 0)),
            pl.BlockSpec((d, tn), lambda j: (0, j)),
            pl.BlockSpec((1, tn), lambda j: (0, j)),
        ],
        out_specs=pl.BlockSpec((bsz, tn), lambda j: (0, j)),
        compiler_params=_cparams(("arbitrary",)),
        name="ada",
    )(c, w_ada, b_ada.reshape(1, n))


def _rope(z, c_t, s1_t, s2_t):
    n = z.shape[-1]
    return z * c_t + pltpu.roll(z, n - ROPE_HALF, 1) * s1_t + pltpu.roll(z, ROPE_HALF, 1) * s2_t


def _proj_kernel(x_ref, mod_ref, rope_ref, w7_ref, wkw_ref, wg_ref, lnk_ref,
                 qa_ref, ka_ref, va_ref, qb_ref, kb_ref, vb_ref, qi_ref, ki_ref, wi_ref,
                 ga_ref, gb_ref):
    x = x_ref[0]
    mod = mod_ref[0]
    sh1 = mod[0:1, :]
    sc1 = mod[1:2, :]
    u = (_ln(x) * (1.0 + sc1) + sh1).astype(BF16)

    rope = rope_ref[0]
    c1 = rope[:, 0:LANES]
    s1 = rope[:, LANES:2 * LANES]
    s2 = rope[:, 2 * LANES:3 * LANES]
    reps = WIDTH // LANES
    c_t = jnp.concatenate([c1] * reps, axis=1)
    s1_t = jnp.concatenate([s1] * reps, axis=1)
    s2_t = jnp.concatenate([s2] * reps, axis=1)

    att_scale = HEAD_DIM ** -0.5
    idx_scale = IDX_DIM ** -0.5

    def seg(k):
        return _dot(u, w7_ref[:, k * WIDTH:(k + 1) * WIDTH])

    qa_ref[0] = (seg(0) * att_scale).astype(BF16)
    ka_ref[0] = seg(1).astype(BF16)
    va_ref[0] = seg(2).astype(BF16)
    qb_ref[0] = (_rope(seg(3), c_t, s1_t, s2_t) * att_scale).astype(BF16)
    kb_ref[0] = _rope(seg(4), c_t, s1_t, s2_t).astype(BF16)
    vb_ref[0] = seg(5).astype(BF16)
    qi_ref[0] = (_rope(seg(6), c_t, s1_t, s2_t) * idx_scale).astype(BF16)

    z = _dot(u, wkw_ref[...])
    lane = lax.broadcasted_iota(jnp.int32, z.shape, 1)
    is_k = lane < IDX_DIM
    mu = jnp.sum(jnp.where(is_k, z, 0.0), axis=-1, keepdims=True) * (1.0 / IDX_DIM)
    zc = jnp.where(is_k, z - mu, 0.0)
    var = jnp.sum(zc * zc, axis=-1, keepdims=True) * (1.0 / IDX_DIM)
    lnk = lnk_ref[...]
    y = zc * lax.rsqrt(var + LN_EPS) * lnk[0:1, :] + lnk[1:2, :]
    y = _rope(y, c1, jnp.where(is_k, s1, 0.0), jnp.where(is_k, s2, 0.0))
    ki_ref[0] = y[:, 0:IDX_DIM].astype(BF16)
    wi_ref[0] = z[:, IDX_DIM:IDX_DIM + N_IDX_HEADS] * (N_IDX_HEADS ** -0.5)

    d = ga_ref.shape[-1]
    ga_ref[0] = _dot(u, wg_ref[:, 0:d]).astype(BF16)
    gb_ref[0] = _dot(u, wg_ref[:, d:2 * d]).astype(BF16)


def _proj(x, mod, rope_tab, w7, wkw, wg, lnk, tm):
    bsz, s, d = x.shape
    const = dict(pipeline_mode=pl.Buffered(1))
    row = lambda b, i: (b, i, 0)
    wspec = lambda shape: pl.BlockSpec(shape, lambda b, i: (0, 0), **const)
    out_w = jax.ShapeDtypeStruct((bsz, s, WIDTH), BF16)
    out_d = jax.ShapeDtypeStruct((bsz, s, d), BF16)
    return pl.pallas_call(
        _proj_kernel,
        out_shape=[out_w] * 7 + [
            jax.ShapeDtypeStruct((bsz, s, IDX_DIM), BF16),
            jax.ShapeDtypeStruct((bsz, s, N_IDX_HEADS), F32),
            out_d, out_d],
        grid=(bsz, s // tm),
        in_specs=[
            pl.BlockSpec((1, tm, d), row),
            pl.BlockSpec((1, N_MOD, d), lambda b, i: (b, 0, 0)),
            pl.BlockSpec((1, tm, 3 * LANES), row),
            wspec(w7.shape), wspec(wkw.shape), wspec(wg.shape), wspec(lnk.shape),
        ],
        out_specs=[pl.BlockSpec((1, tm, WIDTH), row)] * 7 + [
            pl.BlockSpec((1, tm, IDX_DIM), row),
            pl.BlockSpec((1, tm, N_IDX_HEADS), row),
            pl.BlockSpec((1, tm, d), row), pl.BlockSpec((1, tm, d), row)],
        compiler_params=_cparams(("arbitrary", "arbitrary")),
        name="proj",
    )(x, mod, rope_tab, w7, wkw, wg, lnk)


def _band_kernel(q_ref, k_ref, v_ref, bias_ref, o_ref, kpad, vpad, *, tq):
    i = pl.program_id(1)
    s = k_ref.shape[1]
    win = tq + BAND_PAD

    @pl.when(i == 0)
    def _():
        zeros = jnp.zeros((BAND_PAD, WIDTH), BF16)
        kpad[0:BAND_PAD, :] = zeros
        vpad[0:BAND_PAD, :] = zeros
        kpad[BAND_PAD:BAND_PAD + s, :] = k_ref[0]
        vpad[BAND_PAD:BAND_PAD + s, :] = v_ref[0]

    q0 = pl.multiple_of(i * tq, tq)
    kw = kpad[pl.ds(q0, win), :]
    vw = vpad[pl.ds(q0, win), :]
    q = q_ref[0]
    col = lax.broadcasted_iota(jnp.int32, (tq, win), 1)
    is_frame = (col + q0) >= BAND_PAD
    for h in range(N_HEADS):
        sl = slice(h * HEAD_DIM, (h + 1) * HEAD_DIM)
        sc = _dot_nt(q[:, sl], kw[:, sl]) + bias_ref[h]
        sc = jnp.where(is_frame, sc, NEG_INF)
        m = jnp.max(sc, axis=-1, keepdims=True)
        p = jnp.exp(sc - m)
        l = jnp.sum(p, axis=-1, keepdims=True)
        o = _dot(p.astype(BF16), vw[:, sl]) / l
        o_ref[0, :, sl] = o.astype(BF16)


def _band(qa, ka, va, bias_tab, tq):
    bsz, s, _ = qa.shape
    win = tq + BAND_PAD
    return pl.pallas_call(
        functools.partial(_band_kernel, tq=tq),
        out_shape=jax.ShapeDtypeStruct((bsz, s, WIDTH), BF16),
        grid=(bsz, s // tq),
        in_specs=[
            pl.BlockSpec((1, tq, WIDTH), lambda b, i: (b, i, 0)),
            pl.BlockSpec((1, s, WIDTH), lambda b, i: (b, 0, 0)),
            pl.BlockSpec((1, s, WIDTH), lambda b, i: (b, 0, 0)),
            pl.BlockSpec((N_HEADS, tq, win), lambda b, i: (0, 0, 0)),
        ],
        out_specs=pl.BlockSpec((1, tq, WIDTH), lambda b, i: (b, i, 0)),
        scratch_shapes=[pltpu.VMEM((BAND_PAD + s, WIDTH), BF16),
                        pltpu.VMEM((BAND_PAD + s, WIDTH), BF16)],
        compiler_params=_cparams(("arbitrary", "arbitrary")),
        name="band",
    )(qa, ka, va, bias_tab)


BISECT_STEPS_PER_CHECK = 4
BISECT_MAX_CHECKS = 400


def _dsa_kernel(qi_ref, wi_ref, qb_ref, ki_ref, kb_ref, vb_ref, o_ref, sc_scr, *, tq, k_sel):
    i = pl.program_id(1)
    s = ki_ref.shape[1]
    q0 = i * tq
    kf = float(k_sel)

    q_idx = qi_ref[0]
    ki = ki_ref[0]
    w = wi_ref[0]
    score = jnp.zeros((tq, s), F32)
    for h in range(N_IDX_HEADS):
        lg = _dot_nt(q_idx[:, h * IDX_DIM:(h + 1) * IDX_DIM], ki)
        score = score + jnp.maximum(lg, 0.0) * w[:, h:h + 1]
    t_pos = q0 + lax.broadcasted_iota(jnp.int32, (tq, 1), 0)
    limit = (t_pos // CHUNK + 1) * CHUNK
    key_pos = lax.broadcasted_iota(jnp.int32, (tq, s), 1)
    admissible = key_pos < limit
    sc_scr[...] = jnp.where(admissible, score, NEG_INF)

    def count(pred):
        return jnp.sum(jnp.where(pred, 1.0, 0.0), axis=-1, keepdims=True)

    sc = sc_scr[...]
    smax = jnp.max(sc, axis=-1, keepdims=True)
    smin = jnp.min(jnp.where(admissible, sc, jnp.inf), axis=-1, keepdims=True)
    n_adm = limit.astype(F32)
    c_max = count(sc >= smax)
    c_pos = count(sc > 0.0)
    c_nn = count(sc >= 0.0)
    zero = jnp.zeros_like(smax)
    at_zero = (c_pos < kf) & (c_nn >= kf)
    below_zero = c_nn < kf
    lo = jnp.where(at_zero | ~below_zero, zero, smin)
    clo = jnp.where(at_zero | ~below_zero, c_nn, n_adm)
    hi = jnp.where(at_zero | below_zero, zero, smax)
    few = n_adm <= kf
    lo = jnp.where(few, smin, lo)
    clo = jnp.where(few, n_adm, clo)
    top_tied = (c_max >= kf) & ~few
    lo = jnp.where(top_tied, smax, lo)
    clo = jnp.where(top_tied, c_max, clo)
    hi = jnp.where(top_tied, smax, hi)
    done0 = jnp.where(few | at_zero | top_tied, 1.0, 0.0)

    def bisect(carry):
        lo, hi, clo, done, it = carry
        for _ in range(BISECT_STEPS_PER_CHECK):
            mid = 0.5 * lo + 0.5 * hi
            stuck = (mid <= lo) | (mid >= hi)
            c = count(sc_scr[...] >= mid)
            ge = c >= kf
            lo = jnp.where(ge, mid, lo)
            clo = jnp.where(ge, c, clo)
            hi = jnp.where(ge, hi, mid)
            done = jnp.where(stuck | (clo <= kf), 1.0, done)
        return lo, hi, clo, done, it + 1

    def not_converged(carry):
        _, _, _, done, it = carry
        return (jnp.min(done) < 0.5) & (it < BISECT_MAX_CHECKS)

    lo, hi, clo, _, _ = lax.while_loop(not_converged, bisect, (lo, hi, clo, done0, jnp.int32(0)))

    thr = lo
    sc = sc_scr[...]
    gt = sc > thr
    eq = sc == thr
    c_gt = count(gt)
    c_eq = count(eq)
    need = kf - c_gt
    tie_cut = jnp.max(c_gt + c_eq - kf) > 0.0

    def with_ties():
        kc = 256
        r_i = lax.broadcasted_iota(jnp.int32, (kc, kc), 0)
        c_i = lax.broadcasted_iota(jnp.int32, (kc, kc), 1)
        before = jnp.where(r_i < c_i, 1.0, 0.0).astype(BF16)
        carry = jnp.zeros((tq, 1), F32)
        for c0 in range(0, s, kc):
            eq_c = eq[:, c0:c0 + kc]
            eq_b = jnp.where(eq_c, 1.0, 0.0).astype(BF16)
            rank = _dot(eq_b, before) + carry
            keep = gt[:, c0:c0 + kc] | (eq_c & (rank < need))
            sc_scr[:, c0:c0 + kc] = jnp.where(keep, 0.0, NEG_INF)
            carry = carry + count(eq_c)

    def without_ties():
        sc_scr[...] = jnp.where(gt | eq, 0.0, NEG_INF)

    lax.cond(tie_cut, with_ties, without_ties)

    qb = qb_ref[0]
    kb = kb_ref[0]
    vb = vb_ref[0]
    for h in range(N_HEADS):
        sl = slice(h * HEAD_DIM, (h + 1) * HEAD_DIM)
        a = _dot_nt(qb[:, sl], kb[:, sl]) + sc_scr[...]
        m = jnp.max(a, axis=-1, keepdims=True)
        p = jnp.exp(a - m)
        l = jnp.sum(p, axis=-1, keepdims=True)
        o = _dot(p.astype(BF16), vb[:, sl]) / l
        o_ref[0, :, sl] = o.astype(BF16)


def _dsa(qi, wi, qb, ki, kb, vb, tq):
    bsz, s, _ = qb.shape
    k_sel = min(TOPK_MAX, s // 4)
    row = lambda b, i: (b, i, 0)
    full = lambda b, i: (b, 0, 0)
    return pl.pallas_call(
        functools.partial(_dsa_kernel, tq=tq, k_sel=k_sel),
        out_shape=jax.ShapeDtypeStruct((bsz, s, WIDTH), BF16),
        grid=(bsz, s // tq),
        in_specs=[
            pl.BlockSpec((1, tq, N_IDX_HEADS * IDX_DIM), row),
            pl.BlockSpec((1, tq, N_IDX_HEADS), row),
            pl.BlockSpec((1, tq, WIDTH), row),
            pl.BlockSpec((1, s, IDX_DIM), full),
            pl.BlockSpec((1, s, WIDTH), full),
            pl.BlockSpec((1, s, WIDTH), full),
        ],
        out_specs=pl.BlockSpec((1, tq, WIDTH), row),
        scratch_shapes=[pltpu.VMEM((tq, s), F32)],
        compiler_params=_cparams(("arbitrary", "arbitrary")),
        name="dsa",
    )(qi, wi, qb, ki, kb, vb)


def _merge_kernel(oa_ref, ob_ref, ga_ref, gb_ref, x_ref, mod_ref, wba_ref, wbb_ref, wo_ref,
                  ln1_ref, wr_ref, rb_ref, x1_ref, u2_ref, cc_ref, msk_scr, *, alpha):
    tm = x_ref.shape[1]
    ya = _dot(oa_ref[0], wba_ref[...])
    yb = _dot(ob_ref[0], wbb_ref[...])
    merged = _sigmoid(ga_ref[0].astype(F32)) * ya + _sigmoid(gb_ref[0].astype(F32)) * yb
    mix = _dot(merged.astype(BF16), wo_ref[...])
    mod = mod_ref[0]
    g1 = mod[2:3, :]
    sh2 = mod[3:4, :]
    sc2 = mod[4:5, :]
    ln1 = ln1_ref[...]
    x1 = _ln(alpha * x_ref[0] + g1 * mix) * ln1[0:1, :] + ln1[1:2, :]
    x1_ref[0] = x1
    u2 = _ln(x1) * (1.0 + sc2) + sh2
    u2_ref[0] = u2.astype(BF16)

    u_hi, u_lo = _split_bf16(u2)
    wr_hi = wr_ref[0]
    wr_lo = wr_ref[1]
    logits = _dot(u_hi, wr_hi) + _dot(u_lo, wr_hi) + _dot(u_hi, wr_lo)
    aff = _sigmoid(logits.T[0:N_EXPERTS, :])
    biased = aff + rb_ref[...]
    grp = biased.reshape(N_GROUPS, GROUP_SIZE, tm)
    sub = lax.broadcasted_iota(jnp.int32, grp.shape, 1)
    m1 = jnp.max(grp, axis=1, keepdims=True)
    first = jnp.min(jnp.where(grp == m1, sub, GROUP_SIZE), axis=1, keepdims=True)
    m2 = jnp.max(jnp.where(sub == first, NEG_INF, grp), axis=1, keepdims=True)
    gscore = (m1 + m2).reshape(N_GROUPS, tm)
    g_i = lax.broadcasted_iota(jnp.int32, (N_GROUPS, tm), 0)
    g_rank = jnp.zeros((N_GROUPS, tm), F32)
    for g in range(N_GROUPS):
        other = gscore[g:g + 1, :]
        beats = (other > gscore) | ((other == gscore) & (g < g_i))
        g_rank = g_rank + jnp.where(beats, 1.0, 0.0)
    g_keep = jnp.where(g_rank < TOPK_GROUPS, 1.0, 0.0).reshape(N_GROUPS, 1, tm)
    e_keep = jnp.broadcast_to(g_keep, (N_GROUPS, GROUP_SIZE, tm)).reshape(N_EXPERTS, tm)
    masked = jnp.where(e_keep > 0.5, biased, NEG_INF)
    msk_scr[...] = masked
    e_i = lax.broadcasted_iota(jnp.int32, (N_EXPERTS, tm), 0)

    def rank_step(e, rank):
        other = msk_scr[pl.ds(e, 1), :]
        beats = (other > masked) | ((other == masked) & (e < e_i))
        return rank + jnp.where(beats, 1.0, 0.0)

    e_rank = lax.fori_loop(0, N_EXPERTS, rank_step, jnp.zeros((N_EXPERTS, tm), F32))
    top_aff = jnp.where(e_rank < TOPK_EXPERTS, aff, 0.0)
    comb_t = top_aff / jnp.sum(top_aff, axis=0, keepdims=True) * ROUTED_SCALE
    comb = jnp.concatenate([comb_t, jnp.zeros_like(comb_t)], axis=0).T
    c_hi = comb.astype(BF16).astype(F32)
    c_lo = comb - c_hi
    cc_ref[0] = (c_hi + pltpu.roll(c_lo, N_EXPERTS, 1)).astype(BF16)


def _merge(oa, ob, ga, gb, x, mod, wba, wbb, wo, ln1, wr, rb, tm, alpha):
    bsz, s, d = x.shape
    row = lambda b, i: (b, i, 0)
    w2 = lambda shape: pl.BlockSpec(shape, lambda b, i: (0,) * len(shape))
    return pl.pallas_call(
        functools.partial(_merge_kernel, alpha=alpha),
        out_shape=[jax.ShapeDtypeStruct((bsz, s, d), F32),
                   jax.ShapeDtypeStruct((bsz, s, d), BF16),
                   jax.ShapeDtypeStruct((bsz, s, 2 * N_EXPERTS), BF16)],
        grid=(bsz, s // tm),
        in_specs=[
            pl.BlockSpec((1, tm, WIDTH), row), pl.BlockSpec((1, tm, WIDTH), row),
            pl.BlockSpec((1, tm, d), row), pl.BlockSpec((1, tm, d), row),
            pl.BlockSpec((1, tm, d), row),
            pl.BlockSpec((1, N_MOD, d), lambda b, i: (b, 0, 0)),
            w2(wba.shape), w2(wbb.shape), w2(wo.shape), w2(ln1.shape), w2(wr.shape), w2(rb.shape),
        ],
        out_specs=[pl.BlockSpec((1, tm, d), row), pl.BlockSpec((1, tm, d), row),
                   pl.BlockSpec((1, tm, 2 * N_EXPERTS), row)],
        scratch_shapes=[pltpu.VMEM((N_EXPERTS, tm), F32)],
        compiler_params=_cparams(("arbitrary", "arbitrary")),
        name="merge",
    )(oa, ob, ga, gb, x, mod, wba, wbb, wo, ln1, wr, rb)


def _silu(z):
    return z * _sigmoid(z)


def _moe_kernel(u2_ref, cc_ref, x1_ref, mod_ref, wg_ref, wu_ref, wd_ref, wsg_ref, wsu_ref, wsd_ref,
                ln2_ref, o_ref, *, alpha):
    g = pl.program_id(1)
    t = u2_ref[...]
    d_exp = wg_ref.shape[-1]

    @pl.when(g == 0)
    def _():
        hs = _silu(_dot(t, wsg_ref[...])) * _dot(t, wsu_ref[...])
        o_ref[...] = _dot(hs.astype(BF16), wsd_ref[...])

    width = EXPERTS_PER_STEP * d_exp
    r_i = lax.broadcasted_iota(jnp.int32, (2 * N_EXPERTS, width), 0)
    c_i = lax.broadcasted_iota(jnp.int32, (2 * N_EXPERTS, width), 1)
    pick = jnp.where((r_i % N_EXPERTS) == g * EXPERTS_PER_STEP + c_i // d_exp, 1.0, 0.0).astype(BF16)
    gate = _dot(cc_ref[...], pick)

    hs = [_silu(_dot(t, wg_ref[j])) * _dot(t, wu_ref[j]) for j in range(EXPERTS_PER_STEP)]
    h = jnp.concatenate(hs, axis=1) * gate
    wd = wd_ref[...].reshape(width, wd_ref.shape[-1])
    o_ref[...] += _dot(h.astype(BF16), wd)

    @pl.when(g == pl.num_programs(1) - 1)
    def _():
        mod = mod_ref[0]
        g2 = mod[5:6, :]
        ln2 = ln2_ref[...]
        o_ref[...] = _ln(alpha * x1_ref[...] + g2 * o_ref[...]) * ln2[0:1, :] + ln2[1:2, :]


def _moe(u2, cc, x1, mod, wg, wu, wd, wsg, wsu, wsd, ln2, tm, tokens_per_batch, alpha):
    t, d = u2.shape
    n_exp, _, d_exp = wg.shape
    eps = EXPERTS_PER_STEP
    blocks_per_batch = tokens_per_batch // tm
    row = lambda i, g: (i, 0)
    w2 = lambda shape: pl.BlockSpec(shape, lambda i, g: (0,) * len(shape))
    return pl.pallas_call(
        functools.partial(_moe_kernel, alpha=alpha),
        out_shape=jax.ShapeDtypeStruct((t, d), F32),
        grid=(t // tm, n_exp // eps),
        in_specs=[
            pl.BlockSpec((tm, d), row),
            pl.BlockSpec((tm, 2 * N_EXPERTS), row),
            pl.BlockSpec((tm, d), row),
            pl.BlockSpec((1, N_MOD, d), lambda i, g: (i // blocks_per_batch, 0, 0)),
            pl.BlockSpec((eps, d, d_exp), lambda i, g: (g, 0, 0)),
            pl.BlockSpec((eps, d, d_exp), lambda i, g: (g, 0, 0)),
            pl.BlockSpec((eps, d_exp, d), lambda i, g: (g, 0, 0)),
            w2(wsg.shape), w2(wsu.shape), w2(wsd.shape), w2(ln2.shape),
        ],
        out_specs=pl.BlockSpec((tm, d), row),
        compiler_params=_cparams(("arbitrary", "arbitrary")),
        name="moe",
    )(u2, cc, x1, mod, wg, wu, wd, wsg, wsu, wsd, ln2)


def _rope_tables(positions):
    inv_freq = ROPE_THETA ** (-jnp.arange(0, ROPE_DIM, 2, dtype=F32) / ROPE_DIM)
    ang = positions.astype(F32)[..., None] * inv_freq
    cos, sin = jnp.cos(ang), jnp.sin(ang)
    ones = jnp.ones(cos.shape[:-1] + (HEAD_DIM - ROPE_DIM,), F32)
    zeros = jnp.zeros_like(ones)
    zh = jnp.zeros_like(sin)
    c = jnp.concatenate([cos, cos, ones], axis=-1)
    s1 = jnp.concatenate([-sin, zh, zeros], axis=-1)
    s2 = jnp.concatenate([zh, sin, zeros], axis=-1)
    reps = LANES // HEAD_DIM
    return jnp.concatenate([jnp.tile(c, reps), jnp.tile(s1, reps), jnp.tile(s2, reps)], axis=-1)


def _band_bias_table(rel_bias, tq):
    win = tq + BAND_PAD
    qi = np.arange(tq)[:, None]
    kj = np.arange(win)[None, :]
    rel = np.clip(qi - (kj - BAND_PAD), -MAX_REL, MAX_REL) + MAX_REL
    q_chunk = qi // CHUNK
    k_chunk = kj // CHUNK - (BAND_CHUNKS - 1)
    in_band = (k_chunk <= q_chunk) & (k_chunk >= q_chunk - (BAND_CHUNKS - 1))
    bias = rel_bias[:, rel].astype(F32)
    return jnp.where(jnp.asarray(in_band)[None], bias, NEG_INF)


def kernel(x, c, positions, w_ada, b_ada, w_in, rel_bias, idx_k_norm_g, idx_k_norm_b, w_branch_a,
           w_branch_b, w_out, ln1_g, ln1_b, w_router, router_bias, w_exp_gate, w_exp_up, w_exp_down,
           w_sh_gate, w_sh_up, w_sh_down, ln2_g, ln2_b):
    bsz, s, d = x.shape
    depth = w_ada.shape[0]
    alpha = (2.0 * depth) ** 0.25
    tm_proj = min(512, s)
    tq_band = min(256, s)
    tq_dsa = min(128, s)
    tm_merge = min(512, s)
    tm_moe = min(1024, s)

    rope_tab = _rope_tables(positions)
    n7 = 7 * WIDTH
    for l in range(depth):
        mod = _ada(c, w_ada[l], b_ada[l]).reshape(bsz, N_MOD, d)
        w_l = w_in[l]
        w7 = w_l[:, :n7].astype(BF16)
        n_kw = IDX_DIM + N_IDX_HEADS
        wkw = jnp.pad(w_l[:, n7:n7 + n_kw], ((0, 0), (0, LANES - n_kw))).astype(BF16)
        wg = w_l[:, n7 + n_kw:].astype(BF16)
        lnk = jnp.pad(jnp.stack([idx_k_norm_g[l], idx_k_norm_b[l]]), ((0, 0), (0, LANES - IDX_DIM)))
        qa, ka, va, qb, kb, vb, qi, ki, wi, ga, gb = _proj(x, mod, rope_tab, w7, wkw, wg, lnk, tm_proj)

        oa = _band(qa, ka, va, _band_bias_table(rel_bias[l], tq_band), tq_band)
        ob = _dsa(qi, wi, qb, ki, kb, vb, tq_dsa)

        wr = jnp.pad(w_router[l], ((0, 0), (0, LANES - N_EXPERTS)))
        wr_hi = wr.astype(BF16)
        wr_lo = (wr - wr_hi.astype(F32)).astype(BF16)
        x1, u2, cc = _merge(
            oa, ob, ga, gb, x, mod,
            w_branch_a[l].astype(BF16), w_branch_b[l].astype(BF16), w_out[l].astype(BF16),
            jnp.stack([ln1_g[l], ln1_b[l]]), jnp.stack([wr_hi, wr_lo]),
            router_bias[l].reshape(N_EXPERTS, 1), tm_merge, alpha)

        t = bsz * s
        out = _moe(
            u2.reshape(t, d), cc.reshape(t, 2 * N_EXPERTS), x1.reshape(t, d), mod,
            w_exp_gate[l].astype(BF16), w_exp_up[l].astype(BF16), w_exp_down[l].astype(BF16),
            w_sh_gate[l].astype(BF16), w_sh_up[l].astype(BF16), w_sh_down[l].astype(BF16),
            jnp.stack([ln2_g[l], ln2_b[l]]), tm_moe, s, alpha)
        x = out.reshape(bsz, s, d)
    return x
```

```python
import functools

import jax
import jax.numpy as jnp
import numpy as np
from jax import lax
from jax.experimental import pallas as pl
from jax.experimental.pallas import tpu as pltpu

F32 = jnp.float32
BF16 = jnp.bfloat16
NEG_INF = float("-inf")

CHUNK = 64
HEAD_DIM = 64
N_HEADS = 8
WIDTH = N_HEADS * HEAD_DIM
BAND_CHUNKS = 9
BAND_PAD = (BAND_CHUNKS - 1) * CHUNK
MAX_REL = 128
ROPE_THETA = 500000.0
ROPE_DIM = HEAD_DIM // 4
ROPE_HALF = ROPE_DIM // 2
N_IDX_HEADS = 8
IDX_DIM = 64
TOPK_MAX = 256
N_EXPERTS = 64
N_GROUPS = 8
GROUP_SIZE = N_EXPERTS // N_GROUPS
TOPK_GROUPS = 4
TOPK_EXPERTS = 8
ROUTED_SCALE = 1.0
N_MOD = 6
LN_EPS = 1e-5
LANES = 128
KEY_TILE = 256

VMEM_LIMIT = 56 * 1024 * 1024

EXPERTS_PER_STEP = 4
DSA_KEY_CLASSES = 4


def _cparams(sem):
    return pltpu.CompilerParams(dimension_semantics=sem, vmem_limit_bytes=VMEM_LIMIT)


def _ln(z):
    mu = jnp.mean(z, axis=-1, keepdims=True)
    zc = z - mu
    var = jnp.mean(zc * zc, axis=-1, keepdims=True)
    return zc * lax.rsqrt(var + LN_EPS)


def _sigmoid(z):
    return 1.0 / (1.0 + jnp.exp(-z))


def _dot(a, b):
    return jnp.dot(a, b, preferred_element_type=F32)


def _dot_nt(a, b):
    return lax.dot_general(a, b, (((1,), (1,)), ((), ())), preferred_element_type=F32)


def _split_bf16(z):
    hi = z.astype(BF16)
    lo = (z - hi.astype(F32)).astype(BF16)
    return hi, lo


def _ada_kernel(c_ref, w_ref, b_ref, o_ref):
    c = c_ref[...]
    ca = c * _sigmoid(c)
    ca_hi, ca_lo = _split_bf16(ca)
    w = w_ref[...]
    w_hi, w_lo = _split_bf16(w)
    acc = _dot(ca_hi, w_hi) + _dot(ca_lo, w_hi) + _dot(ca_hi, w_lo)
    o_ref[...] = acc + b_ref[...]


def _ada(c, w_ada, b_ada):
    bsz, d = c.shape
    n = w_ada.shape[1]
    tn = 1024
    return pl.pallas_call(
        _ada_kernel,
        out_shape=jax.ShapeDtypeStruct((bsz, n), F32),
        grid=(n // tn,),
        in_specs=[
            pl.BlockSpec((bsz, d), lambda j: (0, 0)),
            pl.BlockSpec((d, tn), lambda j: (0, j)),
            pl.BlockSpec((1, tn), lambda j: (0, j)),
        ],
        out_specs=pl.BlockSpec((bsz, tn), lambda j: (0, j)),
        compiler_params=_cparams(("arbitrary",)),
        name="ada",
    )(c, w_ada, b_ada.reshape(1, n))


def _rope(z, c_t, s1_t, s2_t):
    n = z.shape[-1]
    return z * c_t + pltpu.roll(z, n - ROPE_HALF, 1) * s1_t + pltpu.roll(z, ROPE_HALF, 1) * s2_t


def _proj_kernel(x_ref, mod_ref, rope_ref, w7_ref, wkw_ref, wg_ref, lnk_ref,
                 qa_ref, ka_ref, va_ref, qb_ref, kb_ref, vb_ref, qi_ref, ki_ref, wi_ref,
                 ga_ref, gb_ref):
    x = x_ref[0]
    mod = mod_ref[0]
    sh1 = mod[0:1, :]
    sc1 = mod[1:2, :]
    u = (_ln(x) * (1.0 + sc1) + sh1).astype(BF16)

    rope = rope_ref[0]
    c1 = rope[:, 0:LANES]
    s1 = rope[:, LANES:2 * LANES]
    s2 = rope[:, 2 * LANES:3 * LANES]
    reps = WIDTH // LANES
    c_t = jnp.concatenate([c1] * reps, axis=1)
    s1_t = jnp.concatenate([s1] * reps, axis=1)
    s2_t = jnp.concatenate([s2] * reps, axis=1)

    att_scale = HEAD_DIM ** -0.5
    idx_scale = IDX_DIM ** -0.5

    def seg(k):
        return _dot(u, w7_ref[:, k * WIDTH:(k + 1) * WIDTH])

    qa_ref[0] = (seg(0) * att_scale).astype(BF16)
    ka_ref[0] = seg(1).astype(BF16)
    va_ref[0] = seg(2).astype(BF16)
    qb_ref[0] = (_rope(seg(3), c_t, s1_t, s2_t) * att_scale).astype(BF16)
    kb_ref[0] = _rope(seg(4), c_t, s1_t, s2_t).astype(BF16)
    vb_ref[0] = seg(5).astype(BF16)
    qi_ref[0] = (_rope(seg(6), c_t, s1_t, s2_t) * idx_scale).astype(BF16)

    z = _dot(u, wkw_ref[...])
    lane = lax.broadcasted_iota(jnp.int32, z.shape, 1)
    is_k = lane < IDX_DIM
    mu = jnp.sum(jnp.where(is_k, z, 0.0), axis=-1, keepdims=True) * (1.0 / IDX_DIM)
    zc = jnp.where(is_k, z - mu, 0.0)
    var = jnp.sum(zc * zc, axis=-1, keepdims=True) * (1.0 / IDX_DIM)
    lnk = lnk_ref[...]
    y = zc * lax.rsqrt(var + LN_EPS) * lnk[0:1, :] + lnk[1:2, :]
    y = _rope(y, c1, jnp.where(is_k, s1, 0.0), jnp.where(is_k, s2, 0.0))
    ki_ref[0] = y[:, 0:IDX_DIM].astype(BF16)
    wi_ref[0] = z.T[IDX_DIM:IDX_DIM + N_IDX_HEADS, :] * (N_IDX_HEADS ** -0.5)

    d = ga_ref.shape[-1]
    ga_ref[0] = _dot(u, wg_ref[:, 0:d]).astype(BF16)
    gb_ref[0] = _dot(u, wg_ref[:, d:2 * d]).astype(BF16)


def _proj(x, mod, rope_tab, w7, wkw, wg, lnk, tm):
    bsz, s, d = x.shape
    const = dict(pipeline_mode=pl.Buffered(1))
    row = lambda b, i: (b, i, 0)
    wspec = lambda shape: pl.BlockSpec(shape, lambda b, i: (0, 0), **const)
    out_w = jax.ShapeDtypeStruct((bsz, s, WIDTH), BF16)
    out_d = jax.ShapeDtypeStruct((bsz, s, d), BF16)
    return pl.pallas_call(
        _proj_kernel,
        out_shape=[out_w] * 7 + [
            jax.ShapeDtypeStruct((bsz, s, IDX_DIM), BF16),
            jax.ShapeDtypeStruct((bsz, N_IDX_HEADS, s), F32),
            out_d, out_d],
        grid=(bsz, s // tm),
        in_specs=[
            pl.BlockSpec((1, tm, d), row),
            pl.BlockSpec((1, N_MOD, d), lambda b, i: (b, 0, 0)),
            pl.BlockSpec((1, tm, 3 * LANES), row),
            wspec(w7.shape), wspec(wkw.shape), wspec(wg.shape), wspec(lnk.shape),
        ],
        out_specs=[pl.BlockSpec((1, tm, WIDTH), row)] * 7 + [
            pl.BlockSpec((1, tm, IDX_DIM), row),
            pl.BlockSpec((1, N_IDX_HEADS, tm), lambda b, i: (b, 0, i)),
            pl.BlockSpec((1, tm, d), row), pl.BlockSpec((1, tm, d), row)],
        compiler_params=_cparams(("arbitrary", "arbitrary")),
        name="proj",
    )(x, mod, rope_tab, w7, wkw, wg, lnk)


def _band_kernel(q_ref, k_ref, v_ref, bias_ref, o_ref, kpad, vpad, *, tq):
    i = pl.program_id(1)
    s = k_ref.shape[1]
    win = tq + BAND_PAD

    @pl.when(i == 0)
    def _():
        zeros = jnp.zeros((BAND_PAD, WIDTH), BF16)
        kpad[0:BAND_PAD, :] = zeros
        vpad[0:BAND_PAD, :] = zeros
        kpad[BAND_PAD:BAND_PAD + s, :] = k_ref[0]
        vpad[BAND_PAD:BAND_PAD + s, :] = v_ref[0]

    q0 = pl.multiple_of(i * tq, tq)
    kw = kpad[pl.ds(q0, win), :]
    vw = vpad[pl.ds(q0, win), :]
    q = q_ref[0]
    col = lax.broadcasted_iota(jnp.int32, (tq, win), 1)
    is_frame = (col + q0) >= BAND_PAD
    for h in range(N_HEADS):
        sl = slice(h * HEAD_DIM, (h + 1) * HEAD_DIM)
        sc = _dot_nt(q[:, sl], kw[:, sl]) + bias_ref[h]
        sc = jnp.where(is_frame, sc, NEG_INF)
        m = jnp.max(sc, axis=-1, keepdims=True)
        p = jnp.exp(sc - m)
        l = jnp.sum(p, axis=-1, keepdims=True)
        o = _dot(p.astype(BF16), vw[:, sl]) / l
        o_ref[0, :, sl] = o.astype(BF16)


def _band(qa, ka, va, bias_tab, tq):
    bsz, s, _ = qa.shape
    win = tq + BAND_PAD
    return pl.pallas_call(
        functools.partial(_band_kernel, tq=tq),
        out_shape=jax.ShapeDtypeStruct((bsz, s, WIDTH), BF16),
        grid=(bsz, s // tq),
        in_specs=[
            pl.BlockSpec((1, tq, WIDTH), lambda b, i: (b, i, 0)),
            pl.BlockSpec((1, s, WIDTH), lambda b, i: (b, 0, 0)),
            pl.BlockSpec((1, s, WIDTH), lambda b, i: (b, 0, 0)),
            pl.BlockSpec((N_HEADS, tq, win), lambda b, i: (0, 0, 0)),
        ],
        out_specs=pl.BlockSpec((1, tq, WIDTH), lambda b, i: (b, i, 0)),
        scratch_shapes=[pltpu.VMEM((BAND_PAD + s, WIDTH), BF16),
                        pltpu.VMEM((BAND_PAD + s, WIDTH), BF16)],
        compiler_params=_cparams(("arbitrary", "arbitrary")),
        name="band",
    )(qa, ka, va, bias_tab)


BISECT_STEPS_PER_CHECK = 4
BISECT_MAX_CHECKS = 400
COUNT_ROWS = 64


def _dsa_kernel(qi_ref, wit_ref, qb_ref, ki_ref, kb_ref, vb_ref, o_ref, sct_scr, msk_scr, a_scr, p_scr,
                *, tq, k_sel, q_base):
    i = pl.program_id(1)
    sk = ki_ref.shape[1]
    n_kt = sk // KEY_TILE
    n_lg = sk // LANES
    q0 = q_base + i * tq
    kf = float(k_sel)

    def lanes(g):
        return slice(g * LANES, (g + 1) * LANES)

    def slab_reduce(fn, combine, init):
        acc = jnp.full((COUNT_ROWS, tq), init, F32)
        for r in range(sk // COUNT_ROWS):
            acc = combine(acc, fn(sct_scr[r * COUNT_ROWS:(r + 1) * COUNT_ROWS, :]))
        return acc

    def count(pred):
        part = slab_reduce(lambda t: jnp.where(pred(t), 1.0, 0.0), jnp.add, 0.0)
        return jnp.sum(part, axis=0, keepdims=True)

    w_t = wit_ref[0]
    t_pos = q0 + lax.broadcasted_iota(jnp.int32, (1, tq), 1)
    limit = (t_pos // CHUNK + 1) * CHUNK
    key_in_tile = lax.broadcasted_iota(jnp.int32, (KEY_TILE, tq), 0)
    for kt in range(n_kt):
        ks = slice(kt * KEY_TILE, (kt + 1) * KEY_TILE)
        ki_t = ki_ref[0, ks, :]
        acc = jnp.zeros((KEY_TILE, tq), F32)
        for h in range(N_IDX_HEADS):
            lg = _dot_nt(ki_t, qi_ref[0, :, h * IDX_DIM:(h + 1) * IDX_DIM])
            acc = acc + jnp.maximum(lg, 0.0) * w_t[h:h + 1, :]
        sct_scr[ks, :] = jnp.where(key_in_tile < limit - kt * KEY_TILE, acc, NEG_INF)

    smax = jnp.max(slab_reduce(lambda t: t, jnp.maximum, NEG_INF), axis=0, keepdims=True)
    smin = jnp.min(slab_reduce(lambda t: jnp.where(t == NEG_INF, jnp.inf, t), jnp.minimum, jnp.inf),
                   axis=0, keepdims=True)
    n_adm = limit.astype(F32)
    c_max = count(lambda t: t >= smax)
    c_pos = count(lambda t: t > 0.0)
    c_nn = count(lambda t: t >= 0.0)
    zero = jnp.zeros_like(smax)
    at_zero = (c_pos < kf) & (c_nn >= kf)
    below_zero = c_nn < kf
    lo = jnp.where(at_zero | ~below_zero, zero, smin)
    clo = jnp.where(at_zero | ~below_zero, c_nn, n_adm)
    hi = jnp.where(at_zero | below_zero, zero, smax)
    few = n_adm <= kf
    lo = jnp.where(few, smin, lo)
    clo = jnp.where(few, n_adm, clo)
    top_tied = (c_max >= kf) & ~few
    lo = jnp.where(top_tied, smax, lo)
    clo = jnp.where(top_tied, c_max, clo)
    hi = jnp.where(top_tied, smax, hi)
    done0 = jnp.where(few | at_zero | top_tied, 1.0, 0.0)

    def bisect(carry):
        lo, hi, clo, done, it = carry
        for _ in range(BISECT_STEPS_PER_CHECK):
            mid = 0.5 * lo + 0.5 * hi
            stuck = (mid <= lo) | (mid >= hi)
            c = count(lambda t: t >= mid)
            ge = c >= kf
            lo = jnp.where(ge, mid, lo)
            clo = jnp.where(ge, c, clo)
            hi = jnp.where(ge, hi, mid)
            done = jnp.where(stuck | (clo <= kf), 1.0, done)
        return lo, hi, clo, done, it + 1

    def not_converged(carry):
        _, _, _, done, it = carry
        return (jnp.min(done) < 0.5) & (it < BISECT_MAX_CHECKS)

    lo, hi, clo, _, _ = lax.while_loop(not_converged, bisect, (lo, hi, clo, done0, jnp.int32(0)))

    thr = lo
    c_gt = count(lambda t: t > thr)
    c_eq = count(lambda t: t == thr)
    need = kf - c_gt
    tie_cut = jnp.max(c_gt + c_eq - kf) > 0.0

    def with_ties():
        r_i = lax.broadcasted_iota(jnp.int32, (KEY_TILE, KEY_TILE), 0)
        c_i = lax.broadcasted_iota(jnp.int32, (KEY_TILE, KEY_TILE), 1)
        earlier = jnp.where(c_i < r_i, 1.0, 0.0).astype(BF16)
        carry = jnp.zeros((1, tq), F32)
        for kt in range(n_kt):
            ks = slice(kt * KEY_TILE, (kt + 1) * KEY_TILE)
            sc = sct_scr[ks, :]
            eq = sc == thr
            eq_f = jnp.where(eq, 1.0, 0.0)
            rank = _dot(earlier, eq_f.astype(BF16)) + carry
            keep = (sc > thr) | (eq & (rank < need))
            msk_scr[:, ks] = jnp.where(keep, 0.0, NEG_INF).T
            carry = carry + jnp.sum(eq_f, axis=0, keepdims=True)

    def without_ties():
        for kt in range(n_kt):
            ks = slice(kt * KEY_TILE, (kt + 1) * KEY_TILE)
            msk_scr[:, ks] = jnp.where(sct_scr[ks, :] >= thr, 0.0, NEG_INF).T

    lax.cond(tie_cut, with_ties, without_ties)

    heads = [slice(h * HEAD_DIM, (h + 1) * HEAD_DIM) for h in range(N_HEADS)]
    row_max = []
    for h, sl in enumerate(heads):
        qh = qb_ref[0, :, sl]
        mx = jnp.full((tq, LANES), NEG_INF, F32)
        for kt in range(n_kt):
            ks = slice(kt * KEY_TILE, (kt + 1) * KEY_TILE)
            a = _dot_nt(qh, kb_ref[0, ks, sl]) + msk_scr[:, ks]
            a_scr[h, :, ks] = a
            for g in range(KEY_TILE // LANES):
                mx = jnp.maximum(mx, a[:, lanes(g)])
        row_max.append(jnp.max(mx, axis=-1, keepdims=True))
    row_sum = []
    for h in range(N_HEADS):
        m_b = jnp.broadcast_to(row_max[h], (tq, LANES))
        ls = jnp.zeros((tq, LANES), F32)
        for g in range(n_lg):
            p = jnp.exp(a_scr[h, :, lanes(g)] - m_b)
            ls = ls + p
            p_scr[h, :, lanes(g)] = p.astype(BF16)
        row_sum.append(jnp.sum(ls, axis=-1, keepdims=True))
    for h, sl in enumerate(heads):
        o = _dot(p_scr[h], vb_ref[0, :, sl]) / row_sum[h]
        o_ref[0, :, sl] = o.astype(BF16)


def _dsa_class(qi, wit, qb, ki, kb, vb, tq, k_sel, q_base, q_len, sk):
    bsz = qb.shape[0]
    blk0 = q_base // tq
    row = lambda b, i: (b, blk0 + i, 0)
    keys = lambda b, i: (b, 0, 0)
    return pl.pallas_call(
        functools.partial(_dsa_kernel, tq=tq, k_sel=k_sel, q_base=q_base),
        out_shape=jax.ShapeDtypeStruct((bsz, q_len, WIDTH), BF16),
        grid=(bsz, q_len // tq),
        in_specs=[
            pl.BlockSpec((1, tq, N_IDX_HEADS * IDX_DIM), row),
            pl.BlockSpec((1, N_IDX_HEADS, tq), lambda b, i: (b, 0, blk0 + i)),
            pl.BlockSpec((1, tq, WIDTH), row),
            pl.BlockSpec((1, sk, IDX_DIM), keys),
            pl.BlockSpec((1, sk, WIDTH), keys),
            pl.BlockSpec((1, sk, WIDTH), keys),
        ],
        out_specs=pl.BlockSpec((1, tq, WIDTH), lambda b, i: (b, i, 0)),
        scratch_shapes=[pltpu.VMEM((sk, tq), F32), pltpu.VMEM((tq, sk), F32),
                        pltpu.VMEM((N_HEADS, tq, sk), F32), pltpu.VMEM((N_HEADS, tq, sk), BF16)],
        compiler_params=_cparams(("arbitrary", "arbitrary")),
        name=f"dsa_k{sk}",
    )(qi, wit, qb, ki, kb, vb)


def _dsa(qi, wit, qb, ki, kb, vb, tq):
    s = qb.shape[1]
    k_sel = min(TOPK_MAX, s // 4)
    n_cls = max(1, min(DSA_KEY_CLASSES, s // KEY_TILE))
    q_len = s // n_cls
    outs = [_dsa_class(qi, wit, qb, ki, kb, vb, tq, k_sel, c * q_len, q_len, (c + 1) * q_len)
            for c in range(n_cls)]
    return jnp.concatenate(outs, axis=1)


def _merge_kernel(oa_ref, ob_ref, ga_ref, gb_ref, x_ref, mod_ref, wba_ref, wbb_ref, wo_ref,
                  ln1_ref, wr_ref, rb_ref, x1_ref, u2_ref, cc_ref, msk_scr, *, alpha):
    tm = x_ref.shape[1]
    ya = _dot(oa_ref[0], wba_ref[...])
    yb = _dot(ob_ref[0], wbb_ref[...])
    merged = _sigmoid(ga_ref[0].astype(F32)) * ya + _sigmoid(gb_ref[0].astype(F32)) * yb
    mix = _dot(merged.astype(BF16), wo_ref[...])
    mod = mod_ref[0]
    g1 = mod[2:3, :]
    sh2 = mod[3:4, :]
    sc2 = mod[4:5, :]
    ln1 = ln1_ref[...]
    x1 = _ln(alpha * x_ref[0] + g1 * mix) * ln1[0:1, :] + ln1[1:2, :]
    x1_ref[0] = x1
    u2 = _ln(x1) * (1.0 + sc2) + sh2
    u2_ref[0] = u2.astype(BF16)

    u_hi, u_lo = _split_bf16(u2)
    wr_hi = wr_ref[0]
    wr_lo = wr_ref[1]
    logits = _dot(u_hi, wr_hi) + _dot(u_lo, wr_hi) + _dot(u_hi, wr_lo)
    aff = _sigmoid(logits.T[0:N_EXPERTS, :])
    biased = aff + rb_ref[...]
    grp = biased.reshape(N_GROUPS, GROUP_SIZE, tm)
    sub = lax.broadcasted_iota(jnp.int32, grp.shape, 1)
    m1 = jnp.max(grp, axis=1, keepdims=True)
    first = jnp.min(jnp.where(grp == m1, sub, GROUP_SIZE), axis=1, keepdims=True)
    m2 = jnp.max(jnp.where(sub == first, NEG_INF, grp), axis=1, keepdims=True)
    gscore = (m1 + m2).reshape(N_GROUPS, tm)
    g_i = lax.broadcasted_iota(jnp.int32, (N_GROUPS, tm), 0)
    g_rank = jnp.zeros((N_GROUPS, tm), F32)
    for g in range(N_GROUPS):
        other = gscore[g:g + 1, :]
        beats = (other > gscore) | ((other == gscore) & (g < g_i))
        g_rank = g_rank + jnp.where(beats, 1.0, 0.0)
    g_keep = jnp.where(g_rank < TOPK_GROUPS, 1.0, 0.0).reshape(N_GROUPS, 1, tm)
    e_keep = jnp.broadcast_to(g_keep, (N_GROUPS, GROUP_SIZE, tm)).reshape(N_EXPERTS, tm)
    masked = jnp.where(e_keep > 0.5, biased, NEG_INF)
    msk_scr[...] = masked
    e_i = lax.broadcasted_iota(jnp.int32, (N_EXPERTS, tm), 0)

    def rank_step(e, rank):
        other = msk_scr[pl.ds(e, 1), :]
        beats = (other > masked) | ((other == masked) & (e < e_i))
        return rank + jnp.where(beats, 1.0, 0.0)

    e_rank = lax.fori_loop(0, N_EXPERTS, rank_step, jnp.zeros((N_EXPERTS, tm), F32))
    top_aff = jnp.where(e_rank < TOPK_EXPERTS, aff, 0.0)
    comb_t = top_aff / jnp.sum(top_aff, axis=0, keepdims=True) * ROUTED_SCALE
    comb = jnp.concatenate([comb_t, jnp.zeros_like(comb_t)], axis=0).T
    c_hi = comb.astype(BF16).astype(F32)
    c_lo = comb - c_hi
    cc_ref[0] = (c_hi + pltpu.roll(c_lo, N_EXPERTS, 1)).astype(BF16)


def _merge(oa, ob, ga, gb, x, mod, wba, wbb, wo, ln1, wr, rb, tm, alpha):
    bsz, s, d = x.shape
    row = lambda b, i: (b, i, 0)
    w2 = lambda shape: pl.BlockSpec(shape, lambda b, i: (0,) * len(shape))
    return pl.pallas_call(
        functools.partial(_merge_kernel, alpha=alpha),
        out_shape=[jax.ShapeDtypeStruct((bsz, s, d), F32),
                   jax.ShapeDtypeStruct((bsz, s, d), BF16),
                   jax.ShapeDtypeStruct((bsz, s, 2 * N_EXPERTS), BF16)],
        grid=(bsz, s // tm),
        in_specs=[
            pl.BlockSpec((1, tm, WIDTH), row), pl.BlockSpec((1, tm, WIDTH), row),
            pl.BlockSpec((1, tm, d), row), pl.BlockSpec((1, tm, d), row),
            pl.BlockSpec((1, tm, d), row),
            pl.BlockSpec((1, N_MOD, d), lambda b, i: (b, 0, 0)),
            w2(wba.shape), w2(wbb.shape), w2(wo.shape), w2(ln1.shape), w2(wr.shape), w2(rb.shape),
        ],
        out_specs=[pl.BlockSpec((1, tm, d), row), pl.BlockSpec((1, tm, d), row),
                   pl.BlockSpec((1, tm, 2 * N_EXPERTS), row)],
        scratch_shapes=[pltpu.VMEM((N_EXPERTS, tm), F32)],
        compiler_params=_cparams(("arbitrary", "arbitrary")),
        name="merge",
    )(oa, ob, ga, gb, x, mod, wba, wbb, wo, ln1, wr, rb)


def _silu(z):
    return z * _sigmoid(z)


def _moe_kernel(u2_ref, cc_ref, x1_ref, mod_ref, wg_ref, wu_ref, wd_ref, wsg_ref, wsu_ref, wsd_ref,
                ln2_ref, o_ref, *, alpha):
    g = pl.program_id(1)
    t = u2_ref[...]
    d_exp = wg_ref.shape[-1]

    @pl.when(g == 0)
    def _():
        hs = _silu(_dot(t, wsg_ref[...])) * _dot(t, wsu_ref[...])
        o_ref[...] = _dot(hs.astype(BF16), wsd_ref[...])

    width = EXPERTS_PER_STEP * d_exp
    r_i = lax.broadcasted_iota(jnp.int32, (2 * N_EXPERTS, width), 0)
    c_i = lax.broadcasted_iota(jnp.int32, (2 * N_EXPERTS, width), 1)
    pick = jnp.where((r_i % N_EXPERTS) == g * EXPERTS_PER_STEP + c_i // d_exp, 1.0, 0.0).astype(BF16)
    gate = _dot(cc_ref[...], pick)

    hs = [_silu(_dot(t, wg_ref[j])) * _dot(t, wu_ref[j]) for j in range(EXPERTS_PER_STEP)]
    h = jnp.concatenate(hs, axis=1) * gate
    wd = wd_ref[...].reshape(width, wd_ref.shape[-1])
    o_ref[...] += _dot(h.astype(BF16), wd)

    @pl.when(g == pl.num_programs(1) - 1)
    def _():
        mod = mod_ref[0]
        g2 = mod[5:6, :]
        ln2 = ln2_ref[...]
        o_ref[...] = _ln(alpha * x1_ref[...] + g2 * o_ref[...]) * ln2[0:1, :] + ln2[1:2, :]


def _moe(u2, cc, x1, mod, wg, wu, wd, wsg, wsu, wsd, ln2, tm, tokens_per_batch, alpha):
    t, d = u2.shape
    n_exp, _, d_exp = wg.shape
    eps = EXPERTS_PER_STEP
    blocks_per_batch = tokens_per_batch // tm
    row = lambda i, g: (i, 0)
    w2 = lambda shape: pl.BlockSpec(shape, lambda i, g: (0,) * len(shape))
    return pl.pallas_call(
        functools.partial(_moe_kernel, alpha=alpha),
        out_shape=jax.ShapeDtypeStruct((t, d), F32),
        grid=(t // tm, n_exp // eps),
        in_specs=[
            pl.BlockSpec((tm, d), row),
            pl.BlockSpec((tm, 2 * N_EXPERTS), row),
            pl.BlockSpec((tm, d), row),
            pl.BlockSpec((1, N_MOD, d), lambda i, g: (i // blocks_per_batch, 0, 0)),
            pl.BlockSpec((eps, d, d_exp), lambda i, g: (g, 0, 0)),
            pl.BlockSpec((eps, d, d_exp), lambda i, g: (g, 0, 0)),
            pl.BlockSpec((eps, d_exp, d), lambda i, g: (g, 0, 0)),
            w2(wsg.shape), w2(wsu.shape), w2(wsd.shape), w2(ln2.shape),
        ],
        out_specs=pl.BlockSpec((tm, d), row),
        compiler_params=_cparams(("arbitrary", "arbitrary")),
        name="moe",
    )(u2, cc, x1, mod, wg, wu, wd, wsg, wsu, wsd, ln2)


def _rope_tables(positions):
    inv_freq = ROPE_THETA ** (-jnp.arange(0, ROPE_DIM, 2, dtype=F32) / ROPE_DIM)
    ang = positions.astype(F32)[..., None] * inv_freq
    cos, sin = jnp.cos(ang), jnp.sin(ang)
    ones = jnp.ones(cos.shape[:-1] + (HEAD_DIM - ROPE_DIM,), F32)
    zeros = jnp.zeros_like(ones)
    zh = jnp.zeros_like(sin)
    c = jnp.concatenate([cos, cos, ones], axis=-1)
    s1 = jnp.concatenate([-sin, zh, zeros], axis=-1)
    s2 = jnp.concatenate([zh, sin, zeros], axis=-1)
    reps = LANES // HEAD_DIM
    return jnp.concatenate([jnp.tile(c, reps), jnp.tile(s1, reps), jnp.tile(s2, reps)], axis=-1)


def _band_bias_table(rel_bias, tq):
    n_heads = rel_bias.shape[0]
    win = tq + BAND_PAD
    row_len = win + tq
    n_high = BAND_PAD - MAX_REL + 1
    n_ramp = 2 * MAX_REL - 1
    high = rel_bias[:, 2 * MAX_REL:]
    profile = jnp.concatenate([
        jnp.broadcast_to(high, (n_heads, n_high)),
        rel_bias[:, n_ramp:0:-1],
        jnp.broadcast_to(rel_bias[:, :1], (n_heads, win - n_high - n_ramp)),
        jnp.broadcast_to(high, (n_heads, tq)),
    ], axis=1)
    skew = jnp.broadcast_to(profile[:, None, :], (n_heads, tq, row_len)).reshape(n_heads, tq * row_len)
    skew = skew[:, :tq * (row_len - 1)].reshape(n_heads, tq, row_len - 1)[:, :, :win]
    qi = np.arange(tq)[:, None]
    kj = np.arange(win)[None, :]
    q_chunk = qi // CHUNK
    k_chunk = kj // CHUNK - (BAND_CHUNKS - 1)
    in_band = (k_chunk <= q_chunk) & (k_chunk >= q_chunk - (BAND_CHUNKS - 1))
    return jnp.where(jnp.asarray(in_band)[None], skew, NEG_INF)


def kernel(x, c, positions, w_ada, b_ada, w_in, rel_bias, idx_k_norm_g, idx_k_norm_b, w_branch_a,
           w_branch_b, w_out, ln1_g, ln1_b, w_router, router_bias, w_exp_gate, w_exp_up, w_exp_down,
           w_sh_gate, w_sh_up, w_sh_down, ln2_g, ln2_b):
    bsz, s, d = x.shape
    depth = w_ada.shape[0]
    alpha = (2.0 * depth) ** 0.25
    tm_proj = min(512, s)
    tq_band = min(256, s)
    tq_dsa = min(128, s)
    tm_merge = min(512, s)
    tm_moe = min(1024, s)

    rope_tab = _rope_tables(positions)
    n7 = 7 * WIDTH
    for l in range(depth):
        mod = _ada(c, w_ada[l], b_ada[l]).reshape(bsz, N_MOD, d)
        w_l = w_in[l]
        w7 = w_l[:, :n7].astype(BF16)
        n_kw = IDX_DIM + N_IDX_HEADS
        wkw = jnp.pad(w_l[:, n7:n7 + n_kw], ((0, 0), (0, LANES - n_kw))).astype(BF16)
        wg = w_l[:, n7 + n_kw:].astype(BF16)
        lnk = jnp.pad(jnp.stack([idx_k_norm_g[l], idx_k_norm_b[l]]), ((0, 0), (0, LANES - IDX_DIM)))
        qa, ka, va, qb, kb, vb, qi, ki, wi, ga, gb = _proj(x, mod, rope_tab, w7, wkw, wg, lnk, tm_proj)

        oa = _band(qa, ka, va, _band_bias_table(rel_bias[l], tq_band), tq_band)
        ob = _dsa(qi, wi, qb, ki, kb, vb, tq_dsa)

        wr = jnp.pad(w_router[l], ((0, 0), (0, LANES - N_EXPERTS)))
        wr_hi = wr.astype(BF16)
        wr_lo = (wr - wr_hi.astype(F32)).astype(BF16)
        x1, u2, cc = _merge(
            oa, ob, ga, gb, x, mod,
            w_branch_a[l].astype(BF16), w_branch_b[l].astype(BF16), w_out[l].astype(BF16),
            jnp.stack([ln1_g[l], ln1_b[l]]), jnp.stack([wr_hi, wr_lo]),
            router_bias[l].reshape(N_EXPERTS, 1), tm_merge, alpha)

        t = bsz * s
        out = _moe(
            u2.reshape(t, d), cc.reshape(t, 2 * N_EXPERTS), x1.reshape(t, d), mod,
            w_exp_gate[l].astype(BF16), w_exp_up[l].astype(BF16), w_exp_down[l].astype(BF16),
            w_sh_gate[l].astype(BF16), w_sh_up[l].astype(BF16), w_sh_down[l].astype(BF16),
            jnp.stack([ln2_g[l], ln2_b[l]]), tm_moe, s, alpha)
        x = out.reshape(bsz, s, d)
    return x
```

```python
import functools

import jax
import jax.numpy as jnp
import numpy as np
from jax import lax
from jax.experimental import pallas as pl
from jax.experimental.pallas import tpu as pltpu

F32 = jnp.float32
BF16 = jnp.bfloat16
NEG_INF = float("-inf")

CHUNK = 64
HEAD_DIM = 64
N_HEADS = 8
WIDTH = N_HEADS * HEAD_DIM
BAND_CHUNKS = 9
BAND_PAD = (BAND_CHUNKS - 1) * CHUNK
MAX_REL = 128
ROPE_THETA = 500000.0
ROPE_DIM = HEAD_DIM // 4
ROPE_HALF = ROPE_DIM // 2
N_IDX_HEADS = 8
IDX_DIM = 64
TOPK_MAX = 256
N_EXPERTS = 64
N_GROUPS = 8
GROUP_SIZE = N_EXPERTS // N_GROUPS
TOPK_GROUPS = 4
TOPK_EXPERTS = 8
ROUTED_SCALE = 1.0
N_MOD = 6
LN_EPS = 1e-5
LANES = 128
KEY_TILE = 256

VMEM_LIMIT = 56 * 1024 * 1024

EXPERTS_PER_STEP = 4
DSA_KEY_CLASSES = 8


def _cparams(sem):
    return pltpu.CompilerParams(dimension_semantics=sem, vmem_limit_bytes=VMEM_LIMIT)


def _ln(z):
    mu = jnp.mean(z, axis=-1, keepdims=True)
    zc = z - mu
    var = jnp.mean(zc * zc, axis=-1, keepdims=True)
    return zc * lax.rsqrt(var + LN_EPS)


def _sigmoid(z):
    return 1.0 / (1.0 + jnp.exp(-z))


def _dot(a, b):
    return jnp.dot(a, b, preferred_element_type=F32)


def _dot_nt(a, b):
    return lax.dot_general(a, b, (((1,), (1,)), ((), ())), preferred_element_type=F32)


def _split_bf16(z):
    hi = z.astype(BF16)
    lo = (z - hi.astype(F32)).astype(BF16)
    return hi, lo


def _ada_kernel(c_ref, w_ref, b_ref, o_ref):
    c = c_ref[...]
    ca = c * _sigmoid(c)
    ca_hi, ca_lo = _split_bf16(ca)
    w = w_ref[...]
    w_hi, w_lo = _split_bf16(w)
    acc = _dot(ca_hi, w_hi) + _dot(ca_lo, w_hi) + _dot(ca_hi, w_lo)
    o_ref[...] = acc + b_ref[...]


def _ada(c, w_ada, b_ada):
    bsz, d = c.shape
    n = w_ada.shape[1]
    tn = 1024
    return pl.pallas_call(
        _ada_kernel,
        out_shape=jax.ShapeDtypeStruct((bsz, n), F32),
        grid=(n // tn,),
        in_specs=[
            pl.BlockSpec((bsz, d), lambda j: (0, 0)),
            pl.BlockSpec((d, tn), lambda j: (0, j)),
            pl.BlockSpec((1, tn), lambda j: (0, j)),
        ],
        out_specs=pl.BlockSpec((bsz, tn), lambda j: (0, j)),
        compiler_params=_cparams(("arbitrary",)),
        name="ada",
    )(c, w_ada, b_ada.reshape(1, n))


def _rope(z, c_t, s1_t, s2_t):
    n = z.shape[-1]
    return z * c_t + pltpu.roll(z, n - ROPE_HALF, 1) * s1_t + pltpu.roll(z, ROPE_HALF, 1) * s2_t


def _proj_kernel(x_ref, mod_ref, rope_ref, w7_ref, wkw_ref, wg_ref, lnk_ref,
                 qa_ref, ka_ref, va_ref, qb_ref, kb_ref, vb_ref, qi_ref, ki_ref, wi_ref,
                 ga_ref, gb_ref):
    x = x_ref[0]
    mod = mod_ref[0]
    sh1 = mod[0:1, :]
    sc1 = mod[1:2, :]
    u = (_ln(x) * (1.0 + sc1) + sh1).astype(BF16)

    rope = rope_ref[0]
    c1 = rope[:, 0:LANES]
    s1 = rope[:, LANES:2 * LANES]
    s2 = rope[:, 2 * LANES:3 * LANES]
    reps = WIDTH // LANES
    c_t = jnp.concatenate([c1] * reps, axis=1)
    s1_t = jnp.concatenate([s1] * reps, axis=1)
    s2_t = jnp.concatenate([s2] * reps, axis=1)

    att_scale = HEAD_DIM ** -0.5
    idx_scale = IDX_DIM ** -0.5

    def seg(k):
        return _dot(u, w7_ref[:, k * WIDTH:(k + 1) * WIDTH])

    qa_ref[0] = (seg(0) * att_scale).astype(BF16)
    ka_ref[0] = seg(1).astype(BF16)
    va_ref[0] = seg(2).astype(BF16)
    qb_ref[0] = (_rope(seg(3), c_t, s1_t, s2_t) * att_scale).astype(BF16)
    kb_ref[0] = _rope(seg(4), c_t, s1_t, s2_t).astype(BF16)
    vb_ref[0] = seg(5).astype(BF16)
    qi_ref[0] = (_rope(seg(6), c_t, s1_t, s2_t) * idx_scale).astype(BF16)

    z = _dot(u, wkw_ref[...])
    lane = lax.broadcasted_iota(jnp.int32, z.shape, 1)
    is_k = lane < IDX_DIM
    mu = jnp.sum(jnp.where(is_k, z, 0.0), axis=-1, keepdims=True) * (1.0 / IDX_DIM)
    zc = jnp.where(is_k, z - mu, 0.0)
    var = jnp.sum(zc * zc, axis=-1, keepdims=True) * (1.0 / IDX_DIM)
    lnk = lnk_ref[...]
    y = zc * lax.rsqrt(var + LN_EPS) * lnk[0:1, :] + lnk[1:2, :]
    y = _rope(y, c1, jnp.where(is_k, s1, 0.0), jnp.where(is_k, s2, 0.0))
    ki_ref[0] = y[:, 0:IDX_DIM].astype(BF16)
    wi_ref[0] = z.T[IDX_DIM:IDX_DIM + N_IDX_HEADS, :] * (N_IDX_HEADS ** -0.5)

    d = ga_ref.shape[-1]
    ga_ref[0] = _dot(u, wg_ref[:, 0:d]).astype(BF16)
    gb_ref[0] = _dot(u, wg_ref[:, d:2 * d]).astype(BF16)


def _proj(x, mod, rope_tab, w7, wkw, wg, lnk, tm):
    bsz, s, d = x.shape
    const = dict(pipeline_mode=pl.Buffered(1))
    row = lambda b, i: (b, i, 0)
    wspec = lambda shape: pl.BlockSpec(shape, lambda b, i: (0, 0), **const)
    out_w = jax.ShapeDtypeStruct((bsz, s, WIDTH), BF16)
    out_d = jax.ShapeDtypeStruct((bsz, s, d), BF16)
    return pl.pallas_call(
        _proj_kernel,
        out_shape=[out_w] * 7 + [
            jax.ShapeDtypeStruct((bsz, s, IDX_DIM), BF16),
            jax.ShapeDtypeStruct((bsz, N_IDX_HEADS, s), F32),
            out_d, out_d],
        grid=(bsz, s // tm),
        in_specs=[
            pl.BlockSpec((1, tm, d), row),
            pl.BlockSpec((1, N_MOD, d), lambda b, i: (b, 0, 0)),
            pl.BlockSpec((1, tm, 3 * LANES), row),
            wspec(w7.shape), wspec(wkw.shape), wspec(wg.shape), wspec(lnk.shape),
        ],
        out_specs=[pl.BlockSpec((1, tm, WIDTH), row)] * 7 + [
            pl.BlockSpec((1, tm, IDX_DIM), row),
            pl.BlockSpec((1, N_IDX_HEADS, tm), lambda b, i: (b, 0, i)),
            pl.BlockSpec((1, tm, d), row), pl.BlockSpec((1, tm, d), row)],
        compiler_params=_cparams(("arbitrary", "arbitrary")),
        name="proj",
    )(x, mod, rope_tab, w7, wkw, wg, lnk)


def _band_kernel(q_ref, k_ref, v_ref, bias_ref, o_ref, kpad, vpad, a_scr, p_scr, *, tq):
    i = pl.program_id(1)
    s = k_ref.shape[1]
    win = tq + BAND_PAD

    @pl.when(i == 0)
    def _():
        zeros = jnp.zeros((BAND_PAD, WIDTH), BF16)
        kpad[0:BAND_PAD, :] = zeros
        vpad[0:BAND_PAD, :] = zeros
        kpad[BAND_PAD:BAND_PAD + s, :] = k_ref[0]
        vpad[BAND_PAD:BAND_PAD + s, :] = v_ref[0]

    q0 = pl.multiple_of(i * tq, tq)
    n_lg = win // LANES
    col = lax.broadcasted_iota(jnp.int32, (tq, LANES), 1)
    first_frame = BAND_PAD - q0
    heads = [slice(h * HEAD_DIM, (h + 1) * HEAD_DIM) for h in range(N_HEADS)]

    def lanes(g):
        return slice(g * LANES, (g + 1) * LANES)

    row_max = []
    for h, sl in enumerate(heads):
        sc = _dot_nt(q_ref[0, :, sl], kpad[pl.ds(q0, win), sl])
        mx = jnp.full((tq, LANES), NEG_INF, F32)
        for g in range(n_lg):
            a = jnp.where(col >= first_frame - g * LANES, sc[:, lanes(g)] + bias_ref[h, :, lanes(g)], NEG_INF)
            a_scr[h, :, lanes(g)] = a
            mx = jnp.maximum(mx, a)
        row_max.append(jnp.max(mx, axis=-1, keepdims=True))
    row_sum = []
    for h in range(N_HEADS):
        m_b = jnp.broadcast_to(row_max[h], (tq, LANES))
        ls = jnp.zeros((tq, LANES), F32)
        for g in range(n_lg):
            p = jnp.exp(a_scr[h, :, lanes(g)] - m_b)
            ls = ls + p
            p_scr[h, :, lanes(g)] = p.astype(BF16)
        row_sum.append(jnp.sum(ls, axis=-1, keepdims=True))
    for h, sl in enumerate(heads):
        o = _dot(p_scr[h], vpad[pl.ds(q0, win), sl]) / row_sum[h]
        o_ref[0, :, sl] = o.astype(BF16)


def _band(qa, ka, va, bias_tab, tq):
    bsz, s, _ = qa.shape
    win = tq + BAND_PAD
    return pl.pallas_call(
        functools.partial(_band_kernel, tq=tq),
        out_shape=jax.ShapeDtypeStruct((bsz, s, WIDTH), BF16),
        grid=(bsz, s // tq),
        in_specs=[
            pl.BlockSpec((1, tq, WIDTH), lambda b, i: (b, i, 0)),
            pl.BlockSpec((1, s, WIDTH), lambda b, i: (b, 0, 0)),
            pl.BlockSpec((1, s, WIDTH), lambda b, i: (b, 0, 0)),
            pl.BlockSpec((N_HEADS, tq, win), lambda b, i: (0, 0, 0)),
        ],
        out_specs=pl.BlockSpec((1, tq, WIDTH), lambda b, i: (b, i, 0)),
        scratch_shapes=[pltpu.VMEM((BAND_PAD + s, WIDTH), BF16),
                        pltpu.VMEM((BAND_PAD + s, WIDTH), BF16),
                        pltpu.VMEM((N_HEADS, tq, win), F32),
                        pltpu.VMEM((N_HEADS, tq, win), BF16)],
        compiler_params=_cparams(("arbitrary", "arbitrary")),
        name="band",
    )(qa, ka, va, bias_tab)


BISECT_STEPS_PER_CHECK = 4
BISECT_MAX_CHECKS = 400
COUNT_ROWS = 64


def _dsa_kernel(qi_ref, wit_ref, qb_ref, ki_ref, kb_ref, vb_ref, o_ref, sct_scr, msk_scr, a_scr, p_scr,
                *, tq, k_sel, q_base):
    i = pl.program_id(1)
    sk = ki_ref.shape[1]
    n_kt = sk // KEY_TILE
    n_lg = sk // LANES
    q0 = q_base + i * tq
    kf = float(k_sel)

    def lanes(g):
        return slice(g * LANES, (g + 1) * LANES)

    def slab_reduce(fn, combine, init):
        acc = jnp.full((COUNT_ROWS, tq), init, F32)
        for r in range(sk // COUNT_ROWS):
            acc = combine(acc, fn(sct_scr[r * COUNT_ROWS:(r + 1) * COUNT_ROWS, :]))
        return acc

    def count(pred):
        part = slab_reduce(lambda t: jnp.where(pred(t), 1.0, 0.0), jnp.add, 0.0)
        return jnp.sum(part, axis=0, keepdims=True)

    w_t = wit_ref[0]
    t_pos = q0 + lax.broadcasted_iota(jnp.int32, (1, tq), 1)
    limit = (t_pos // CHUNK + 1) * CHUNK
    key_in_tile = lax.broadcasted_iota(jnp.int32, (KEY_TILE, tq), 0)
    for kt in range(n_kt):
        ks = slice(kt * KEY_TILE, (kt + 1) * KEY_TILE)
        ki_t = ki_ref[0, ks, :]
        acc = jnp.zeros((KEY_TILE, tq), F32)
        for h in range(N_IDX_HEADS):
            lg = _dot_nt(ki_t, qi_ref[0, :, h * IDX_DIM:(h + 1) * IDX_DIM])
            acc = acc + jnp.maximum(lg, 0.0) * w_t[h:h + 1, :]
        sct_scr[ks, :] = jnp.where(key_in_tile < limit - kt * KEY_TILE, acc, NEG_INF)

    smax = jnp.max(slab_reduce(lambda t: t, jnp.maximum, NEG_INF), axis=0, keepdims=True)
    smin = jnp.min(slab_reduce(lambda t: jnp.where(t == NEG_INF, jnp.inf, t), jnp.minimum, jnp.inf),
                   axis=0, keepdims=True)
    n_adm = limit.astype(F32)
    c_max = count(lambda t: t >= smax)
    c_pos = count(lambda t: t > 0.0)
    c_nn = count(lambda t: t >= 0.0)
    zero = jnp.zeros_like(smax)
    at_zero = (c_pos < kf) & (c_nn >= kf)
    below_zero = c_nn < kf
    lo = jnp.where(at_zero | ~below_zero, zero, smin)
    clo = jnp.where(at_zero | ~below_zero, c_nn, n_adm)
    hi = jnp.where(at_zero | below_zero, zero, smax)
    few = n_adm <= kf
    lo = jnp.where(few, smin, lo)
    clo = jnp.where(few, n_adm, clo)
    top_tied = (c_max >= kf) & ~few
    lo = jnp.where(top_tied, smax, lo)
    clo = jnp.where(top_tied, c_max, clo)
    hi = jnp.where(top_tied, smax, hi)
    done0 = jnp.where(few | at_zero | top_tied, 1.0, 0.0)

    def bisect(carry):
        lo, hi, clo, done, it = carry
        for _ in range(BISECT_STEPS_PER_CHECK):
            mid = 0.5 * lo + 0.5 * hi
            stuck = (mid <= lo) | (mid >= hi)
            c = count(lambda t: t >= mid)
            ge = c >= kf
            lo = jnp.where(ge, mid, lo)
            clo = jnp.where(ge, c, clo)
            hi = jnp.where(ge, hi, mid)
            done = jnp.where(stuck | (clo <= kf), 1.0, done)
        return lo, hi, clo, done, it + 1

    def not_converged(carry):
        _, _, _, done, it = carry
        return (jnp.min(done) < 0.5) & (it < BISECT_MAX_CHECKS)

    lo, hi, clo, _, _ = lax.while_loop(not_converged, bisect, (lo, hi, clo, done0, jnp.int32(0)))

    thr = lo
    c_gt = count(lambda t: t > thr)
    c_eq = count(lambda t: t == thr)
    need = kf - c_gt
    tie_cut = jnp.max(c_gt + c_eq - kf) > 0.0

    def with_ties():
        r_i = lax.broadcasted_iota(jnp.int32, (KEY_TILE, KEY_TILE), 0)
        c_i = lax.broadcasted_iota(jnp.int32, (KEY_TILE, KEY_TILE), 1)
        earlier = jnp.where(c_i < r_i, 1.0, 0.0).astype(BF16)
        carry = jnp.zeros((1, tq), F32)
        for kt in range(n_kt):
            ks = slice(kt * KEY_TILE, (kt + 1) * KEY_TILE)
            sc = sct_scr[ks, :]
            eq = sc == thr
            eq_f = jnp.where(eq, 1.0, 0.0)
            rank = _dot(earlier, eq_f.astype(BF16)) + carry
            keep = (sc > thr) | (eq & (rank < need))
            msk_scr[:, ks] = jnp.where(keep, 0.0, NEG_INF).T
            carry = carry + jnp.sum(eq_f, axis=0, keepdims=True)

    def without_ties():
        for kt in range(n_kt):
            ks = slice(kt * KEY_TILE, (kt + 1) * KEY_TILE)
            msk_scr[:, ks] = jnp.where(sct_scr[ks, :] >= thr, 0.0, NEG_INF).T

    lax.cond(tie_cut, with_ties, without_ties)

    heads = [slice(h * HEAD_DIM, (h + 1) * HEAD_DIM) for h in range(N_HEADS)]
    row_max = []
    for h, sl in enumerate(heads):
        qh = qb_ref[0, :, sl]
        mx = jnp.full((tq, LANES), NEG_INF, F32)
        for kt in range(n_kt):
            ks = slice(kt * KEY_TILE, (kt + 1) * KEY_TILE)
            a = _dot_nt(qh, kb_ref[0, ks, sl]) + msk_scr[:, ks]
            a_scr[h, :, ks] = a
            for g in range(KEY_TILE // LANES):
                mx = jnp.maximum(mx, a[:, lanes(g)])
        row_max.append(jnp.max(mx, axis=-1, keepdims=True))
    row_sum = []
    for h in range(N_HEADS):
        m_b = jnp.broadcast_to(row_max[h], (tq, LANES))
        ls = jnp.zeros((tq, LANES), F32)
        for g in range(n_lg):
            p = jnp.exp(a_scr[h, :, lanes(g)] - m_b)
            ls = ls + p
            p_scr[h, :, lanes(g)] = p.astype(BF16)
        row_sum.append(jnp.sum(ls, axis=-1, keepdims=True))
    for h, sl in enumerate(heads):
        o = _dot(p_scr[h], vb_ref[0, :, sl]) / row_sum[h]
        o_ref[0, :, sl] = o.astype(BF16)


def _dsa_class(qi, wit, qb, ki, kb, vb, tq, k_sel, q_base, q_len, sk):
    bsz = qb.shape[0]
    blk0 = q_base // tq
    row = lambda b, i: (b, blk0 + i, 0)
    keys = lambda b, i: (b, 0, 0)
    return pl.pallas_call(
        functools.partial(_dsa_kernel, tq=tq, k_sel=k_sel, q_base=q_base),
        out_shape=jax.ShapeDtypeStruct((bsz, q_len, WIDTH), BF16),
        grid=(bsz, q_len // tq),
        in_specs=[
            pl.BlockSpec((1, tq, N_IDX_HEADS * IDX_DIM), row),
            pl.BlockSpec((1, N_IDX_HEADS, tq), lambda b, i: (b, 0, blk0 + i)),
            pl.BlockSpec((1, tq, WIDTH), row),
            pl.BlockSpec((1, sk, IDX_DIM), keys),
            pl.BlockSpec((1, sk, WIDTH), keys),
            pl.BlockSpec((1, sk, WIDTH), keys),
        ],
        out_specs=pl.BlockSpec((1, tq, WIDTH), lambda b, i: (b, i, 0)),
        scratch_shapes=[pltpu.VMEM((sk, tq), F32), pltpu.VMEM((tq, sk), F32),
                        pltpu.VMEM((N_HEADS, tq, sk), F32), pltpu.VMEM((N_HEADS, tq, sk), BF16)],
        compiler_params=_cparams(("arbitrary", "arbitrary")),
        name=f"dsa_k{sk}",
    )(qi, wit, qb, ki, kb, vb)


def _dsa(qi, wit, qb, ki, kb, vb, tq):
    s = qb.shape[1]
    k_sel = min(TOPK_MAX, s // 4)
    n_cls = max(1, min(DSA_KEY_CLASSES, s // KEY_TILE))
    q_len = s // n_cls
    outs = [_dsa_class(qi, wit, qb, ki, kb, vb, tq, k_sel, c * q_len, q_len, (c + 1) * q_len)
            for c in range(n_cls)]
    return jnp.concatenate(outs, axis=1)


def _merge_kernel(oa_ref, ob_ref, ga_ref, gb_ref, x_ref, mod_ref, wba_ref, wbb_ref, wo_ref,
                  ln1_ref, wr_ref, rb_ref, x1_ref, u2_ref, comb_ref, msk_scr, *, alpha):
    tm = x_ref.shape[1]
    ya = _dot(oa_ref[0], wba_ref[...])
    yb = _dot(ob_ref[0], wbb_ref[...])
    merged = _sigmoid(ga_ref[0].astype(F32)) * ya + _sigmoid(gb_ref[0].astype(F32)) * yb
    mix = _dot(merged.astype(BF16), wo_ref[...])
    mod = mod_ref[0]
    g1 = mod[2:3, :]
    sh2 = mod[3:4, :]
    sc2 = mod[4:5, :]
    ln1 = ln1_ref[...]
    x1 = _ln(alpha * x_ref[0] + g1 * mix) * ln1[0:1, :] + ln1[1:2, :]
    x1_ref[0] = x1
    u2 = _ln(x1) * (1.0 + sc2) + sh2
    u2_ref[0] = u2.astype(BF16)

    u_hi, u_lo = _split_bf16(u2)
    both = _dot(u_hi, wr_ref[...])
    logits = both[:, 0:LANES] + both[:, LANES:] + _dot(u_lo, wr_ref[:, 0:LANES])
    aff = _sigmoid(logits.T[0:N_EXPERTS, :])
    biased = aff + rb_ref[...]
    grp = biased.reshape(N_GROUPS, GROUP_SIZE, tm)
    sub = lax.broadcasted_iota(jnp.int32, grp.shape, 1)
    m1 = jnp.max(grp, axis=1, keepdims=True)
    first = jnp.min(jnp.where(grp == m1, sub, GROUP_SIZE), axis=1, keepdims=True)
    m2 = jnp.max(jnp.where(sub == first, NEG_INF, grp), axis=1, keepdims=True)
    gscore = (m1 + m2).reshape(N_GROUPS, tm)
    g_i = lax.broadcasted_iota(jnp.int32, (N_GROUPS, tm), 0)
    g_rank = jnp.zeros((N_GROUPS, tm), F32)
    for g in range(N_GROUPS):
        other = gscore[g:g + 1, :]
        beats = (other > gscore) | ((other == gscore) & (g < g_i))
        g_rank = g_rank + jnp.where(beats, 1.0, 0.0)
    g_keep = jnp.where(g_rank < TOPK_GROUPS, 1.0, 0.0).reshape(N_GROUPS, 1, tm)
    e_keep = jnp.broadcast_to(g_keep, (N_GROUPS, GROUP_SIZE, tm)).reshape(N_EXPERTS, tm)
    masked = jnp.where(e_keep > 0.5, biased, NEG_INF)
    msk_scr[...] = masked
    e_i = lax.broadcasted_iota(jnp.int32, (N_EXPERTS, tm), 0)

    def rank_step(e, rank):
        other = msk_scr[pl.ds(e, 1), :]
        beats = (other > masked) | ((other == masked) & (e < e_i))
        return rank + jnp.where(beats, 1.0, 0.0)

    e_rank = lax.fori_loop(0, N_EXPERTS, rank_step, jnp.zeros((N_EXPERTS, tm), F32))
    top_aff = jnp.where(e_rank < TOPK_EXPERTS, aff, 0.0)
    comb_t = top_aff / jnp.sum(top_aff, axis=0, keepdims=True) * ROUTED_SCALE
    comb_ref[0] = jnp.concatenate([comb_t, jnp.zeros_like(comb_t)], axis=0).T


def _merge(oa, ob, ga, gb, x, mod, wba, wbb, wo, ln1, wr, rb, tm, alpha):
    bsz, s, d = x.shape
    row = lambda b, i: (b, i, 0)
    w2 = lambda shape: pl.BlockSpec(shape, lambda b, i: (0,) * len(shape))
    return pl.pallas_call(
        functools.partial(_merge_kernel, alpha=alpha),
        out_shape=[jax.ShapeDtypeStruct((bsz, s, d), F32),
                   jax.ShapeDtypeStruct((bsz, s, d), BF16),
                   jax.ShapeDtypeStruct((bsz, s, LANES), F32)],
        grid=(bsz, s // tm),
        in_specs=[
            pl.BlockSpec((1, tm, WIDTH), row), pl.BlockSpec((1, tm, WIDTH), row),
            pl.BlockSpec((1, tm, d), row), pl.BlockSpec((1, tm, d), row),
            pl.BlockSpec((1, tm, d), row),
            pl.BlockSpec((1, N_MOD, d), lambda b, i: (b, 0, 0)),
            w2(wba.shape), w2(wbb.shape), w2(wo.shape), w2(ln1.shape), w2(wr.shape), w2(rb.shape),
        ],
        out_specs=[pl.BlockSpec((1, tm, d), row), pl.BlockSpec((1, tm, d), row),
                   pl.BlockSpec((1, tm, LANES), row)],
        scratch_shapes=[pltpu.VMEM((N_EXPERTS, tm), F32)],
        compiler_params=_cparams(("arbitrary", "arbitrary")),
        name="merge",
    )(oa, ob, ga, gb, x, mod, wba, wbb, wo, ln1, wr, rb)


def _silu(z):
    return z * _sigmoid(z)


def _moe_kernel(u2_ref, gate_ref, x1_ref, mod_ref, wg_ref, wu_ref, wd_ref, wsg_ref, wsu_ref, wsd_ref,
                ln2_ref, o_ref, *, alpha):
    g = pl.program_id(1)
    t = u2_ref[...]
    d_exp = wg_ref.shape[-1]

    @pl.when(g == 0)
    def _():
        hs = _silu(_dot(t, wsg_ref[...])) * _dot(t, wsu_ref[...])
        o_ref[...] = _dot(hs.astype(BF16), wsd_ref[...])

    gate = gate_ref[0]
    hs = [_silu(_dot(t, wg_ref[j])) * _dot(t, wu_ref[j]) * gate[:, j:j + 1] for j in range(EXPERTS_PER_STEP)]
    h = jnp.concatenate(hs, axis=1)
    wd = wd_ref[...].reshape(EXPERTS_PER_STEP * d_exp, wd_ref.shape[-1])
    o_ref[...] += _dot(h.astype(BF16), wd)

    @pl.when(g == pl.num_programs(1) - 1)
    def _():
        mod = mod_ref[0]
        g2 = mod[5:6, :]
        ln2 = ln2_ref[...]
        o_ref[...] = _ln(alpha * x1_ref[...] + g2 * o_ref[...]) * ln2[0:1, :] + ln2[1:2, :]


def _moe(u2, gate, x1, mod, wg, wu, wd, wsg, wsu, wsd, ln2, tm, tokens_per_batch, alpha):
    t, d = u2.shape
    n_exp, _, d_exp = wg.shape
    eps = EXPERTS_PER_STEP
    blocks_per_batch = tokens_per_batch // tm
    row = lambda i, g: (i, 0)
    w2 = lambda shape: pl.BlockSpec(shape, lambda i, g: (0,) * len(shape))
    return pl.pallas_call(
        functools.partial(_moe_kernel, alpha=alpha),
        out_shape=jax.ShapeDtypeStruct((t, d), F32),
        grid=(t // tm, n_exp // eps),
        in_specs=[
            pl.BlockSpec((tm, d), row),
            pl.BlockSpec((1, tm, eps), lambda i, g: (g, i, 0)),
            pl.BlockSpec((tm, d), row),
            pl.BlockSpec((1, N_MOD, d), lambda i, g: (i // blocks_per_batch, 0, 0)),
            pl.BlockSpec((eps, d, d_exp), lambda i, g: (g, 0, 0)),
            pl.BlockSpec((eps, d, d_exp), lambda i, g: (g, 0, 0)),
            pl.BlockSpec((eps, d_exp, d), lambda i, g: (g, 0, 0)),
            w2(wsg.shape), w2(wsu.shape), w2(wsd.shape), w2(ln2.shape),
        ],
        out_specs=pl.BlockSpec((tm, d), row),
        compiler_params=_cparams(("arbitrary", "arbitrary")),
        name="moe",
    )(u2, gate, x1, mod, wg, wu, wd, wsg, wsu, wsd, ln2)


def _rope_tables(positions):
    inv_freq = ROPE_THETA ** (-jnp.arange(0, ROPE_DIM, 2, dtype=F32) / ROPE_DIM)
    ang = positions.astype(F32)[..., None] * inv_freq
    cos, sin = jnp.cos(ang), jnp.sin(ang)
    ones = jnp.ones(cos.shape[:-1] + (HEAD_DIM - ROPE_DIM,), F32)
    zeros = jnp.zeros_like(ones)
    zh = jnp.zeros_like(sin)
    c = jnp.concatenate([cos, cos, ones], axis=-1)
    s1 = jnp.concatenate([-sin, zh, zeros], axis=-1)
    s2 = jnp.concatenate([zh, sin, zeros], axis=-1)
    reps = LANES // HEAD_DIM
    return jnp.concatenate([jnp.tile(c, reps), jnp.tile(s1, reps), jnp.tile(s2, reps)], axis=-1)


def _band_bias_table(rel_bias, tq):
    n_heads = rel_bias.shape[0]
    win = tq + BAND_PAD
    row_len = win + tq
    n_high = BAND_PAD - MAX_REL + 1
    n_ramp = 2 * MAX_REL - 1
    high = rel_bias[:, 2 * MAX_REL:]
    profile = jnp.concatenate([
        jnp.broadcast_to(high, (n_heads, n_high)),
        rel_bias[:, n_ramp:0:-1],
        jnp.broadcast_to(rel_bias[:, :1], (n_heads, win - n_high - n_ramp)),
        jnp.broadcast_to(high, (n_heads, tq)),
    ], axis=1)
    skew = jnp.broadcast_to(profile[:, None, :], (n_heads, tq, row_len)).reshape(n_heads, tq * row_len)
    skew = skew[:, :tq * (row_len - 1)].reshape(n_heads, tq, row_len - 1)[:, :, :win]
    qi = np.arange(tq)[:, None]
    kj = np.arange(win)[None, :]
    q_chunk = qi // CHUNK
    k_chunk = kj // CHUNK - (BAND_CHUNKS - 1)
    in_band = (k_chunk <= q_chunk) & (k_chunk >= q_chunk - (BAND_CHUNKS - 1))
    return jnp.where(jnp.asarray(in_band)[None], skew, NEG_INF)


def kernel(x, c, positions, w_ada, b_ada, w_in, rel_bias, idx_k_norm_g, idx_k_norm_b, w_branch_a,
           w_branch_b, w_out, ln1_g, ln1_b, w_router, router_bias, w_exp_gate, w_exp_up, w_exp_down,
           w_sh_gate, w_sh_up, w_sh_down, ln2_g, ln2_b):
    bsz, s, d = x.shape
    depth = w_ada.shape[0]
    alpha = (2.0 * depth) ** 0.25
    tm_proj = min(512, s)
    tq_band = min(256, s)
    tq_dsa = min(128, s)
    tm_merge = min(512, s)
    tm_moe = min(1024, s)

    rope_tab = _rope_tables(positions)
    n7 = 7 * WIDTH
    for l in range(depth):
        mod = _ada(c, w_ada[l], b_ada[l]).reshape(bsz, N_MOD, d)
        w_l = w_in[l]
        w7 = w_l[:, :n7].astype(BF16)
        n_kw = IDX_DIM + N_IDX_HEADS
        wkw = jnp.pad(w_l[:, n7:n7 + n_kw], ((0, 0), (0, LANES - n_kw))).astype(BF16)
        wg = w_l[:, n7 + n_kw:].astype(BF16)
        lnk = jnp.pad(jnp.stack([idx_k_norm_g[l], idx_k_norm_b[l]]), ((0, 0), (0, LANES - IDX_DIM)))
        qa, ka, va, qb, kb, vb, qi, ki, wi, ga, gb = _proj(x, mod, rope_tab, w7, wkw, wg, lnk, tm_proj)

        oa = _band(qa, ka, va, _band_bias_table(rel_bias[l], tq_band), tq_band)
        ob = _dsa(qi, wi, qb, ki, kb, vb, tq_dsa)

        wr = jnp.pad(w_router[l], ((0, 0), (0, LANES - N_EXPERTS)))
        wr_hi = wr.astype(BF16)
        wr_lo = (wr - wr_hi.astype(F32)).astype(BF16)
        x1, u2, comb = _merge(
            oa, ob, ga, gb, x, mod,
            w_branch_a[l].astype(BF16), w_branch_b[l].astype(BF16), w_out[l].astype(BF16),
            jnp.stack([ln1_g[l], ln1_b[l]]), jnp.concatenate([wr_hi, wr_lo], axis=1),
            router_bias[l].reshape(N_EXPERTS, 1), tm_merge, alpha)

        t = bsz * s
        gate = comb.reshape(t, LANES)[:, :N_EXPERTS].reshape(t, N_EXPERTS // EXPERTS_PER_STEP, EXPERTS_PER_STEP)
        gate = gate.transpose(1, 0, 2)
        out = _moe(
            u2.reshape(t, d), gate, x1.reshape(t, d), mod,
            w_exp_gate[l].astype(BF16), w_exp_up[l].astype(BF16), w_exp_down[l].astype(BF16),
            w_sh_gate[l].astype(BF16), w_sh_up[l].astype(BF16), w_sh_down[l].astype(BF16),
            jnp.stack([ln2_g[l], ln2_b[l]]), tm_moe, s, alpha)
        x = out.reshape(bsz, s, d)
    return x
```

```python
import functools

import jax
import jax.numpy as jnp
import numpy as np
from jax import lax
from jax.experimental import pallas as pl
from jax.experimental.pallas import tpu as pltpu

F32 = jnp.float32
BF16 = jnp.bfloat16
NEG_INF = float("-inf")

CHUNK = 64
HEAD_DIM = 64
N_HEADS = 8
WIDTH = N_HEADS * HEAD_DIM
BAND_CHUNKS = 9
BAND_PAD = (BAND_CHUNKS - 1) * CHUNK
MAX_REL = 128
ROPE_THETA = 500000.0
ROPE_DIM = HEAD_DIM // 4
ROPE_HALF = ROPE_DIM // 2
N_IDX_HEADS = 8
IDX_DIM = 64
TOPK_MAX = 256
N_EXPERTS = 64
N_GROUPS = 8
GROUP_SIZE = N_EXPERTS // N_GROUPS
TOPK_GROUPS = 4
TOPK_EXPERTS = 8
ROUTED_SCALE = 1.0
N_MOD = 6
LN_EPS = 1e-5
LANES = 128
KEY_TILE = 256

VMEM_LIMIT = 56 * 1024 * 1024

EXPERTS_PER_STEP = 4
DSA_KEY_CLASSES = 8


def _cparams(sem):
    return pltpu.CompilerParams(dimension_semantics=sem, vmem_limit_bytes=VMEM_LIMIT)


def _ln(z):
    mu = jnp.mean(z, axis=-1, keepdims=True)
    zc = z - mu
    var = jnp.mean(zc * zc, axis=-1, keepdims=True)
    return zc * lax.rsqrt(var + LN_EPS)


def _sigmoid(z):
    return 1.0 / (1.0 + jnp.exp(-z))


def _dot(a, b):
    return jnp.dot(a, b, preferred_element_type=F32)


def _dot_nt(a, b):
    return lax.dot_general(a, b, (((1,), (1,)), ((), ())), preferred_element_type=F32)


def _split_bf16(z):
    hi = z.astype(BF16)
    lo = (z - hi.astype(F32)).astype(BF16)
    return hi, lo


def _ada_kernel(c_ref, w_ref, b_ref, o_ref):
    c = c_ref[...]
    ca = c * _sigmoid(c)
    ca_hi, ca_lo = _split_bf16(ca)
    w = w_ref[...]
    w_hi, w_lo = _split_bf16(w)
    acc = _dot(ca_hi, w_hi) + _dot(ca_lo, w_hi) + _dot(ca_hi, w_lo)
    o_ref[...] = acc + b_ref[...]


def _ada(c, w_ada, b_ada):
    bsz, d = c.shape
    n = w_ada.shape[1]
    tn = 1024
    return pl.pallas_call(
        _ada_kernel,
        out_shape=jax.ShapeDtypeStruct((bsz, n), F32),
        grid=(n // tn,),
        in_specs=[
            pl.BlockSpec((bsz, d), lambda j: (0, 0)),
            pl.BlockSpec((d, tn), lambda j: (0, j)),
            pl.BlockSpec((1, tn), lambda j: (0, j)),
        ],
        out_specs=pl.BlockSpec((bsz, tn), lambda j: (0, j)),
        compiler_params=_cparams(("arbitrary",)),
        name="ada",
    )(c, w_ada, b_ada.reshape(1, n))


def _rope(z, c_t, s1_t, s2_t):
    n = z.shape[-1]
    return z * c_t + pltpu.roll(z, n - ROPE_HALF, 1) * s1_t + pltpu.roll(z, ROPE_HALF, 1) * s2_t


def _proj_kernel(x_ref, mod_ref, rope_ref, w7_ref, wkw_ref, wg_ref, lnk_ref,
                 qa_ref, ka_ref, va_ref, qb_ref, kb_ref, vb_ref, qi_ref, ki_ref, wi_ref,
                 ga_ref, gb_ref):
    x = x_ref[0]
    mod = mod_ref[0]
    sh1 = mod[0:1, :]
    sc1 = mod[1:2, :]
    u = (_ln(x) * (1.0 + sc1) + sh1).astype(BF16)

    rope = rope_ref[0]
    c1 = rope[:, 0:LANES]
    s1 = rope[:, LANES:2 * LANES]
    s2 = rope[:, 2 * LANES:3 * LANES]
    reps = WIDTH // LANES
    c_t = jnp.concatenate([c1] * reps, axis=1)
    s1_t = jnp.concatenate([s1] * reps, axis=1)
    s2_t = jnp.concatenate([s2] * reps, axis=1)

    att_scale = HEAD_DIM ** -0.5
    idx_scale = IDX_DIM ** -0.5

    def seg(k):
        return _dot(u, w7_ref[:, k * WIDTH:(k + 1) * WIDTH])

    qa_ref[0] = (seg(0) * att_scale).astype(BF16)
    ka_ref[0] = seg(1).astype(BF16)
    va_ref[0] = seg(2).astype(BF16)
    qb_ref[0] = (_rope(seg(3), c_t, s1_t, s2_t) * att_scale).astype(BF16)
    kb_ref[0] = _rope(seg(4), c_t, s1_t, s2_t).astype(BF16)
    vb_ref[0] = seg(5).astype(BF16)
    qi_ref[0] = (_rope(seg(6), c_t, s1_t, s2_t) * idx_scale).astype(BF16)

    z = _dot(u, wkw_ref[...])
    lane = lax.broadcasted_iota(jnp.int32, z.shape, 1)
    is_k = lane < IDX_DIM
    mu = jnp.sum(jnp.where(is_k, z, 0.0), axis=-1, keepdims=True) * (1.0 / IDX_DIM)
    zc = jnp.where(is_k, z - mu, 0.0)
    var = jnp.sum(zc * zc, axis=-1, keepdims=True) * (1.0 / IDX_DIM)
    lnk = lnk_ref[...]
    y = zc * lax.rsqrt(var + LN_EPS) * lnk[0:1, :] + lnk[1:2, :]
    y = _rope(y, c1, jnp.where(is_k, s1, 0.0), jnp.where(is_k, s2, 0.0))
    ki_ref[0] = y[:, 0:IDX_DIM].astype(BF16)
    wi_ref[0] = z.T[IDX_DIM:IDX_DIM + N_IDX_HEADS, :] * (N_IDX_HEADS ** -0.5)

    d = ga_ref.shape[-1]
    ga_ref[0] = _dot(u, wg_ref[:, 0:d]).astype(BF16)
    gb_ref[0] = _dot(u, wg_ref[:, d:2 * d]).astype(BF16)


def _proj(x, mod, rope_tab, w7, wkw, wg, lnk, tm):
    bsz, s, d = x.shape
    const = dict(pipeline_mode=pl.Buffered(1))
    row = lambda b, i: (b, i, 0)
    wspec = lambda shape: pl.BlockSpec(shape, lambda b, i: (0, 0), **const)
    out_w = jax.ShapeDtypeStruct((bsz, s, WIDTH), BF16)
    out_d = jax.ShapeDtypeStruct((bsz, s, d), BF16)
    return pl.pallas_call(
        _proj_kernel,
        out_shape=[out_w] * 7 + [
            jax.ShapeDtypeStruct((bsz, s, IDX_DIM), BF16),
            jax.ShapeDtypeStruct((bsz, N_IDX_HEADS, s), F32),
            out_d, out_d],
        grid=(bsz, s // tm),
        in_specs=[
            pl.BlockSpec((1, tm, d), row),
            pl.BlockSpec((1, N_MOD, d), lambda b, i: (b, 0, 0)),
            pl.BlockSpec((1, tm, 3 * LANES), row),
            wspec(w7.shape), wspec(wkw.shape), wspec(wg.shape), wspec(lnk.shape),
        ],
        out_specs=[pl.BlockSpec((1, tm, WIDTH), row)] * 7 + [
            pl.BlockSpec((1, tm, IDX_DIM), row),
            pl.BlockSpec((1, N_IDX_HEADS, tm), lambda b, i: (b, 0, i)),
            pl.BlockSpec((1, tm, d), row), pl.BlockSpec((1, tm, d), row)],
        compiler_params=_cparams(("arbitrary", "arbitrary")),
        name="proj",
    )(x, mod, rope_tab, w7, wkw, wg, lnk)


def _band_kernel(q_ref, k_ref, v_ref, bias_ref, o_ref, kpad, vpad, a_scr, p_scr, *, tq):
    i = pl.program_id(1)
    s = k_ref.shape[1]
    win = tq + BAND_PAD

    @pl.when(i == 0)
    def _():
        zeros = jnp.zeros((BAND_PAD, WIDTH), BF16)
        kpad[0:BAND_PAD, :] = zeros
        vpad[0:BAND_PAD, :] = zeros
        kpad[BAND_PAD:BAND_PAD + s, :] = k_ref[0]
        vpad[BAND_PAD:BAND_PAD + s, :] = v_ref[0]

    q0 = pl.multiple_of(i * tq, tq)
    n_lg = win // LANES
    col = lax.broadcasted_iota(jnp.int32, (tq, LANES), 1)
    first_frame = BAND_PAD - q0
    heads = [slice(h * HEAD_DIM, (h + 1) * HEAD_DIM) for h in range(N_HEADS)]

    def lanes(g):
        return slice(g * LANES, (g + 1) * LANES)

    row_max = []
    for h, sl in enumerate(heads):
        sc = _dot_nt(q_ref[0, :, sl], kpad[pl.ds(q0, win), sl])
        mx = jnp.full((tq, LANES), NEG_INF, F32)
        for g in range(n_lg):
            a = jnp.where(col >= first_frame - g * LANES, sc[:, lanes(g)] + bias_ref[h, :, lanes(g)], NEG_INF)
            a_scr[h, :, lanes(g)] = a
            mx = jnp.maximum(mx, a)
        row_max.append(jnp.max(mx, axis=-1, keepdims=True))
    row_sum = []
    for h in range(N_HEADS):
        m_b = jnp.broadcast_to(row_max[h], (tq, LANES))
        ls = jnp.zeros((tq, LANES), F32)
        for g in range(n_lg):
            p = jnp.exp(a_scr[h, :, lanes(g)] - m_b)
            ls = ls + p
            p_scr[h, :, lanes(g)] = p.astype(BF16)
        row_sum.append(jnp.sum(ls, axis=-1, keepdims=True))
    for h, sl in enumerate(heads):
        o = _dot(p_scr[h], vpad[pl.ds(q0, win), sl]) / row_sum[h]
        o_ref[0, :, sl] = o.astype(BF16)


def _band(qa, ka, va, bias_tab, tq):
    bsz, s, _ = qa.shape
    win = tq + BAND_PAD
    return pl.pallas_call(
        functools.partial(_band_kernel, tq=tq),
        out_shape=jax.ShapeDtypeStruct((bsz, s, WIDTH), BF16),
        grid=(bsz, s // tq),
        in_specs=[
            pl.BlockSpec((1, tq, WIDTH), lambda b, i: (b, i, 0)),
            pl.BlockSpec((1, s, WIDTH), lambda b, i: (b, 0, 0)),
            pl.BlockSpec((1, s, WIDTH), lambda b, i: (b, 0, 0)),
            pl.BlockSpec((N_HEADS, tq, win), lambda b, i: (0, 0, 0)),
        ],
        out_specs=pl.BlockSpec((1, tq, WIDTH), lambda b, i: (b, i, 0)),
        scratch_shapes=[pltpu.VMEM((BAND_PAD + s, WIDTH), BF16),
                        pltpu.VMEM((BAND_PAD + s, WIDTH), BF16),
                        pltpu.VMEM((N_HEADS, tq, win), F32),
                        pltpu.VMEM((N_HEADS, tq, win), BF16)],
        compiler_params=_cparams(("arbitrary", "arbitrary")),
        name="band",
    )(qa, ka, va, bias_tab)


BISECT_STEPS_PER_CHECK = 4
BISECT_MAX_CHECKS = 400
COUNT_ROWS = 64


def _dsa_kernel(qi_ref, wit_ref, qb_ref, ki_ref, kb_ref, vb_ref, o_ref, sct_scr, msk_scr, a_scr, p_scr,
                *, tq, k_sel, q_base):
    i = pl.program_id(1)
    sk = ki_ref.shape[1]
    n_kt = sk // KEY_TILE
    n_lg = sk // LANES
    q0 = q_base + i * tq
    kf = float(k_sel)

    def lanes(g):
        return slice(g * LANES, (g + 1) * LANES)

    def slab_reduce(fn, combine, init):
        acc = jnp.full((COUNT_ROWS, tq), init, F32)
        for r in range(sk // COUNT_ROWS):
            acc = combine(acc, fn(sct_scr[r * COUNT_ROWS:(r + 1) * COUNT_ROWS, :]))
        return acc

    def count(pred):
        part = slab_reduce(lambda t: jnp.where(pred(t), 1.0, 0.0), jnp.add, 0.0)
        return jnp.sum(part, axis=0, keepdims=True)

    w_t = wit_ref[0]
    t_pos = q0 + lax.broadcasted_iota(jnp.int32, (1, tq), 1)
    limit = (t_pos // CHUNK + 1) * CHUNK
    key_in_tile = lax.broadcasted_iota(jnp.int32, (KEY_TILE, tq), 0)
    for kt in range(n_kt):
        ks = slice(kt * KEY_TILE, (kt + 1) * KEY_TILE)
        ki_t = ki_ref[0, ks, :]
        acc = jnp.zeros((KEY_TILE, tq), F32)
        for h in range(N_IDX_HEADS):
            lg = _dot_nt(ki_t, qi_ref[0, :, h * IDX_DIM:(h + 1) * IDX_DIM])
            acc = acc + jnp.maximum(lg, 0.0) * w_t[h:h + 1, :]
        sct_scr[ks, :] = jnp.where(key_in_tile < limit - kt * KEY_TILE, acc, NEG_INF)

    smax = jnp.max(slab_reduce(lambda t: t, jnp.maximum, NEG_INF), axis=0, keepdims=True)
    smin = jnp.min(slab_reduce(lambda t: jnp.where(t == NEG_INF, jnp.inf, t), jnp.minimum, jnp.inf),
                   axis=0, keepdims=True)
    n_adm = limit.astype(F32)
    c_max = count(lambda t: t >= smax)
    c_pos = count(lambda t: t > 0.0)
    c_nn = count(lambda t: t >= 0.0)
    zero = jnp.zeros_like(smax)
    at_zero = (c_pos < kf) & (c_nn >= kf)
    below_zero = c_nn < kf
    lo = jnp.where(at_zero | ~below_zero, zero, smin)
    clo = jnp.where(at_zero | ~below_zero, c_nn, n_adm)
    hi = jnp.where(at_zero | below_zero, zero, smax)
    few = n_adm <= kf
    lo = jnp.where(few, smin, lo)
    clo = jnp.where(few, n_adm, clo)
    top_tied = (c_max >= kf) & ~few
    lo = jnp.where(top_tied, smax, lo)
    clo = jnp.where(top_tied, c_max, clo)
    hi = jnp.where(top_tied, smax, hi)
    done0 = jnp.where(few | at_zero | top_tied, 1.0, 0.0)

    def bisect(carry):
        lo, hi, clo, done, it = carry
        for _ in range(BISECT_STEPS_PER_CHECK):
            mid = 0.5 * lo + 0.5 * hi
            stuck = (mid <= lo) | (mid >= hi)
            c = count(lambda t: t >= mid)
            ge = c >= kf
            lo = jnp.where(ge, mid, lo)
            clo = jnp.where(ge, c, clo)
            hi = jnp.where(ge, hi, mid)
            done = jnp.where(stuck | (clo <= kf), 1.0, done)
        return lo, hi, clo, done, it + 1

    def not_converged(carry):
        _, _, _, done, it = carry
        return (jnp.min(done) < 0.5) & (it < BISECT_MAX_CHECKS)

    lo, hi, clo, _, _ = lax.while_loop(not_converged, bisect, (lo, hi, clo, done0, jnp.int32(0)))

    thr = lo
    c_gt = count(lambda t: t > thr)
    c_eq = count(lambda t: t == thr)
    need = kf - c_gt
    tie_cut = jnp.max(c_gt + c_eq - kf) > 0.0

    def with_ties():
        r_i = lax.broadcasted_iota(jnp.int32, (KEY_TILE, KEY_TILE), 0)
        c_i = lax.broadcasted_iota(jnp.int32, (KEY_TILE, KEY_TILE), 1)
        earlier = jnp.where(c_i < r_i, 1.0, 0.0).astype(BF16)
        carry = jnp.zeros((1, tq), F32)
        for kt in range(n_kt):
            ks = slice(kt * KEY_TILE, (kt + 1) * KEY_TILE)
            sc = sct_scr[ks, :]
            eq = sc == thr
            eq_f = jnp.where(eq, 1.0, 0.0)
            rank = _dot(earlier, eq_f.astype(BF16)) + carry
            keep = (sc > thr) | (eq & (rank < need))
            msk_scr[:, ks] = jnp.where(keep, 0.0, NEG_INF).T
            carry = carry + jnp.sum(eq_f, axis=0, keepdims=True)

    def without_ties():
        for kt in range(n_kt):
            ks = slice(kt * KEY_TILE, (kt + 1) * KEY_TILE)
            msk_scr[:, ks] = jnp.where(sct_scr[ks, :] >= thr, 0.0, NEG_INF).T

    lax.cond(tie_cut, with_ties, without_ties)

    heads = [slice(h * HEAD_DIM, (h + 1) * HEAD_DIM) for h in range(N_HEADS)]
    row_max = []
    for h, sl in enumerate(heads):
        qh = qb_ref[0, :, sl]
        mx = jnp.full((tq, LANES), NEG_INF, F32)
        for kt in range(n_kt):
            ks = slice(kt * KEY_TILE, (kt + 1) * KEY_TILE)
            a = _dot_nt(qh, kb_ref[0, ks, sl]) + msk_scr[:, ks]
            a_scr[h, :, ks] = a
            for g in range(KEY_TILE // LANES):
                mx = jnp.maximum(mx, a[:, lanes(g)])
        row_max.append(jnp.max(mx, axis=-1, keepdims=True))
    row_sum = []
    for h in range(N_HEADS):
        m_b = jnp.broadcast_to(row_max[h], (tq, LANES))
        ls = jnp.zeros((tq, LANES), F32)
        for g in range(n_lg):
            p = jnp.exp(a_scr[h, :, lanes(g)] - m_b)
            ls = ls + p
            p_scr[h, :, lanes(g)] = p.astype(BF16)
        row_sum.append(jnp.sum(ls, axis=-1, keepdims=True))
    for h, sl in enumerate(heads):
        o = _dot(p_scr[h], vb_ref[0, :, sl]) / row_sum[h]
        o_ref[0, :, sl] = o.astype(BF16)


def _dsa_class(qi, wit, qb, ki, kb, vb, tq, k_sel, q_base, q_len, sk):
    bsz = qb.shape[0]
    blk0 = q_base // tq
    row = lambda b, i: (b, blk0 + i, 0)
    keys = lambda b, i: (b, 0, 0)
    return pl.pallas_call(
        functools.partial(_dsa_kernel, tq=tq, k_sel=k_sel, q_base=q_base),
        out_shape=jax.ShapeDtypeStruct((bsz, q_len, WIDTH), BF16),
        grid=(bsz, q_len // tq),
        in_specs=[
            pl.BlockSpec((1, tq, N_IDX_HEADS * IDX_DIM), row),
            pl.BlockSpec((1, N_IDX_HEADS, tq), lambda b, i: (b, 0, blk0 + i)),
            pl.BlockSpec((1, tq, WIDTH), row),
            pl.BlockSpec((1, sk, IDX_DIM), keys),
            pl.BlockSpec((1, sk, WIDTH), keys),
            pl.BlockSpec((1, sk, WIDTH), keys),
        ],
        out_specs=pl.BlockSpec((1, tq, WIDTH), lambda b, i: (b, i, 0)),
        scratch_shapes=[pltpu.VMEM((sk, tq), F32), pltpu.VMEM((tq, sk), F32),
                        pltpu.VMEM((N_HEADS, tq, sk), F32), pltpu.VMEM((N_HEADS, tq, sk), BF16)],
        compiler_params=_cparams(("arbitrary", "arbitrary")),
        name=f"dsa_k{sk}",
    )(qi, wit, qb, ki, kb, vb)


def _dsa(qi, wit, qb, ki, kb, vb, tq):
    s = qb.shape[1]
    k_sel = min(TOPK_MAX, s // 4)
    n_cls = max(1, min(DSA_KEY_CLASSES, s // KEY_TILE))
    q_len = s // n_cls
    outs = [_dsa_class(qi, wit, qb, ki, kb, vb, tq, k_sel, c * q_len, q_len, (c + 1) * q_len)
            for c in range(n_cls)]
    return jnp.concatenate(outs, axis=1)


def _merge_kernel(oa_ref, ob_ref, ga_ref, gb_ref, x_ref, mod_ref, wba_ref, wbb_ref, wo_ref,
                  ln1_ref, wr_ref, rb_ref, x1_ref, u2_ref, comb_ref, *, alpha):
    tm = x_ref.shape[1]
    ya = _dot(oa_ref[0], wba_ref[...])
    yb = _dot(ob_ref[0], wbb_ref[...])
    merged = _sigmoid(ga_ref[0].astype(F32)) * ya + _sigmoid(gb_ref[0].astype(F32)) * yb
    mix = _dot(merged.astype(BF16), wo_ref[...])
    mod = mod_ref[0]
    g1 = mod[2:3, :]
    sh2 = mod[3:4, :]
    sc2 = mod[4:5, :]
    ln1 = ln1_ref[...]
    x1 = _ln(alpha * x_ref[0] + g1 * mix) * ln1[0:1, :] + ln1[1:2, :]
    x1_ref[0] = x1
    u2 = _ln(x1) * (1.0 + sc2) + sh2
    u2_ref[0] = u2.astype(BF16)

    u_hi, u_lo = _split_bf16(u2)
    both = _dot(u_hi, wr_ref[...])
    logits = both[:, 0:LANES] + both[:, LANES:] + _dot(u_lo, wr_ref[:, 0:LANES])
    aff = _sigmoid(logits.T[0:N_EXPERTS, :])
    biased = aff + rb_ref[...]
    grp = biased.reshape(N_GROUPS, GROUP_SIZE, tm)
    sub = lax.broadcasted_iota(jnp.int32, grp.shape, 1)
    m1 = jnp.max(grp, axis=1, keepdims=True)
    first = jnp.min(jnp.where(grp == m1, sub, GROUP_SIZE), axis=1, keepdims=True)
    m2 = jnp.max(jnp.where(sub == first, NEG_INF, grp), axis=1, keepdims=True)
    gscore = (m1 + m2).reshape(N_GROUPS, tm)
    g_i = lax.broadcasted_iota(jnp.int32, (N_GROUPS, tm), 0)
    g_rank = jnp.zeros((N_GROUPS, tm), F32)
    for g in range(N_GROUPS):
        other = gscore[g:g + 1, :]
        beats = (other > gscore) | ((other == gscore) & (g < g_i))
        g_rank = g_rank + jnp.where(beats, 1.0, 0.0)
    g_keep = jnp.where(g_rank < TOPK_GROUPS, 1.0, 0.0).reshape(N_GROUPS, 1, tm)
    e_keep = jnp.broadcast_to(g_keep, (N_GROUPS, GROUP_SIZE, tm)).reshape(N_EXPERTS, tm)
    masked = jnp.where(e_keep > 0.5, biased, NEG_INF)
    e_i = lax.broadcasted_iota(jnp.int32, (N_EXPERTS, tm), 0)
    chosen = jnp.zeros((N_EXPERTS, tm), F32)
    for _ in range(TOPK_EXPERTS):
        best = jnp.max(masked, axis=0, keepdims=True)
        first = jnp.min(jnp.where(masked == best, e_i, N_EXPERTS), axis=0, keepdims=True)
        hit = e_i == first
        chosen = jnp.where(hit, 1.0, chosen)
        masked = jnp.where(hit, NEG_INF, masked)
    top_aff = jnp.where(chosen > 0.5, aff, 0.0)
    comb_t = top_aff / jnp.sum(top_aff, axis=0, keepdims=True) * ROUTED_SCALE
    comb_ref[0] = jnp.concatenate([comb_t, jnp.zeros_like(comb_t)], axis=0).T


def _merge(oa, ob, ga, gb, x, mod, wba, wbb, wo, ln1, wr, rb, tm, alpha):
    bsz, s, d = x.shape
    row = lambda b, i: (b, i, 0)
    w2 = lambda shape: pl.BlockSpec(shape, lambda b, i: (0,) * len(shape))
    return pl.pallas_call(
        functools.partial(_merge_kernel, alpha=alpha),
        out_shape=[jax.ShapeDtypeStruct((bsz, s, d), F32),
                   jax.ShapeDtypeStruct((bsz, s, d), BF16),
                   jax.ShapeDtypeStruct((bsz, s, LANES), F32)],
        grid=(bsz, s // tm),
        in_specs=[
            pl.BlockSpec((1, tm, WIDTH), row), pl.BlockSpec((1, tm, WIDTH), row),
            pl.BlockSpec((1, tm, d), row), pl.BlockSpec((1, tm, d), row),
            pl.BlockSpec((1, tm, d), row),
            pl.BlockSpec((1, N_MOD, d), lambda b, i: (b, 0, 0)),
            w2(wba.shape), w2(wbb.shape), w2(wo.shape), w2(ln1.shape), w2(wr.shape), w2(rb.shape),
        ],
        out_specs=[pl.BlockSpec((1, tm, d), row), pl.BlockSpec((1, tm, d), row),
                   pl.BlockSpec((1, tm, LANES), row)],
        compiler_params=_cparams(("arbitrary", "arbitrary")),
        name="merge",
    )(oa, ob, ga, gb, x, mod, wba, wbb, wo, ln1, wr, rb)


def _silu(z):
    return z * _sigmoid(z)


def _moe_kernel(u2_ref, gate_ref, x1_ref, mod_ref, wg_ref, wu_ref, wd_ref, wsg_ref, wsu_ref, wsd_ref,
                ln2_ref, o_ref, *, alpha):
    g = pl.program_id(1)
    t = u2_ref[...]
    d_exp = wg_ref.shape[-1]

    @pl.when(g == 0)
    def _():
        hs = _silu(_dot(t, wsg_ref[...])) * _dot(t, wsu_ref[...])
        o_ref[...] = _dot(hs.astype(BF16), wsd_ref[...])

    gate = pltpu.roll(gate_ref[...], (LANES - g * EXPERTS_PER_STEP) % LANES, 1)
    hs = [_silu(_dot(t, wg_ref[j])) * _dot(t, wu_ref[j]) * gate[:, j:j + 1] for j in range(EXPERTS_PER_STEP)]
    h = jnp.concatenate(hs, axis=1)
    wd = wd_ref[...].reshape(EXPERTS_PER_STEP * d_exp, wd_ref.shape[-1])
    o_ref[...] += _dot(h.astype(BF16), wd)

    @pl.when(g == pl.num_programs(1) - 1)
    def _():
        mod = mod_ref[0]
        g2 = mod[5:6, :]
        ln2 = ln2_ref[...]
        o_ref[...] = _ln(alpha * x1_ref[...] + g2 * o_ref[...]) * ln2[0:1, :] + ln2[1:2, :]


def _moe(u2, gate, x1, mod, wg, wu, wd, wsg, wsu, wsd, ln2, tm, tokens_per_batch, alpha):
    t, d = u2.shape
    n_exp, _, d_exp = wg.shape
    eps = EXPERTS_PER_STEP
    blocks_per_batch = tokens_per_batch // tm
    row = lambda i, g: (i, 0)
    w2 = lambda shape: pl.BlockSpec(shape, lambda i, g: (0,) * len(shape))
    return pl.pallas_call(
        functools.partial(_moe_kernel, alpha=alpha),
        out_shape=jax.ShapeDtypeStruct((t, d), F32),
        grid=(t // tm, n_exp // eps),
        in_specs=[
            pl.BlockSpec((tm, d), row),
            pl.BlockSpec((tm, LANES), row),
            pl.BlockSpec((tm, d), row),
            pl.BlockSpec((1, N_MOD, d), lambda i, g: (i // blocks_per_batch, 0, 0)),
            pl.BlockSpec((eps, d, d_exp), lambda i, g: (g, 0, 0)),
            pl.BlockSpec((eps, d, d_exp), lambda i, g: (g, 0, 0)),
            pl.BlockSpec((eps, d_exp, d), lambda i, g: (g, 0, 0)),
            w2(wsg.shape), w2(wsu.shape), w2(wsd.shape), w2(ln2.shape),
        ],
        out_specs=pl.BlockSpec((tm, d), row),
        compiler_params=_cparams(("arbitrary", "arbitrary")),
        name="moe",
    )(u2, gate, x1, mod, wg, wu, wd, wsg, wsu, wsd, ln2)


def _rope_tables(positions):
    inv_freq = ROPE_THETA ** (-jnp.arange(0, ROPE_DIM, 2, dtype=F32) / ROPE_DIM)
    ang = positions.astype(F32)[..., None] * inv_freq
    cos, sin = jnp.cos(ang), jnp.sin(ang)
    ones = jnp.ones(cos.shape[:-1] + (HEAD_DIM - ROPE_DIM,), F32)
    zeros = jnp.zeros_like(ones)
    zh = jnp.zeros_like(sin)
    c = jnp.concatenate([cos, cos, ones], axis=-1)
    s1 = jnp.concatenate([-sin, zh, zeros], axis=-1)
    s2 = jnp.concatenate([zh, sin, zeros], axis=-1)
    reps = LANES // HEAD_DIM
    return jnp.concatenate([jnp.tile(c, reps), jnp.tile(s1, reps), jnp.tile(s2, reps)], axis=-1)


def _band_bias_table(rel_bias, tq):
    n_heads = rel_bias.shape[0]
    win = tq + BAND_PAD
    row_len = win + tq
    n_high = BAND_PAD - MAX_REL + 1
    n_ramp = 2 * MAX_REL - 1
    high = rel_bias[:, 2 * MAX_REL:]
    profile = jnp.concatenate([
        jnp.broadcast_to(high, (n_heads, n_high)),
        rel_bias[:, n_ramp:0:-1],
        jnp.broadcast_to(rel_bias[:, :1], (n_heads, win - n_high - n_ramp)),
        jnp.broadcast_to(high, (n_heads, tq)),
    ], axis=1)
    skew = jnp.broadcast_to(profile[:, None, :], (n_heads, tq, row_len)).reshape(n_heads, tq * row_len)
    skew = skew[:, :tq * (row_len - 1)].reshape(n_heads, tq, row_len - 1)[:, :, :win]
    qi = np.arange(tq)[:, None]
    kj = np.arange(win)[None, :]
    q_chunk = qi // CHUNK
    k_chunk = kj // CHUNK - (BAND_CHUNKS - 1)
    in_band = (k_chunk <= q_chunk) & (k_chunk >= q_chunk - (BAND_CHUNKS - 1))
    return jnp.where(jnp.asarray(in_band)[None], skew, NEG_INF)


def kernel(x, c, positions, w_ada, b_ada, w_in, rel_bias, idx_k_norm_g, idx_k_norm_b, w_branch_a,
           w_branch_b, w_out, ln1_g, ln1_b, w_router, router_bias, w_exp_gate, w_exp_up, w_exp_down,
           w_sh_gate, w_sh_up, w_sh_down, ln2_g, ln2_b):
    bsz, s, d = x.shape
    depth = w_ada.shape[0]
    alpha = (2.0 * depth) ** 0.25
    tm_proj = min(512, s)
    tq_band = min(256, s)
    tq_dsa = min(256, s)
    tm_merge = min(512, s)
    tm_moe = min(1024, s)

    rope_tab = _rope_tables(positions)
    n7 = 7 * WIDTH
    for l in range(depth):
        mod = _ada(c, w_ada[l], b_ada[l]).reshape(bsz, N_MOD, d)
        w_l = w_in[l]
        w7 = w_l[:, :n7].astype(BF16)
        n_kw = IDX_DIM + N_IDX_HEADS
        wkw = jnp.pad(w_l[:, n7:n7 + n_kw], ((0, 0), (0, LANES - n_kw))).astype(BF16)
        wg = w_l[:, n7 + n_kw:].astype(BF16)
        lnk = jnp.pad(jnp.stack([idx_k_norm_g[l], idx_k_norm_b[l]]), ((0, 0), (0, LANES - IDX_DIM)))
        qa, ka, va, qb, kb, vb, qi, ki, wi, ga, gb = _proj(x, mod, rope_tab, w7, wkw, wg, lnk, tm_proj)

        oa = _band(qa, ka, va, _band_bias_table(rel_bias[l], tq_band), tq_band)
        ob = _dsa(qi, wi, qb, ki, kb, vb, tq_dsa)

        wr = jnp.pad(w_router[l], ((0, 0), (0, LANES - N_EXPERTS)))
        wr_hi = wr.astype(BF16)
        wr_lo = (wr - wr_hi.astype(F32)).astype(BF16)
        x1, u2, comb = _merge(
            oa, ob, ga, gb, x, mod,
            w_branch_a[l].astype(BF16), w_branch_b[l].astype(BF16), w_out[l].astype(BF16),
            jnp.stack([ln1_g[l], ln1_b[l]]), jnp.concatenate([wr_hi, wr_lo], axis=1),
            router_bias[l].reshape(N_EXPERTS, 1), tm_merge, alpha)

        t = bsz * s
        out = _moe(
            u2.reshape(t, d), comb.reshape(t, LANES), x1.reshape(t, d), mod,
            w_exp_gate[l].astype(BF16), w_exp_up[l].astype(BF16), w_exp_down[l].astype(BF16),
            w_sh_gate[l].astype(BF16), w_sh_up[l].astype(BF16), w_sh_down[l].astype(BF16),
            jnp.stack([ln2_g[l], ln2_b[l]]), tm_moe, s, alpha)
        x = out.reshape(bsz, s, d)
    return x
```

```python
import functools

import jax
import jax.numpy as jnp
import numpy as np
from jax import lax
from jax.experimental import pallas as pl
from jax.experimental.pallas import tpu as pltpu
from jax.experimental.pallas import tpu_sc as plsc

F32 = jnp.float32
BF16 = jnp.bfloat16
NEG_INF = float("-inf")

CHUNK = 64
HEAD_DIM = 64
N_HEADS = 8
WIDTH = N_HEADS * HEAD_DIM
BAND_CHUNKS = 9
BAND_PAD = (BAND_CHUNKS - 1) * CHUNK
MAX_REL = 128
ROPE_THETA = 500000.0
ROPE_DIM = HEAD_DIM // 4
ROPE_HALF = ROPE_DIM // 2
N_IDX_HEADS = 8
IDX_DIM = 64
TOPK_MAX = 256
N_EXPERTS = 64
N_GROUPS = 8
GROUP_SIZE = N_EXPERTS // N_GROUPS
TOPK_GROUPS = 4
TOPK_EXPERTS = 8
ROUTED_SCALE = 1.0
N_MOD = 6
LN_EPS = 1e-5
LANES = 128
KEY_TILE = 256

VMEM_LIMIT = 56 * 1024 * 1024

DSA_KEY_CLASSES = 8
TILE_ROWS = 8
MOE_ROW_TILE = 512
SC_WINDOW = 32
SC_CORES = 2
SC_SUBCORES = 16


def _cparams(sem):
    return pltpu.CompilerParams(dimension_semantics=sem, vmem_limit_bytes=VMEM_LIMIT)


def _ln(z):
    mu = jnp.mean(z, axis=-1, keepdims=True)
    zc = z - mu
    var = jnp.mean(zc * zc, axis=-1, keepdims=True)
    return zc * lax.rsqrt(var + LN_EPS)


def _sigmoid(z):
    return 1.0 / (1.0 + jnp.exp(-z))


def _dot(a, b):
    return jnp.dot(a, b, preferred_element_type=F32)


def _dot_nt(a, b):
    return lax.dot_general(a, b, (((1,), (1,)), ((), ())), preferred_element_type=F32)


def _split_bf16(z):
    hi = z.astype(BF16)
    lo = (z - hi.astype(F32)).astype(BF16)
    return hi, lo


def _ada_kernel(c_ref, w_ref, b_ref, o_ref):
    c = c_ref[...]
    ca = c * _sigmoid(c)
    ca_hi, ca_lo = _split_bf16(ca)
    w = w_ref[...]
    w_hi, w_lo = _split_bf16(w)
    acc = _dot(ca_hi, w_hi) + _dot(ca_lo, w_hi) + _dot(ca_hi, w_lo)
    o_ref[...] = acc + b_ref[...]


def _ada(c, w_ada, b_ada):
    bsz, d = c.shape
    n = w_ada.shape[1]
    tn = 1024
    return pl.pallas_call(
        _ada_kernel,
        out_shape=jax.ShapeDtypeStruct((bsz, n), F32),
        grid=(n // tn,),
        in_specs=[
            pl.BlockSpec((bsz, d), lambda j: (0, 0)),
            pl.BlockSpec((d, tn), lambda j: (0, j)),
            pl.BlockSpec((1, tn), lambda j: (0, j)),
        ],
        out_specs=pl.BlockSpec((bsz, tn), lambda j: (0, j)),
        compiler_params=_cparams(("arbitrary",)),
        name="ada",
    )(c, w_ada, b_ada.reshape(1, n))


def _rope(z, c_t, s1_t, s2_t):
    n = z.shape[-1]
    return z * c_t + pltpu.roll(z, n - ROPE_HALF, 1) * s1_t + pltpu.roll(z, ROPE_HALF, 1) * s2_t


def _proj_kernel(x_ref, mod_ref, rope_ref, w7_ref, wkw_ref, wg_ref, lnk_ref,
                 qa_ref, ka_ref, va_ref, qb_ref, kb_ref, vb_ref, qi_ref, ki_ref, wi_ref,
                 ga_ref, gb_ref):
    x = x_ref[0]
    mod = mod_ref[0]
    sh1 = mod[0:1, :]
    sc1 = mod[1:2, :]
    u = (_ln(x) * (1.0 + sc1) + sh1).astype(BF16)

    rope = rope_ref[0]
    c1 = rope[:, 0:LANES]
    s1 = rope[:, LANES:2 * LANES]
    s2 = rope[:, 2 * LANES:3 * LANES]
    reps = WIDTH // LANES
    c_t = jnp.concatenate([c1] * reps, axis=1)
    s1_t = jnp.concatenate([s1] * reps, axis=1)
    s2_t = jnp.concatenate([s2] * reps, axis=1)

    att_scale = HEAD_DIM ** -0.5
    idx_scale = IDX_DIM ** -0.5

    def seg(k):
        return _dot(u, w7_ref[:, k * WIDTH:(k + 1) * WIDTH])

    qa_ref[0] = (seg(0) * att_scale).astype(BF16)
    ka_ref[0] = seg(1).astype(BF16)
    va_ref[0] = seg(2).astype(BF16)
    qb_ref[0] = (_rope(seg(3), c_t, s1_t, s2_t) * att_scale).astype(BF16)
    kb_ref[0] = _rope(seg(4), c_t, s1_t, s2_t).astype(BF16)
    vb_ref[0] = seg(5).astype(BF16)
    qi_ref[0] = (_rope(seg(6), c_t, s1_t, s2_t) * idx_scale).astype(BF16)

    z = _dot(u, wkw_ref[...])
    lane = lax.broadcasted_iota(jnp.int32, z.shape, 1)
    is_k = lane < IDX_DIM
    mu = jnp.sum(jnp.where(is_k, z, 0.0), axis=-1, keepdims=True) * (1.0 / IDX_DIM)
    zc = jnp.where(is_k, z - mu, 0.0)
    var = jnp.sum(zc * zc, axis=-1, keepdims=True) * (1.0 / IDX_DIM)
    lnk = lnk_ref[...]
    y = zc * lax.rsqrt(var + LN_EPS) * lnk[0:1, :] + lnk[1:2, :]
    y = _rope(y, c1, jnp.where(is_k, s1, 0.0), jnp.where(is_k, s2, 0.0))
    ki_ref[0] = y[:, 0:IDX_DIM].astype(BF16)
    wi_ref[0] = z.T[IDX_DIM:IDX_DIM + N_IDX_HEADS, :] * (N_IDX_HEADS ** -0.5)

    d = ga_ref.shape[-1]
    ga_ref[0] = _dot(u, wg_ref[:, 0:d]).astype(BF16)
    gb_ref[0] = _dot(u, wg_ref[:, d:2 * d]).astype(BF16)


def _proj(x, mod, rope_tab, w7, wkw, wg, lnk, tm):
    bsz, s, d = x.shape
    const = dict(pipeline_mode=pl.Buffered(1))
    row = lambda b, i: (b, i, 0)
    wspec = lambda shape: pl.BlockSpec(shape, lambda b, i: (0, 0), **const)
    out_w = jax.ShapeDtypeStruct((bsz, s, WIDTH), BF16)
    out_d = jax.ShapeDtypeStruct((bsz, s, d), BF16)
    return pl.pallas_call(
        _proj_kernel,
        out_shape=[out_w] * 7 + [
            jax.ShapeDtypeStruct((bsz, s, IDX_DIM), BF16),
            jax.ShapeDtypeStruct((bsz, N_IDX_HEADS, s), F32),
            out_d, out_d],
        grid=(bsz, s // tm),
        in_specs=[
            pl.BlockSpec((1, tm, d), row),
            pl.BlockSpec((1, N_MOD, d), lambda b, i: (b, 0, 0)),
            pl.BlockSpec((1, tm, 3 * LANES), row),
            wspec(w7.shape), wspec(wkw.shape), wspec(wg.shape), wspec(lnk.shape),
        ],
        out_specs=[pl.BlockSpec((1, tm, WIDTH), row)] * 7 + [
            pl.BlockSpec((1, tm, IDX_DIM), row),
            pl.BlockSpec((1, N_IDX_HEADS, tm), lambda b, i: (b, 0, i)),
            pl.BlockSpec((1, tm, d), row), pl.BlockSpec((1, tm, d), row)],
        compiler_params=_cparams(("arbitrary", "arbitrary")),
        name="proj",
    )(x, mod, rope_tab, w7, wkw, wg, lnk)


def _band_kernel(q_ref, k_ref, v_ref, bias_ref, o_ref, kpad, vpad, a_scr, p_scr, *, tq):
    i = pl.program_id(1)
    s = k_ref.shape[1]
    win = tq + BAND_PAD

    @pl.when(i == 0)
    def _():
        zeros = jnp.zeros((BAND_PAD, WIDTH), BF16)
        kpad[0:BAND_PAD, :] = zeros
        vpad[0:BAND_PAD, :] = zeros
        kpad[BAND_PAD:BAND_PAD + s, :] = k_ref[0]
        vpad[BAND_PAD:BAND_PAD + s, :] = v_ref[0]

    q0 = pl.multiple_of(i * tq, tq)
    n_lg = win // LANES
    col = lax.broadcasted_iota(jnp.int32, (tq, LANES), 1)
    first_frame = BAND_PAD - q0
    heads = [slice(h * HEAD_DIM, (h + 1) * HEAD_DIM) for h in range(N_HEADS)]

    def lanes(g):
        return slice(g * LANES, (g + 1) * LANES)

    row_max = []
    for h, sl in enumerate(heads):
        sc = _dot_nt(q_ref[0, :, sl], kpad[pl.ds(q0, win), sl])
        mx = jnp.full((tq, LANES), NEG_INF, F32)
        for g in range(n_lg):
            a = jnp.where(col >= first_frame - g * LANES, sc[:, lanes(g)] + bias_ref[h, :, lanes(g)], NEG_INF)
            a_scr[h, :, lanes(g)] = a
            mx = jnp.maximum(mx, a)
        row_max.append(jnp.max(mx, axis=-1, keepdims=True))
    row_sum = []
    for h in range(N_HEADS):
        m_b = jnp.broadcast_to(row_max[h], (tq, LANES))
        ls = jnp.zeros((tq, LANES), F32)
        for g in range(n_lg):
            p = jnp.exp(a_scr[h, :, lanes(g)] - m_b)
            ls = ls + p
            p_scr[h, :, lanes(g)] = p.astype(BF16)
        row_sum.append(jnp.sum(ls, axis=-1, keepdims=True))
    for h, sl in enumerate(heads):
        o = _dot(p_scr[h], vpad[pl.ds(q0, win), sl]) / row_sum[h]
        o_ref[0, :, sl] = o.astype(BF16)


def _band(qa, ka, va, bias_tab, tq):
    bsz, s, _ = qa.shape
    win = tq + BAND_PAD
    return pl.pallas_call(
        functools.partial(_band_kernel, tq=tq),
        out_shape=jax.ShapeDtypeStruct((bsz, s, WIDTH), BF16),
        grid=(bsz, s // tq),
        in_specs=[
            pl.BlockSpec((1, tq, WIDTH), lambda b, i: (b, i, 0)),
            pl.BlockSpec((1, s, WIDTH), lambda b, i: (b, 0, 0)),
            pl.BlockSpec((1, s, WIDTH), lambda b, i: (b, 0, 0)),
            pl.BlockSpec((N_HEADS, tq, win), lambda b, i: (0, 0, 0)),
        ],
        out_specs=pl.BlockSpec((1, tq, WIDTH), lambda b, i: (b, i, 0)),
        scratch_shapes=[pltpu.VMEM((BAND_PAD + s, WIDTH), BF16),
                        pltpu.VMEM((BAND_PAD + s, WIDTH), BF16),
                        pltpu.VMEM((N_HEADS, tq, win), F32),
                        pltpu.VMEM((N_HEADS, tq, win), BF16)],
        compiler_params=_cparams(("arbitrary", "arbitrary")),
        name="band",
    )(qa, ka, va, bias_tab)


BISECT_STEPS_PER_CHECK = 4
BISECT_MAX_CHECKS = 400
COUNT_ROWS = 64


def _dsa_kernel(qi_ref, wit_ref, qb_ref, ki_ref, kb_ref, vb_ref, o_ref, sct_scr, msk_scr, a_scr, p_scr,
                *, tq, k_sel, q_base):
    i = pl.program_id(1)
    sk = ki_ref.shape[1]
    n_kt = sk // KEY_TILE
    n_lg = sk // LANES
    q0 = q_base + i * tq
    kf = float(k_sel)

    def lanes(g):
        return slice(g * LANES, (g + 1) * LANES)

    def slab_reduce(fn, combine, init):
        acc = jnp.full((COUNT_ROWS, tq), init, F32)
        for r in range(sk // COUNT_ROWS):
            acc = combine(acc, fn(sct_scr[r * COUNT_ROWS:(r + 1) * COUNT_ROWS, :]))
        return acc

    def count(pred):
        part = slab_reduce(lambda t: jnp.where(pred(t), 1.0, 0.0), jnp.add, 0.0)
        return jnp.sum(part, axis=0, keepdims=True)

    w_t = wit_ref[0]
    t_pos = q0 + lax.broadcasted_iota(jnp.int32, (1, tq), 1)
    limit = (t_pos // CHUNK + 1) * CHUNK
    key_in_tile = lax.broadcasted_iota(jnp.int32, (KEY_TILE, tq), 0)
    for kt in range(n_kt):
        ks = slice(kt * KEY_TILE, (kt + 1) * KEY_TILE)
        ki_t = ki_ref[0, ks, :]
        acc = jnp.zeros((KEY_TILE, tq), F32)
        for h in range(N_IDX_HEADS):
            lg = _dot_nt(ki_t, qi_ref[0, :, h * IDX_DIM:(h + 1) * IDX_DIM])
            acc = acc + jnp.maximum(lg, 0.0) * w_t[h:h + 1, :]
        sct_scr[ks, :] = jnp.where(key_in_tile < limit - kt * KEY_TILE, acc, NEG_INF)

    smax = jnp.max(slab_reduce(lambda t: t, jnp.maximum, NEG_INF), axis=0, keepdims=True)
    smin = jnp.min(slab_reduce(lambda t: jnp.where(t == NEG_INF, jnp.inf, t), jnp.minimum, jnp.inf),
                   axis=0, keepdims=True)
    n_adm = limit.astype(F32)
    c_max = count(lambda t: t >= smax)
    c_pos = count(lambda t: t > 0.0)
    c_nn = count(lambda t: t >= 0.0)
    zero = jnp.zeros_like(smax)
    at_zero = (c_pos < kf) & (c_nn >= kf)
    below_zero = c_nn < kf
    lo = jnp.where(at_zero | ~below_zero, zero, smin)
    clo = jnp.where(at_zero | ~below_zero, c_nn, n_adm)
    hi = jnp.where(at_zero | below_zero, zero, smax)
    few = n_adm <= kf
    lo = jnp.where(few, smin, lo)
    clo = jnp.where(few, n_adm, clo)
    top_tied = (c_max >= kf) & ~few
    lo = jnp.where(top_tied, smax, lo)
    clo = jnp.where(top_tied, c_max, clo)
    hi = jnp.where(top_tied, smax, hi)
    done0 = jnp.where(few | at_zero | top_tied, 1.0, 0.0)

    def bisect(carry):
        lo, hi, clo, done, it = carry
        for _ in range(BISECT_STEPS_PER_CHECK):
            mid = 0.5 * lo + 0.5 * hi
            stuck = (mid <= lo) | (mid >= hi)
            c = count(lambda t: t >= mid)
            ge = c >= kf
            lo = jnp.where(ge, mid, lo)
            clo = jnp.where(ge, c, clo)
            hi = jnp.where(ge, hi, mid)
            done = jnp.where(stuck | (clo <= kf), 1.0, done)
        return lo, hi, clo, done, it + 1

    def not_converged(carry):
        _, _, _, done, it = carry
        return (jnp.min(done) < 0.5) & (it < BISECT_MAX_CHECKS)

    lo, hi, clo, _, _ = lax.while_loop(not_converged, bisect, (lo, hi, clo, done0, jnp.int32(0)))

    thr = lo
    c_gt = count(lambda t: t > thr)
    c_eq = count(lambda t: t == thr)
    need = kf - c_gt
    tie_cut = jnp.max(c_gt + c_eq - kf) > 0.0

    def with_ties():
        r_i = lax.broadcasted_iota(jnp.int32, (KEY_TILE, KEY_TILE), 0)
        c_i = lax.broadcasted_iota(jnp.int32, (KEY_TILE, KEY_TILE), 1)
        earlier = jnp.where(c_i < r_i, 1.0, 0.0).astype(BF16)
        carry = jnp.zeros((1, tq), F32)
        for kt in range(n_kt):
            ks = slice(kt * KEY_TILE, (kt + 1) * KEY_TILE)
            sc = sct_scr[ks, :]
            eq = sc == thr
            eq_f = jnp.where(eq, 1.0, 0.0)
            rank = _dot(earlier, eq_f.astype(BF16)) + carry
            keep = (sc > thr) | (eq & (rank < need))
            msk_scr[:, ks] = jnp.where(keep, 0.0, NEG_INF).T
            carry = carry + jnp.sum(eq_f, axis=0, keepdims=True)

    def without_ties():
        for kt in range(n_kt):
            ks = slice(kt * KEY_TILE, (kt + 1) * KEY_TILE)
            msk_scr[:, ks] = jnp.where(sct_scr[ks, :] >= thr, 0.0, NEG_INF).T

    lax.cond(tie_cut, with_ties, without_ties)

    heads = [slice(h * HEAD_DIM, (h + 1) * HEAD_DIM) for h in range(N_HEADS)]
    row_max = []
    for h, sl in enumerate(heads):
        qh = qb_ref[0, :, sl]
        mx = jnp.full((tq, LANES), NEG_INF, F32)
        for kt in range(n_kt):
            ks = slice(kt * KEY_TILE, (kt + 1) * KEY_TILE)
            a = _dot_nt(qh, kb_ref[0, ks, sl]) + msk_scr[:, ks]
            a_scr[h, :, ks] = a
            for g in range(KEY_TILE // LANES):
                mx = jnp.maximum(mx, a[:, lanes(g)])
        row_max.append(jnp.max(mx, axis=-1, keepdims=True))
    row_sum = []
    for h in range(N_HEADS):
        m_b = jnp.broadcast_to(row_max[h], (tq, LANES))
        ls = jnp.zeros((tq, LANES), F32)
        for g in range(n_lg):
            p = jnp.exp(a_scr[h, :, lanes(g)] - m_b)
            ls = ls + p
            p_scr[h, :, lanes(g)] = p.astype(BF16)
        row_sum.append(jnp.sum(ls, axis=-1, keepdims=True))
    for h, sl in enumerate(heads):
        o = _dot(p_scr[h], vb_ref[0, :, sl]) / row_sum[h]
        o_ref[0, :, sl] = o.astype(BF16)


def _dsa_class(qi, wit, qb, ki, kb, vb, tq, k_sel, q_base, q_len, sk):
    bsz = qb.shape[0]
    blk0 = q_base // tq
    row = lambda b, i: (b, blk0 + i, 0)
    keys = lambda b, i: (b, 0, 0)
    return pl.pallas_call(
        functools.partial(_dsa_kernel, tq=tq, k_sel=k_sel, q_base=q_base),
        out_shape=jax.ShapeDtypeStruct((bsz, q_len, WIDTH), BF16),
        grid=(bsz, q_len // tq),
        in_specs=[
            pl.BlockSpec((1, tq, N_IDX_HEADS * IDX_DIM), row),
            pl.BlockSpec((1, N_IDX_HEADS, tq), lambda b, i: (b, 0, blk0 + i)),
            pl.BlockSpec((1, tq, WIDTH), row),
            pl.BlockSpec((1, sk, IDX_DIM), keys),
            pl.BlockSpec((1, sk, WIDTH), keys),
            pl.BlockSpec((1, sk, WIDTH), keys),
        ],
        out_specs=pl.BlockSpec((1, tq, WIDTH), lambda b, i: (b, i, 0)),
        scratch_shapes=[pltpu.VMEM((sk, tq), F32), pltpu.VMEM((tq, sk), F32),
                        pltpu.VMEM((N_HEADS, tq, sk), F32), pltpu.VMEM((N_HEADS, tq, sk), BF16)],
        compiler_params=_cparams(("arbitrary", "arbitrary")),
        name=f"dsa_k{sk}",
    )(qi, wit, qb, ki, kb, vb)


def _dsa(qi, wit, qb, ki, kb, vb, tq):
    s = qb.shape[1]
    k_sel = min(TOPK_MAX, s // 4)
    n_cls = max(1, min(DSA_KEY_CLASSES, s // KEY_TILE))
    q_len = s // n_cls
    outs = [_dsa_class(qi, wit, qb, ki, kb, vb, tq, k_sel, c * q_len, q_len, (c + 1) * q_len)
            for c in range(n_cls)]
    return jnp.concatenate(outs, axis=1)


def _merge_kernel(oa_ref, ob_ref, ga_ref, gb_ref, x_ref, mod_ref, wba_ref, wbb_ref, wo_ref,
                  ln1_ref, wr_ref, rb_ref, x1_ref, u2_ref, gk_ref, u2t_ref, expk_ref, rankk_ref,
                  seen_scr, *, alpha):
    tm = x_ref.shape[1]
    ya = _dot(oa_ref[0], wba_ref[...])
    yb = _dot(ob_ref[0], wbb_ref[...])
    merged = _sigmoid(ga_ref[0].astype(F32)) * ya + _sigmoid(gb_ref[0].astype(F32)) * yb
    mix = _dot(merged.astype(BF16), wo_ref[...])
    mod = mod_ref[0]
    g1 = mod[2:3, :]
    sh2 = mod[3:4, :]
    sc2 = mod[4:5, :]
    ln1 = ln1_ref[...]
    x1 = _ln(alpha * x_ref[0] + g1 * mix) * ln1[0:1, :] + ln1[1:2, :]
    x1_ref[0] = x1
    u2 = _ln(x1) * (1.0 + sc2) + sh2
    u2_ref[0] = u2.astype(BF16)

    u_hi, u_lo = _split_bf16(u2)
    both = _dot(u_hi, wr_ref[...])
    logits = both[:, 0:LANES] + both[:, LANES:] + _dot(u_lo, wr_ref[:, 0:LANES])
    aff = _sigmoid(logits.T[0:N_EXPERTS, :])
    biased = aff + rb_ref[...]
    grp = biased.reshape(N_GROUPS, GROUP_SIZE, tm)
    sub = lax.broadcasted_iota(jnp.int32, grp.shape, 1)
    m1 = jnp.max(grp, axis=1, keepdims=True)
    first = jnp.min(jnp.where(grp == m1, sub, GROUP_SIZE), axis=1, keepdims=True)
    m2 = jnp.max(jnp.where(sub == first, NEG_INF, grp), axis=1, keepdims=True)
    gscore = (m1 + m2).reshape(N_GROUPS, tm)
    g_i = lax.broadcasted_iota(jnp.int32, (N_GROUPS, tm), 0)
    g_rank = jnp.zeros((N_GROUPS, tm), F32)
    for g in range(N_GROUPS):
        other = gscore[g:g + 1, :]
        beats = (other > gscore) | ((other == gscore) & (g < g_i))
        g_rank = g_rank + jnp.where(beats, 1.0, 0.0)
    g_keep = jnp.where(g_rank < TOPK_GROUPS, 1.0, 0.0).reshape(N_GROUPS, 1, tm)
    e_keep = jnp.broadcast_to(g_keep, (N_GROUPS, GROUP_SIZE, tm)).reshape(N_EXPERTS, tm)
    masked = jnp.where(e_keep > 0.5, biased, NEG_INF)
    e_i = lax.broadcasted_iota(jnp.int32, (N_EXPERTS, tm), 0)
    chosen = jnp.zeros((N_EXPERTS, tm), F32)
    picks = []
    for _ in range(TOPK_EXPERTS):
        best = jnp.max(masked, axis=0, keepdims=True)
        first = jnp.min(jnp.where(masked == best, e_i, N_EXPERTS), axis=0, keepdims=True)
        hit = e_i == first
        chosen = jnp.where(hit, 1.0, chosen)
        masked = jnp.where(hit, NEG_INF, masked)
        picks.append(first)
    top_aff = jnp.where(chosen > 0.5, aff, 0.0)
    comb_t = top_aff / jnp.sum(top_aff, axis=0, keepdims=True) * ROUTED_SCALE

    @pl.when((pl.program_id(0) == 0) & (pl.program_id(1) == 0))
    def _():
        seen_scr[...] = jnp.zeros_like(seen_scr)

    r_i = lax.broadcasted_iota(jnp.int32, (tm, tm), 0)
    c_i = lax.broadcasted_iota(jnp.int32, (tm, tm), 1)
    earlier = jnp.where(r_i < c_i, 1.0, 0.0).astype(BF16)
    arrival = _dot(chosen.astype(BF16), earlier) + seen_scr[:, 0:1]
    seen_scr[...] = seen_scr[...] + jnp.sum(chosen, axis=1, keepdims=True)
    ranks, gates = [], []
    for first in picks:
        hit = e_i == first
        ranks.append(jnp.sum(jnp.where(hit, arrival, 0.0), axis=0, keepdims=True))
        gates.append(jnp.sum(jnp.where(hit, comb_t, 0.0), axis=0, keepdims=True))
    expk_ref[...] = jnp.concatenate(picks, axis=0)
    rankk_ref[...] = jnp.concatenate(ranks, axis=0).astype(jnp.int32)
    gate_rows = jnp.concatenate(gates + [jnp.zeros((LANES - TOPK_EXPERTS, tm), F32)], axis=0)
    gk_ref[0] = gate_rows.T
    for c in range(TILE_ROWS):
        u2t_ref[pl.ds(c, tm, stride=TILE_ROWS), :] = u2[:, c * LANES:(c + 1) * LANES]


def _merge(oa, ob, ga, gb, x, mod, wba, wbb, wo, ln1, wr, rb, tm, alpha):
    bsz, s, d = x.shape
    row = lambda b, i: (b, i, 0)
    per_batch = s // tm
    flat = lambda b, i: (b * per_batch + i, 0)
    cols = lambda b, i: (0, b * per_batch + i)
    w2 = lambda shape: pl.BlockSpec(shape, lambda b, i: (0,) * len(shape))
    return pl.pallas_call(
        functools.partial(_merge_kernel, alpha=alpha),
        out_shape=[jax.ShapeDtypeStruct((bsz, s, d), F32),
                   jax.ShapeDtypeStruct((bsz, s, d), BF16),
                   jax.ShapeDtypeStruct((bsz, s, LANES), F32),
                   jax.ShapeDtypeStruct((bsz * s * TILE_ROWS, LANES), F32),
                   jax.ShapeDtypeStruct((TOPK_EXPERTS, bsz * s), jnp.int32),
                   jax.ShapeDtypeStruct((TOPK_EXPERTS, bsz * s), jnp.int32)],
        grid=(bsz, s // tm),
        in_specs=[
            pl.BlockSpec((1, tm, WIDTH), row), pl.BlockSpec((1, tm, WIDTH), row),
            pl.BlockSpec((1, tm, d), row), pl.BlockSpec((1, tm, d), row),
            pl.BlockSpec((1, tm, d), row),
            pl.BlockSpec((1, N_MOD, d), lambda b, i: (b, 0, 0)),
            w2(wba.shape), w2(wbb.shape), w2(wo.shape), w2(ln1.shape), w2(wr.shape), w2(rb.shape),
        ],
        out_specs=[pl.BlockSpec((1, tm, d), row), pl.BlockSpec((1, tm, d), row),
                   pl.BlockSpec((1, tm, LANES), row),
                   pl.BlockSpec((tm * TILE_ROWS, LANES), flat),
                   pl.BlockSpec((TOPK_EXPERTS, tm), cols),
                   pl.BlockSpec((TOPK_EXPERTS, tm), cols)],
        scratch_shapes=[pltpu.VMEM((N_EXPERTS, LANES), F32)],
        compiler_params=_cparams(("arbitrary", "arbitrary")),
        name="merge",
    )(oa, ob, ga, gb, x, mod, wba, wbb, wo, ln1, wr, rb)


def _silu(z):
    return z * _sigmoid(z)


def _sc_params():
    return pltpu.CompilerParams(use_tc_tiling_on_sc=False)


def _sc_mesh():
    return plsc.VectorSubcoreMesh(core_axis_name="c", subcore_axis_name="s",
                                  num_cores=SC_CORES, num_subcores=SC_SUBCORES)


def _sc_dispatch(x_tiles, dest, n_rows):
    n_tok = x_tiles.shape[0]
    n_k = dest.shape[0]
    mesh = _sc_mesh()

    @pl.kernel(out_type=jax.ShapeDtypeStruct((n_rows,) + x_tiles.shape[1:], x_tiles.dtype), mesh=mesh,
               scratch_types=[], compiler_params=_sc_params())
    def scatter_rows(x_hbm, *rest):
        idx_hbm, o_hbm = rest[:n_k], rest[n_k]

        def body(x_vmem, *idx_vmem):
            for k in range(n_k):
                pltpu.sync_copy(x_vmem, o_hbm.at[idx_vmem[k].at[0]])

        pltpu.emit_pipeline(
            body, grid=(n_tok // SC_WINDOW,),
            in_specs=[pl.BlockSpec((SC_WINDOW,) + x_tiles.shape[1:], lambda i: (i, 0, 0))]
            + [pl.BlockSpec((1, SC_WINDOW), lambda i: (0, i))] * n_k,
            out_specs=[], core_axis_name=("c", "s"), dimension_semantics=(pltpu.PARALLEL,),
        )(x_hbm, *idx_hbm)

    return scatter_rows(x_tiles, *[dest[k:k + 1] for k in range(n_k)])


def _sc_gather(rows, idx):
    n = idx.shape[1]
    mesh = _sc_mesh()

    @pl.kernel(out_type=jax.ShapeDtypeStruct((n,) + rows.shape[1:], rows.dtype), mesh=mesh,
               scratch_types=[], compiler_params=_sc_params())
    def gather_rows(r_hbm, i_hbm, o_hbm):
        def body(i_vmem, o_vmem):
            pltpu.sync_copy(r_hbm.at[i_vmem.at[0]], o_vmem)

        pltpu.emit_pipeline(
            body, grid=(n // SC_WINDOW,),
            in_specs=[pl.BlockSpec((1, SC_WINDOW), lambda i: (0, i))],
            out_specs=[pl.BlockSpec((SC_WINDOW,) + rows.shape[1:], lambda i: (i, 0, 0))],
            core_axis_name=("c", "s"), dimension_semantics=(pltpu.PARALLEL,),
        )(i_hbm, o_hbm)

    return gather_rows(rows, idx)


def _tile_rows(ref, n_rows, lead=()):
    return jnp.concatenate([ref[lead + (pl.ds(c, n_rows, stride=TILE_ROWS), slice(None))]
                            for c in range(TILE_ROWS)], axis=1)


def _ffn_grouped_kernel(expert_ref, valid_ref, used_ref, x_ref, wg_ref, wu_ref, wd_ref, y_ref):
    i = pl.program_id(0)
    n_rows = x_ref.shape[0] // TILE_ROWS

    @pl.when(i < used_ref[0])
    def _():
        row = lax.broadcasted_iota(jnp.int32, (n_rows, 1), 0)
        x = jnp.where(row < valid_ref[i], _tile_rows(x_ref, n_rows), 0.0).astype(BF16)
        h = _silu(_dot(x, wg_ref[0])) * _dot(x, wu_ref[0])
        y = _dot(h.astype(BF16), wd_ref[0])
        for c in range(TILE_ROWS):
            y_ref[pl.ds(c, n_rows, stride=TILE_ROWS), :] = y[:, c * LANES:(c + 1) * LANES]


def _ffn_grouped(tile_expert, tile_valid, tiles_used, xs, wg, wu, wd, n_tiles):
    _, d, d_exp = wg.shape
    blk = pl.BlockSpec((MOE_ROW_TILE * TILE_ROWS, LANES), lambda i, te, tv, nu: (i, 0))
    weight = lambda shape: pl.BlockSpec(shape, lambda i, te, tv, nu: (te[i], 0, 0))
    return pl.pallas_call(
        _ffn_grouped_kernel,
        out_shape=jax.ShapeDtypeStruct(xs.shape, F32),
        grid_spec=pltpu.PrefetchScalarGridSpec(
            num_scalar_prefetch=3, grid=(n_tiles,),
            in_specs=[blk, weight((1, d, d_exp)), weight((1, d, d_exp)), weight((1, d_exp, d))],
            out_specs=blk),
        compiler_params=_cparams(("arbitrary",)),
        name="moe_ffn",
    )(tile_expert, tile_valid, tiles_used, xs, wg, wu, wd)


def _moe_final_kernel(yg_ref, gk_ref, u2_ref, x1_ref, mod_ref, wsg_ref, wsu_ref, wsd_ref, ln2_ref, o_ref,
                      *, alpha):
    tm = x1_ref.shape[0]
    t = u2_ref[...]
    ffn = _dot((_silu(_dot(t, wsg_ref[...])) * _dot(t, wsu_ref[...])).astype(BF16), wsd_ref[...])
    gk = gk_ref[...]
    for k in range(TOPK_EXPERTS):
        ffn = ffn + _tile_rows(yg_ref, tm, (k,)) * gk[:, k:k + 1]
    mod = mod_ref[0]
    g2 = mod[5:6, :]
    ln2 = ln2_ref[...]
    o_ref[...] = _ln(alpha * x1_ref[...] + g2 * ffn) * ln2[0:1, :] + ln2[1:2, :]


def _moe_final(yg, gk, u2, x1, mod, wsg, wsu, wsd, ln2, tm, tokens_per_batch, alpha):
    t, d = u2.shape
    blocks_per_batch = tokens_per_batch // tm
    row = lambda i: (i, 0)
    w2 = lambda shape: pl.BlockSpec(shape, lambda i: (0,) * len(shape))
    return pl.pallas_call(
        functools.partial(_moe_final_kernel, alpha=alpha),
        out_shape=jax.ShapeDtypeStruct((t, d), F32),
        grid=(t // tm,),
        in_specs=[
            pl.BlockSpec((TOPK_EXPERTS, tm * TILE_ROWS, LANES), lambda i: (0, i, 0)),
            pl.BlockSpec((tm, LANES), row),
            pl.BlockSpec((tm, d), row),
            pl.BlockSpec((tm, d), row),
            pl.BlockSpec((1, N_MOD, d), lambda i: (i // blocks_per_batch, 0, 0)),
            w2(wsg.shape), w2(wsu.shape), w2(wsd.shape), w2(ln2.shape),
        ],
        out_specs=pl.BlockSpec((tm, d), row),
        compiler_params=_cparams(("arbitrary",)),
        name="moe_final",
    )(yg, gk, u2, x1, mod, wsg, wsu, wsd, ln2)


def _sorted_layout(expk, rankk, n_tiles):
    e_ids = jnp.arange(N_EXPERTS, dtype=jnp.int32)
    is_e = expk[..., None] == e_ids
    counts = jnp.sum(is_e, axis=(0, 1), dtype=jnp.int32)
    tiles = (counts + MOE_ROW_TILE - 1) // MOE_ROW_TILE
    tile_end = jnp.cumsum(tiles)
    first_row = (tile_end - tiles) * MOE_ROW_TILE
    dest = jnp.sum(jnp.where(is_e, first_row, 0), axis=-1, dtype=jnp.int32) + rankk
    tile_ids = jnp.arange(n_tiles, dtype=jnp.int32)
    tile_expert = jnp.minimum(jnp.sum(tile_ids[:, None] >= tile_end[None, :], axis=1, dtype=jnp.int32),
                              N_EXPERTS - 1)
    rows_before = (tile_ids - (tile_end - tiles)[tile_expert]) * MOE_ROW_TILE
    tile_valid = jnp.clip(counts[tile_expert] - rows_before, 0, MOE_ROW_TILE)
    return dest, tile_expert, tile_valid, tile_end[-1:]


def _rope_tables(positions):
    inv_freq = ROPE_THETA ** (-jnp.arange(0, ROPE_DIM, 2, dtype=F32) / ROPE_DIM)
    ang = positions.astype(F32)[..., None] * inv_freq
    cos, sin = jnp.cos(ang), jnp.sin(ang)
    ones = jnp.ones(cos.shape[:-1] + (HEAD_DIM - ROPE_DIM,), F32)
    zeros = jnp.zeros_like(ones)
    zh = jnp.zeros_like(sin)
    c = jnp.concatenate([cos, cos, ones], axis=-1)
    s1 = jnp.concatenate([-sin, zh, zeros], axis=-1)
    s2 = jnp.concatenate([zh, sin, zeros], axis=-1)
    reps = LANES // HEAD_DIM
    return jnp.concatenate([jnp.tile(c, reps), jnp.tile(s1, reps), jnp.tile(s2, reps)], axis=-1)


def _band_bias_table(rel_bias, tq):
    n_heads = rel_bias.shape[0]
    win = tq + BAND_PAD
    row_len = win + tq
    n_high = BAND_PAD - MAX_REL + 1
    n_ramp = 2 * MAX_REL - 1
    high = rel_bias[:, 2 * MAX_REL:]
    profile = jnp.concatenate([
        jnp.broadcast_to(high, (n_heads, n_high)),
        rel_bias[:, n_ramp:0:-1],
        jnp.broadcast_to(rel_bias[:, :1], (n_heads, win - n_high - n_ramp)),
        jnp.broadcast_to(high, (n_heads, tq)),
    ], axis=1)
    skew = jnp.broadcast_to(profile[:, None, :], (n_heads, tq, row_len)).reshape(n_heads, tq * row_len)
    skew = skew[:, :tq * (row_len - 1)].reshape(n_heads, tq, row_len - 1)[:, :, :win]
    qi = np.arange(tq)[:, None]
    kj = np.arange(win)[None, :]
    q_chunk = qi // CHUNK
    k_chunk = kj // CHUNK - (BAND_CHUNKS - 1)
    in_band = (k_chunk <= q_chunk) & (k_chunk >= q_chunk - (BAND_CHUNKS - 1))
    return jnp.where(jnp.asarray(in_band)[None], skew, NEG_INF)


def kernel(x, c, positions, w_ada, b_ada, w_in, rel_bias, idx_k_norm_g, idx_k_norm_b, w_branch_a,
           w_branch_b, w_out, ln1_g, ln1_b, w_router, router_bias, w_exp_gate, w_exp_up, w_exp_down,
           w_sh_gate, w_sh_up, w_sh_down, ln2_g, ln2_b):
    bsz, s, d = x.shape
    depth = w_ada.shape[0]
    alpha = (2.0 * depth) ** 0.25
    tm_proj = min(512, s)
    tq_band = min(256, s)
    tq_dsa = min(256, s)
    tm_merge = min(512, s)
    tm_moe = min(256, s)

    rope_tab = _rope_tables(positions)
    n7 = 7 * WIDTH
    for l in range(depth):
        mod = _ada(c, w_ada[l], b_ada[l]).reshape(bsz, N_MOD, d)
        w_l = w_in[l]
        w7 = w_l[:, :n7].astype(BF16)
        n_kw = IDX_DIM + N_IDX_HEADS
        wkw = jnp.pad(w_l[:, n7:n7 + n_kw], ((0, 0), (0, LANES - n_kw))).astype(BF16)
        wg = w_l[:, n7 + n_kw:].astype(BF16)
        lnk = jnp.pad(jnp.stack([idx_k_norm_g[l], idx_k_norm_b[l]]), ((0, 0), (0, LANES - IDX_DIM)))
        qa, ka, va, qb, kb, vb, qi, ki, wi, ga, gb = _proj(x, mod, rope_tab, w7, wkw, wg, lnk, tm_proj)

        oa = _band(qa, ka, va, _band_bias_table(rel_bias[l], tq_band), tq_band)
        ob = _dsa(qi, wi, qb, ki, kb, vb, tq_dsa)

        wr = jnp.pad(w_router[l], ((0, 0), (0, LANES - N_EXPERTS)))
        wr_hi = wr.astype(BF16)
        wr_lo = (wr - wr_hi.astype(F32)).astype(BF16)
        x1, u2, gk, u2_tiles, expk, rankk = _merge(
            oa, ob, ga, gb, x, mod,
            w_branch_a[l].astype(BF16), w_branch_b[l].astype(BF16), w_out[l].astype(BF16),
            jnp.stack([ln1_g[l], ln1_b[l]]), jnp.concatenate([wr_hi, wr_lo], axis=1),
            router_bias[l].reshape(N_EXPERTS, 1), tm_merge, alpha)

        t = bsz * s
        n_tiles = t * TOPK_EXPERTS // MOE_ROW_TILE + N_EXPERTS
        n_rows = n_tiles * MOE_ROW_TILE
        dest, tile_expert, tile_valid, tiles_used = _sorted_layout(expk, rankk, n_tiles)
        xs = _sc_dispatch(u2_tiles.reshape(t, TILE_ROWS, LANES), dest, n_rows)
        ys = _ffn_grouped(tile_expert, tile_valid, tiles_used, xs.reshape(n_rows * TILE_ROWS, LANES),
                          w_exp_gate[l].astype(BF16), w_exp_up[l].astype(BF16), w_exp_down[l].astype(BF16),
                          n_tiles)
        yg = _sc_gather(ys.reshape(n_rows, TILE_ROWS, LANES), dest.reshape(1, TOPK_EXPERTS * t))
        out = _moe_final(
            yg.reshape(TOPK_EXPERTS, t * TILE_ROWS, LANES), gk.reshape(t, LANES), u2.reshape(t, d),
            x1.reshape(t, d), mod,
            w_sh_gate[l].astype(BF16), w_sh_up[l].astype(BF16), w_sh_down[l].astype(BF16),
            jnp.stack([ln2_g[l], ln2_b[l]]), tm_moe, s, alpha)
        x = out.reshape(bsz, s, d)
    return x
```

```python
import functools

import jax
import jax.numpy as jnp
import numpy as np
from jax import lax
from jax.experimental import pallas as pl
from jax.experimental.pallas import tpu as pltpu
from jax.experimental.pallas import tpu_sc as plsc

F32 = jnp.float32
BF16 = jnp.bfloat16
NEG_INF = float("-inf")

CHUNK = 64
HEAD_DIM = 64
N_HEADS = 8
WIDTH = N_HEADS * HEAD_DIM
BAND_CHUNKS = 9
BAND_PAD = (BAND_CHUNKS - 1) * CHUNK
MAX_REL = 128
ROPE_THETA = 500000.0
ROPE_DIM = HEAD_DIM // 4
ROPE_HALF = ROPE_DIM // 2
N_IDX_HEADS = 8
IDX_DIM = 64
TOPK_MAX = 256
N_EXPERTS = 64
N_GROUPS = 8
GROUP_SIZE = N_EXPERTS // N_GROUPS
TOPK_GROUPS = 4
TOPK_EXPERTS = 8
ROUTED_SCALE = 1.0
N_MOD = 6
LN_EPS = 1e-5
LANES = 128
KEY_TILE = 256

VMEM_LIMIT = 56 * 1024 * 1024

DSA_KEY_CLASSES = 8
TILE_ROWS = 8
MOE_ROW_TILE = 512
SC_WINDOW = 32
MOE_COMBINE_CHUNKS = 4
SC_CORES = 2
SC_SUBCORES = 16


def _cparams(sem):
    return pltpu.CompilerParams(dimension_semantics=sem, vmem_limit_bytes=VMEM_LIMIT)


def _ln(z):
    mu = jnp.mean(z, axis=-1, keepdims=True)
    zc = z - mu
    var = jnp.mean(zc * zc, axis=-1, keepdims=True)
    return zc * lax.rsqrt(var + LN_EPS)


def _sigmoid(z):
    return 1.0 / (1.0 + jnp.exp(-z))


def _dot(a, b):
    return jnp.dot(a, b, preferred_element_type=F32)


def _dot_nt(a, b):
    return lax.dot_general(a, b, (((1,), (1,)), ((), ())), preferred_element_type=F32)


def _split_bf16(z):
    hi = z.astype(BF16)
    lo = (z - hi.astype(F32)).astype(BF16)
    return hi, lo


def _ada_kernel(c_ref, w_ref, b_ref, o_ref):
    c = c_ref[...]
    ca = c * _sigmoid(c)
    ca_hi, ca_lo = _split_bf16(ca)
    w = w_ref[...]
    w_hi, w_lo = _split_bf16(w)
    acc = _dot(ca_hi, w_hi) + _dot(ca_lo, w_hi) + _dot(ca_hi, w_lo)
    o_ref[...] = acc + b_ref[...]


def _ada(c, w_ada, b_ada):
    bsz, d = c.shape
    n = w_ada.shape[1]
    tn = 1024
    return pl.pallas_call(
        _ada_kernel,
        out_shape=jax.ShapeDtypeStruct((bsz, n), F32),
        grid=(n // tn,),
        in_specs=[
            pl.BlockSpec((bsz, d), lambda j: (0, 0)),
            pl.BlockSpec((d, tn), lambda j: (0, j)),
            pl.BlockSpec((1, tn), lambda j: (0, j)),
        ],
        out_specs=pl.BlockSpec((bsz, tn), lambda j: (0, j)),
        compiler_params=_cparams(("arbitrary",)),
        name="ada",
    )(c, w_ada, b_ada.reshape(1, n))


def _rope(z, c_t, s1_t, s2_t):
    n = z.shape[-1]
    return z * c_t + pltpu.roll(z, n - ROPE_HALF, 1) * s1_t + pltpu.roll(z, ROPE_HALF, 1) * s2_t


def _proj_kernel(x_ref, mod_ref, rope_ref, w7_ref, wkw_ref, wg_ref, lnk_ref,
                 qa_ref, ka_ref, va_ref, qb_ref, kb_ref, vb_ref, qi_ref, ki_ref, wi_ref,
                 ga_ref, gb_ref):
    x = x_ref[0]
    mod = mod_ref[0]
    sh1 = mod[0:1, :]
    sc1 = mod[1:2, :]
    u = (_ln(x) * (1.0 + sc1) + sh1).astype(BF16)

    rope = rope_ref[0]
    c1 = rope[:, 0:LANES]
    s1 = rope[:, LANES:2 * LANES]
    s2 = rope[:, 2 * LANES:3 * LANES]
    reps = WIDTH // LANES
    c_t = jnp.concatenate([c1] * reps, axis=1)
    s1_t = jnp.concatenate([s1] * reps, axis=1)
    s2_t = jnp.concatenate([s2] * reps, axis=1)

    att_scale = HEAD_DIM ** -0.5
    idx_scale = IDX_DIM ** -0.5

    def seg(k):
        return _dot(u, w7_ref[:, k * WIDTH:(k + 1) * WIDTH])

    qa_ref[0] = (seg(0) * att_scale).astype(BF16)
    ka_ref[0] = seg(1).astype(BF16)
    va_ref[0] = seg(2).astype(BF16)
    qb_ref[0] = (_rope(seg(3), c_t, s1_t, s2_t) * att_scale).astype(BF16)
    kb_ref[0] = _rope(seg(4), c_t, s1_t, s2_t).astype(BF16)
    vb_ref[0] = seg(5).astype(BF16)
    qi_ref[0] = (_rope(seg(6), c_t, s1_t, s2_t) * idx_scale).astype(BF16)

    z = _dot(u, wkw_ref[...])
    lane = lax.broadcasted_iota(jnp.int32, z.shape, 1)
    is_k = lane < IDX_DIM
    mu = jnp.sum(jnp.where(is_k, z, 0.0), axis=-1, keepdims=True) * (1.0 / IDX_DIM)
    zc = jnp.where(is_k, z - mu, 0.0)
    var = jnp.sum(zc * zc, axis=-1, keepdims=True) * (1.0 / IDX_DIM)
    lnk = lnk_ref[...]
    y = zc * lax.rsqrt(var + LN_EPS) * lnk[0:1, :] + lnk[1:2, :]
    y = _rope(y, c1, jnp.where(is_k, s1, 0.0), jnp.where(is_k, s2, 0.0))
    ki_ref[0] = y[:, 0:IDX_DIM].astype(BF16)
    wi_ref[0] = z.T[IDX_DIM:IDX_DIM + N_IDX_HEADS, :] * (N_IDX_HEADS ** -0.5)

    d = ga_ref.shape[-1]
    ga_ref[0] = _dot(u, wg_ref[:, 0:d]).astype(BF16)
    gb_ref[0] = _dot(u, wg_ref[:, d:2 * d]).astype(BF16)


def _proj(x, mod, rope_tab, w7, wkw, wg, lnk, tm):
    bsz, s, d = x.shape
    const = dict(pipeline_mode=pl.Buffered(1))
    row = lambda b, i: (b, i, 0)
    wspec = lambda shape: pl.BlockSpec(shape, lambda b, i: (0, 0), **const)
    out_w = jax.ShapeDtypeStruct((bsz, s, WIDTH), BF16)
    out_d = jax.ShapeDtypeStruct((bsz, s, d), BF16)
    return pl.pallas_call(
        _proj_kernel,
        out_shape=[out_w] * 7 + [
            jax.ShapeDtypeStruct((bsz, s, IDX_DIM), BF16),
            jax.ShapeDtypeStruct((bsz, N_IDX_HEADS, s), F32),
            out_d, out_d],
        grid=(bsz, s // tm),
        in_specs=[
            pl.BlockSpec((1, tm, d), row),
            pl.BlockSpec((1, N_MOD, d), lambda b, i: (b, 0, 0)),
            pl.BlockSpec((1, tm, 3 * LANES), row),
            wspec(w7.shape), wspec(wkw.shape), wspec(wg.shape), wspec(lnk.shape),
        ],
        out_specs=[pl.BlockSpec((1, tm, WIDTH), row)] * 7 + [
            pl.BlockSpec((1, tm, IDX_DIM), row),
            pl.BlockSpec((1, N_IDX_HEADS, tm), lambda b, i: (b, 0, i)),
            pl.BlockSpec((1, tm, d), row), pl.BlockSpec((1, tm, d), row)],
        compiler_params=_cparams(("arbitrary", "arbitrary")),
        name="proj",
    )(x, mod, rope_tab, w7, wkw, wg, lnk)


def _band_kernel(q_ref, k_ref, v_ref, bias_ref, o_ref, kpad, vpad, a_scr, p_scr, *, tq):
    i = pl.program_id(1)
    s = k_ref.shape[1]
    win = tq + BAND_PAD

    @pl.when(i == 0)
    def _():
        zeros = jnp.zeros((BAND_PAD, WIDTH), BF16)
        kpad[0:BAND_PAD, :] = zeros
        vpad[0:BAND_PAD, :] = zeros
        kpad[BAND_PAD:BAND_PAD + s, :] = k_ref[0]
        vpad[BAND_PAD:BAND_PAD + s, :] = v_ref[0]

    q0 = pl.multiple_of(i * tq, tq)
    n_lg = win // LANES
    col = lax.broadcasted_iota(jnp.int32, (tq, LANES), 1)
    first_frame = BAND_PAD - q0
    heads = [slice(h * HEAD_DIM, (h + 1) * HEAD_DIM) for h in range(N_HEADS)]

    def lanes(g):
        return slice(g * LANES, (g + 1) * LANES)

    row_max = []
    for h, sl in enumerate(heads):
        sc = _dot_nt(q_ref[0, :, sl], kpad[pl.ds(q0, win), sl])
        mx = jnp.full((tq, LANES), NEG_INF, F32)
        for g in range(n_lg):
            a = jnp.where(col >= first_frame - g * LANES, sc[:, lanes(g)] + bias_ref[h, :, lanes(g)], NEG_INF)
            a_scr[h, :, lanes(g)] = a
            mx = jnp.maximum(mx, a)
        row_max.append(jnp.max(mx, axis=-1, keepdims=True))
    row_sum = []
    for h in range(N_HEADS):
        m_b = jnp.broadcast_to(row_max[h], (tq, LANES))
        ls = jnp.zeros((tq, LANES), F32)
        for g in range(n_lg):
            p = jnp.exp(a_scr[h, :, lanes(g)] - m_b)
            ls = ls + p
            p_scr[h, :, lanes(g)] = p.astype(BF16)
        row_sum.append(jnp.sum(ls, axis=-1, keepdims=True))
    for h, sl in enumerate(heads):
        o = _dot(p_scr[h], vpad[pl.ds(q0, win), sl]) / row_sum[h]
        o_ref[0, :, sl] = o.astype(BF16)


def _band(qa, ka, va, bias_tab, tq):
    bsz, s, _ = qa.shape
    win = tq + BAND_PAD
    return pl.pallas_call(
        functools.partial(_band_kernel, tq=tq),
        out_shape=jax.ShapeDtypeStruct((bsz, s, WIDTH), BF16),
        grid=(bsz, s // tq),
        in_specs=[
            pl.BlockSpec((1, tq, WIDTH), lambda b, i: (b, i, 0)),
            pl.BlockSpec((1, s, WIDTH), lambda b, i: (b, 0, 0)),
            pl.BlockSpec((1, s, WIDTH), lambda b, i: (b, 0, 0)),
            pl.BlockSpec((N_HEADS, tq, win), lambda b, i: (0, 0, 0)),
        ],
        out_specs=pl.BlockSpec((1, tq, WIDTH), lambda b, i: (b, i, 0)),
        scratch_shapes=[pltpu.VMEM((BAND_PAD + s, WIDTH), BF16),
                        pltpu.VMEM((BAND_PAD + s, WIDTH), BF16),
                        pltpu.VMEM((N_HEADS, tq, win), F32),
                        pltpu.VMEM((N_HEADS, tq, win), BF16)],
        compiler_params=_cparams(("arbitrary", "arbitrary")),
        name="band",
    )(qa, ka, va, bias_tab)


BISECT_STEPS_PER_CHECK = 4
BISECT_MAX_CHECKS = 400
COUNT_ROWS = 64


def _dsa_kernel(qi_ref, wit_ref, qb_ref, ki_ref, kb_ref, vb_ref, o_ref, sct_scr, msk_scr, a_scr, p_scr,
                *, tq, k_sel, q_base):
    i = pl.program_id(1)
    sk = ki_ref.shape[1]
    n_kt = sk // KEY_TILE
    n_lg = sk // LANES
    q0 = q_base + i * tq
    kf = float(k_sel)

    def lanes(g):
        return slice(g * LANES, (g + 1) * LANES)

    def slab_reduce(fn, combine, init):
        acc = jnp.full((COUNT_ROWS, tq), init, F32)
        for r in range(sk // COUNT_ROWS):
            acc = combine(acc, fn(sct_scr[r * COUNT_ROWS:(r + 1) * COUNT_ROWS, :]))
        return acc

    def count(pred):
        part = slab_reduce(lambda t: jnp.where(pred(t), 1.0, 0.0), jnp.add, 0.0)
        return jnp.sum(part, axis=0, keepdims=True)

    w_t = wit_ref[0]
    t_pos = q0 + lax.broadcasted_iota(jnp.int32, (1, tq), 1)
    limit = (t_pos // CHUNK + 1) * CHUNK
    key_in_tile = lax.broadcasted_iota(jnp.int32, (KEY_TILE, tq), 0)
    for kt in range(n_kt):
        ks = slice(kt * KEY_TILE, (kt + 1) * KEY_TILE)
        ki_t = ki_ref[0, ks, :]
        acc = jnp.zeros((KEY_TILE, tq), F32)
        for h in range(N_IDX_HEADS):
            lg = _dot_nt(ki_t, qi_ref[0, :, h * IDX_DIM:(h + 1) * IDX_DIM])
            acc = acc + jnp.maximum(lg, 0.0) * w_t[h:h + 1, :]
        sct_scr[ks, :] = jnp.where(key_in_tile < limit - kt * KEY_TILE, acc, NEG_INF)

    smax = jnp.max(slab_reduce(lambda t: t, jnp.maximum, NEG_INF), axis=0, keepdims=True)
    smin = jnp.min(slab_reduce(lambda t: jnp.where(t == NEG_INF, jnp.inf, t), jnp.minimum, jnp.inf),
                   axis=0, keepdims=True)
    n_adm = limit.astype(F32)
    c_max = count(lambda t: t >= smax)
    c_pos = count(lambda t: t > 0.0)
    c_nn = count(lambda t: t >= 0.0)
    zero = jnp.zeros_like(smax)
    at_zero = (c_pos < kf) & (c_nn >= kf)
    below_zero = c_nn < kf
    lo = jnp.where(at_zero | ~below_zero, zero, smin)
    clo = jnp.where(at_zero | ~below_zero, c_nn, n_adm)
    hi = jnp.where(at_zero | below_zero, zero, smax)
    few = n_adm <= kf
    lo = jnp.where(few, smin, lo)
    clo = jnp.where(few, n_adm, clo)
    top_tied = (c_max >= kf) & ~few
    lo = jnp.where(top_tied, smax, lo)
    clo = jnp.where(top_tied, c_max, clo)
    hi = jnp.where(top_tied, smax, hi)
    done0 = jnp.where(few | at_zero | top_tied, 1.0, 0.0)

    def bisect(carry):
        lo, hi, clo, done, it = carry
        for _ in range(BISECT_STEPS_PER_CHECK):
            mid = 0.5 * lo + 0.5 * hi
            stuck = (mid <= lo) | (mid >= hi)
            c = count(lambda t: t >= mid)
            ge = c >= kf
            lo = jnp.where(ge, mid, lo)
            clo = jnp.where(ge, c, clo)
            hi = jnp.where(ge, hi, mid)
            done = jnp.where(stuck | (clo <= kf), 1.0, done)
        return lo, hi, clo, done, it + 1

    def not_converged(carry):
        _, _, _, done, it = carry
        return (jnp.min(done) < 0.5) & (it < BISECT_MAX_CHECKS)

    lo, hi, clo, _, _ = lax.while_loop(not_converged, bisect, (lo, hi, clo, done0, jnp.int32(0)))

    thr = lo
    c_gt = count(lambda t: t > thr)
    c_eq = count(lambda t: t == thr)
    need = kf - c_gt
    tie_cut = jnp.max(c_gt + c_eq - kf) > 0.0

    def with_ties():
        r_i = lax.broadcasted_iota(jnp.int32, (KEY_TILE, KEY_TILE), 0)
        c_i = lax.broadcasted_iota(jnp.int32, (KEY_TILE, KEY_TILE), 1)
        earlier = jnp.where(c_i < r_i, 1.0, 0.0).astype(BF16)
        carry = jnp.zeros((1, tq), F32)
        for kt in range(n_kt):
            ks = slice(kt * KEY_TILE, (kt + 1) * KEY_TILE)
            sc = sct_scr[ks, :]
            eq = sc == thr
            eq_f = jnp.where(eq, 1.0, 0.0)
            rank = _dot(earlier, eq_f.astype(BF16)) + carry
            keep = (sc > thr) | (eq & (rank < need))
            msk_scr[:, ks] = jnp.where(keep, 0.0, NEG_INF).T
            carry = carry + jnp.sum(eq_f, axis=0, keepdims=True)

    def without_ties():
        for kt in range(n_kt):
            ks = slice(kt * KEY_TILE, (kt + 1) * KEY_TILE)
            msk_scr[:, ks] = jnp.where(sct_scr[ks, :] >= thr, 0.0, NEG_INF).T

    lax.cond(tie_cut, with_ties, without_ties)

    heads = [slice(h * HEAD_DIM, (h + 1) * HEAD_DIM) for h in range(N_HEADS)]
    row_max = []
    for h, sl in enumerate(heads):
        qh = qb_ref[0, :, sl]
        mx = jnp.full((tq, LANES), NEG_INF, F32)
        for kt in range(n_kt):
            ks = slice(kt * KEY_TILE, (kt + 1) * KEY_TILE)
            a = _dot_nt(qh, kb_ref[0, ks, sl]) + msk_scr[:, ks]
            a_scr[h, :, ks] = a
            for g in range(KEY_TILE // LANES):
                mx = jnp.maximum(mx, a[:, lanes(g)])
        row_max.append(jnp.max(mx, axis=-1, keepdims=True))
    row_sum = []
    for h in range(N_HEADS):
        m_b = jnp.broadcast_to(row_max[h], (tq, LANES))
        ls = jnp.zeros((tq, LANES), F32)
        for g in range(n_lg):
            p = jnp.exp(a_scr[h, :, lanes(g)] - m_b)
            ls = ls + p
            p_scr[h, :, lanes(g)] = p.astype(BF16)
        row_sum.append(jnp.sum(ls, axis=-1, keepdims=True))
    for h, sl in enumerate(heads):
        o = _dot(p_scr[h], vb_ref[0, :, sl]) / row_sum[h]
        o_ref[0, :, sl] = o.astype(BF16)


def _dsa_class(qi, wit, qb, ki, kb, vb, tq, k_sel, q_base, q_len, sk):
    bsz = qb.shape[0]
    blk0 = q_base // tq
    row = lambda b, i: (b, blk0 + i, 0)
    keys = lambda b, i: (b, 0, 0)
    return pl.pallas_call(
        functools.partial(_dsa_kernel, tq=tq, k_sel=k_sel, q_base=q_base),
        out_shape=jax.ShapeDtypeStruct((bsz, q_len, WIDTH), BF16),
        grid=(bsz, q_len // tq),
        in_specs=[
            pl.BlockSpec((1, tq, N_IDX_HEADS * IDX_DIM), row),
            pl.BlockSpec((1, N_IDX_HEADS, tq), lambda b, i: (b, 0, blk0 + i)),
            pl.BlockSpec((1, tq, WIDTH), row),
            pl.BlockSpec((1, sk, IDX_DIM), keys),
            pl.BlockSpec((1, sk, WIDTH), keys),
            pl.BlockSpec((1, sk, WIDTH), keys),
        ],
        out_specs=pl.BlockSpec((1, tq, WIDTH), lambda b, i: (b, i, 0)),
        scratch_shapes=[pltpu.VMEM((sk, tq), F32), pltpu.VMEM((tq, sk), F32),
                        pltpu.VMEM((N_HEADS, tq, sk), F32), pltpu.VMEM((N_HEADS, tq, sk), BF16)],
        compiler_params=_cparams(("arbitrary", "arbitrary")),
        name=f"dsa_k{sk}",
    )(qi, wit, qb, ki, kb, vb)


def _dsa(qi, wit, qb, ki, kb, vb, tq):
    s = qb.shape[1]
    k_sel = min(TOPK_MAX, s // 4)
    n_cls = max(1, min(DSA_KEY_CLASSES, s // KEY_TILE))
    q_len = s // n_cls
    outs = [_dsa_class(qi, wit, qb, ki, kb, vb, tq, k_sel, c * q_len, q_len, (c + 1) * q_len)
            for c in range(n_cls)]
    return jnp.concatenate(outs, axis=1)


def _merge_kernel(oa_ref, ob_ref, ga_ref, gb_ref, x_ref, mod_ref, wba_ref, wbb_ref, wo_ref,
                  ln1_ref, wr_ref, rb_ref, x1_ref, u2_ref, gk_ref, u2t_ref, expk_ref, rankk_ref, seen_ref,
                  seen_scr, *, alpha):
    tm = x_ref.shape[1]
    ya = _dot(oa_ref[0], wba_ref[...])
    yb = _dot(ob_ref[0], wbb_ref[...])
    merged = _sigmoid(ga_ref[0].astype(F32)) * ya + _sigmoid(gb_ref[0].astype(F32)) * yb
    mix = _dot(merged.astype(BF16), wo_ref[...])
    mod = mod_ref[0]
    g1 = mod[2:3, :]
    sh2 = mod[3:4, :]
    sc2 = mod[4:5, :]
    ln1 = ln1_ref[...]
    x1 = _ln(alpha * x_ref[0] + g1 * mix) * ln1[0:1, :] + ln1[1:2, :]
    x1_ref[0] = x1
    u2 = _ln(x1) * (1.0 + sc2) + sh2
    u2_ref[0] = u2.astype(BF16)

    u_hi, u_lo = _split_bf16(u2)
    both = _dot(u_hi, wr_ref[...])
    logits = both[:, 0:LANES] + both[:, LANES:] + _dot(u_lo, wr_ref[:, 0:LANES])
    aff = _sigmoid(logits.T[0:N_EXPERTS, :])
    biased = aff + rb_ref[...]
    grp = biased.reshape(N_GROUPS, GROUP_SIZE, tm)
    sub = lax.broadcasted_iota(jnp.int32, grp.shape, 1)
    m1 = jnp.max(grp, axis=1, keepdims=True)
    first = jnp.min(jnp.where(grp == m1, sub, GROUP_SIZE), axis=1, keepdims=True)
    m2 = jnp.max(jnp.where(sub == first, NEG_INF, grp), axis=1, keepdims=True)
    gscore = (m1 + m2).reshape(N_GROUPS, tm)
    g_i = lax.broadcasted_iota(jnp.int32, (N_GROUPS, tm), 0)
    g_rank = jnp.zeros((N_GROUPS, tm), F32)
    for g in range(N_GROUPS):
        other = gscore[g:g + 1, :]
        beats = (other > gscore) | ((other == gscore) & (g < g_i))
        g_rank = g_rank + jnp.where(beats, 1.0, 0.0)
    g_keep = jnp.where(g_rank < TOPK_GROUPS, 1.0, 0.0).reshape(N_GROUPS, 1, tm)
    e_keep = jnp.broadcast_to(g_keep, (N_GROUPS, GROUP_SIZE, tm)).reshape(N_EXPERTS, tm)
    masked = jnp.where(e_keep > 0.5, biased, NEG_INF)
    e_i = lax.broadcasted_iota(jnp.int32, (N_EXPERTS, tm), 0)
    chosen = jnp.zeros((N_EXPERTS, tm), F32)
    picks = []
    for _ in range(TOPK_EXPERTS):
        best = jnp.max(masked, axis=0, keepdims=True)
        first = jnp.min(jnp.where(masked == best, e_i, N_EXPERTS), axis=0, keepdims=True)
        hit = e_i == first
        chosen = jnp.where(hit, 1.0, chosen)
        masked = jnp.where(hit, NEG_INF, masked)
        picks.append(first)
    top_aff = jnp.where(chosen > 0.5, aff, 0.0)
    comb_t = top_aff / jnp.sum(top_aff, axis=0, keepdims=True) * ROUTED_SCALE

    @pl.when((pl.program_id(0) == 0) & (pl.program_id(1) == 0))
    def _():
        seen_scr[...] = jnp.zeros_like(seen_scr)

    r_i = lax.broadcasted_iota(jnp.int32, (tm, tm), 0)
    c_i = lax.broadcasted_iota(jnp.int32, (tm, tm), 1)
    earlier = jnp.where(r_i < c_i, 1.0, 0.0).astype(BF16)
    arrival = _dot(chosen.astype(BF16), earlier) + seen_scr[:, 0:1]
    seen_scr[...] = seen_scr[...] + jnp.sum(chosen, axis=1, keepdims=True)
    seen_ref[...] = seen_scr[...]
    ranks, gates = [], []
    for first in picks:
        hit = e_i == first
        ranks.append(jnp.sum(jnp.where(hit, arrival, 0.0), axis=0, keepdims=True))
        gates.append(jnp.sum(jnp.where(hit, comb_t, 0.0), axis=0, keepdims=True))
    expk_ref[...] = jnp.concatenate(picks, axis=0)
    rankk_ref[...] = jnp.concatenate(ranks, axis=0).astype(jnp.int32)
    gate_rows = jnp.concatenate(gates + [jnp.zeros((LANES - TOPK_EXPERTS, tm), F32)], axis=0)
    gk_ref[0] = gate_rows.T
    for c in range(TILE_ROWS):
        u2t_ref[pl.ds(c, tm, stride=TILE_ROWS), :] = u2[:, c * LANES:(c + 1) * LANES]


def _merge(oa, ob, ga, gb, x, mod, wba, wbb, wo, ln1, wr, rb, tm, alpha):
    bsz, s, d = x.shape
    row = lambda b, i: (b, i, 0)
    per_batch = s // tm
    flat = lambda b, i: (b * per_batch + i, 0)
    cols = lambda b, i: (0, b * per_batch + i)
    w2 = lambda shape: pl.BlockSpec(shape, lambda b, i: (0,) * len(shape))
    return pl.pallas_call(
        functools.partial(_merge_kernel, alpha=alpha),
        out_shape=[jax.ShapeDtypeStruct((bsz, s, d), F32),
                   jax.ShapeDtypeStruct((bsz, s, d), BF16),
                   jax.ShapeDtypeStruct((bsz, s, LANES), F32),
                   jax.ShapeDtypeStruct((bsz * s * TILE_ROWS, LANES), F32),
                   jax.ShapeDtypeStruct((TOPK_EXPERTS, bsz * s), jnp.int32),
                   jax.ShapeDtypeStruct((TOPK_EXPERTS, bsz * s), jnp.int32),
                   jax.ShapeDtypeStruct((N_EXPERTS, LANES), F32)],
        grid=(bsz, s // tm),
        in_specs=[
            pl.BlockSpec((1, tm, WIDTH), row), pl.BlockSpec((1, tm, WIDTH), row),
            pl.BlockSpec((1, tm, d), row), pl.BlockSpec((1, tm, d), row),
            pl.BlockSpec((1, tm, d), row),
            pl.BlockSpec((1, N_MOD, d), lambda b, i: (b, 0, 0)),
            w2(wba.shape), w2(wbb.shape), w2(wo.shape), w2(ln1.shape), w2(wr.shape), w2(rb.shape),
        ],
        out_specs=[pl.BlockSpec((1, tm, d), row), pl.BlockSpec((1, tm, d), row),
                   pl.BlockSpec((1, tm, LANES), row),
                   pl.BlockSpec((tm * TILE_ROWS, LANES), flat),
                   pl.BlockSpec((TOPK_EXPERTS, tm), cols),
                   pl.BlockSpec((TOPK_EXPERTS, tm), cols),
                   pl.BlockSpec((N_EXPERTS, LANES), lambda b, i: (0, 0))],
        scratch_shapes=[pltpu.VMEM((N_EXPERTS, LANES), F32)],
        compiler_params=_cparams(("arbitrary", "arbitrary")),
        name="merge",
    )(oa, ob, ga, gb, x, mod, wba, wbb, wo, ln1, wr, rb)


def _silu(z):
    return z * _sigmoid(z)


def _sc_params():
    return pltpu.CompilerParams(use_tc_tiling_on_sc=False)


def _sc_mesh():
    return plsc.VectorSubcoreMesh(core_axis_name="c", subcore_axis_name="s",
                                  num_cores=SC_CORES, num_subcores=SC_SUBCORES)


def _sc_dispatch(x_tiles, dest, n_rows):
    n_tok = x_tiles.shape[0]
    n_k = dest.shape[0]
    mesh = _sc_mesh()

    @pl.kernel(out_type=jax.ShapeDtypeStruct((n_rows,) + x_tiles.shape[1:], x_tiles.dtype), mesh=mesh,
               scratch_types=[], compiler_params=_sc_params())
    def scatter_rows(x_hbm, *rest):
        idx_hbm, o_hbm = rest[:n_k], rest[n_k]

        def body(x_vmem, *idx_vmem):
            for k in range(n_k):
                pltpu.sync_copy(x_vmem, o_hbm.at[idx_vmem[k].at[0]])

        pltpu.emit_pipeline(
            body, grid=(n_tok // SC_WINDOW,),
            in_specs=[pl.BlockSpec((SC_WINDOW,) + x_tiles.shape[1:], lambda i: (i, 0, 0))]
            + [pl.BlockSpec((1, SC_WINDOW), lambda i: (0, i))] * n_k,
            out_specs=[], core_axis_name=("c", "s"), dimension_semantics=(pltpu.PARALLEL,),
        )(x_hbm, *idx_hbm)

    return scatter_rows(x_tiles, *[dest[k:k + 1] for k in range(n_k)])


def _sc_gather(rows, idx):
    n = idx.shape[1]
    mesh = _sc_mesh()

    @pl.kernel(out_type=jax.ShapeDtypeStruct((n,) + rows.shape[1:], rows.dtype), mesh=mesh,
               scratch_types=[], compiler_params=_sc_params())
    def gather_rows(r_hbm, i_hbm, o_hbm):
        def body(i_vmem, o_vmem):
            pltpu.sync_copy(r_hbm.at[i_vmem.at[0]], o_vmem)

        pltpu.emit_pipeline(
            body, grid=(n // SC_WINDOW,),
            in_specs=[pl.BlockSpec((1, SC_WINDOW), lambda i: (0, i))],
            out_specs=[pl.BlockSpec((SC_WINDOW,) + rows.shape[1:], lambda i: (i, 0, 0))],
            core_axis_name=("c", "s"), dimension_semantics=(pltpu.PARALLEL,),
        )(i_hbm, o_hbm)

    return gather_rows(rows, idx)


def _tile_rows(ref, n_rows, lead=()):
    return jnp.concatenate([ref[lead + (pl.ds(c, n_rows, stride=TILE_ROWS), slice(None))]
                            for c in range(TILE_ROWS)], axis=1)


def _ffn_grouped_kernel(expert_ref, valid_ref, used_ref, x_ref, wg_ref, wu_ref, wd_ref, y_ref):
    i = pl.program_id(0)
    n_rows = x_ref.shape[0] // TILE_ROWS

    @pl.when(i < used_ref[0])
    def _():
        row = lax.broadcasted_iota(jnp.int32, (n_rows, 1), 0)
        x = jnp.where(row < valid_ref[i], _tile_rows(x_ref, n_rows), 0.0).astype(BF16)
        h = _silu(_dot(x, wg_ref[0])) * _dot(x, wu_ref[0])
        y = _dot(h.astype(BF16), wd_ref[0])
        for c in range(TILE_ROWS):
            y_ref[pl.ds(c, n_rows, stride=TILE_ROWS), :] = y[:, c * LANES:(c + 1) * LANES]


def _ffn_grouped(tile_expert, tile_valid, tiles_used, xs, wg, wu, wd, n_tiles):
    _, d, d_exp = wg.shape
    blk = pl.BlockSpec((MOE_ROW_TILE * TILE_ROWS, LANES), lambda i, te, tv, nu: (i, 0))
    weight = lambda shape: pl.BlockSpec(shape, lambda i, te, tv, nu: (te[i], 0, 0))
    return pl.pallas_call(
        _ffn_grouped_kernel,
        out_shape=jax.ShapeDtypeStruct(xs.shape, F32),
        grid_spec=pltpu.PrefetchScalarGridSpec(
            num_scalar_prefetch=3, grid=(n_tiles,),
            in_specs=[blk, weight((1, d, d_exp)), weight((1, d, d_exp)), weight((1, d_exp, d))],
            out_specs=blk),
        compiler_params=_cparams(("arbitrary",)),
        name="moe_ffn",
    )(tile_expert, tile_valid, tiles_used, xs, wg, wu, wd)


def _moe_final_kernel(yg_ref, gk_ref, u2_ref, x1_ref, mod_ref, wsg_ref, wsu_ref, wsd_ref, ln2_ref, o_ref,
                      *, alpha):
    tm = x1_ref.shape[0]
    t = u2_ref[...]
    ffn = _dot((_silu(_dot(t, wsg_ref[...])) * _dot(t, wsu_ref[...])).astype(BF16), wsd_ref[...])
    gk = gk_ref[...]
    for k in range(TOPK_EXPERTS):
        ffn = ffn + _tile_rows(yg_ref, tm, (k,)) * gk[:, k:k + 1]
    mod = mod_ref[0]
    g2 = mod[5:6, :]
    ln2 = ln2_ref[...]
    o_ref[...] = _ln(alpha * x1_ref[...] + g2 * ffn) * ln2[0:1, :] + ln2[1:2, :]


def _moe_final(yg, gk, u2, x1, mod, wsg, wsu, wsd, ln2, tm, tokens_per_batch, tok0, alpha):
    t, d = u2.shape
    n_tok = yg.shape[1] // TILE_ROWS
    blocks_per_batch = tokens_per_batch // tm
    blk0 = tok0 // tm
    row = lambda i: (blk0 + i, 0)
    w2 = lambda shape: pl.BlockSpec(shape, lambda i: (0,) * len(shape))
    return pl.pallas_call(
        functools.partial(_moe_final_kernel, alpha=alpha),
        out_shape=jax.ShapeDtypeStruct((t, d), F32),
        grid=(n_tok // tm,),
        in_specs=[
            pl.BlockSpec((TOPK_EXPERTS, tm * TILE_ROWS, LANES), lambda i: (0, i, 0)),
            pl.BlockSpec((tm, LANES), row),
            pl.BlockSpec((tm, d), row),
            pl.BlockSpec((tm, d), row),
            pl.BlockSpec((1, N_MOD, d), lambda i: ((blk0 + i) // blocks_per_batch, 0, 0)),
            w2(wsg.shape), w2(wsu.shape), w2(wsd.shape), w2(ln2.shape),
        ],
        out_specs=pl.BlockSpec((tm, d), row),
        input_output_aliases={3: 0},
        compiler_params=_cparams(("arbitrary",)),
        name="moe_final",
    )(yg, gk, u2, x1, mod, wsg, wsu, wsd, ln2)


def _sorted_layout(expk, rankk, counts, n_tiles):
    e_ids = jnp.arange(N_EXPERTS, dtype=jnp.int32)[:, None, None]
    is_e = expk[None] == e_ids
    tiles = (counts + MOE_ROW_TILE - 1) // MOE_ROW_TILE
    tile_end = jnp.cumsum(tiles)
    first_row = (tile_end - tiles) * MOE_ROW_TILE
    dest = jnp.sum(jnp.where(is_e, first_row[:, None, None], 0), axis=0, dtype=jnp.int32) + rankk
    tile_ids = jnp.arange(n_tiles, dtype=jnp.int32)[:, None]
    done = tile_ids >= tile_end[None, :]
    owner = ~done & (tile_ids >= (tile_end - tiles)[None, :])
    tile_expert = jnp.minimum(jnp.sum(done, axis=1, dtype=jnp.int32), N_EXPERTS - 1)
    rows_before = (tile_ids[:, 0] - jnp.sum(jnp.where(done, tiles[None, :], 0), axis=1)) * MOE_ROW_TILE
    owner_count = jnp.sum(jnp.where(owner, counts[None, :], 0), axis=1)
    tile_valid = jnp.clip(owner_count - rows_before, 0, MOE_ROW_TILE)
    return dest, tile_expert, tile_valid, tile_end[-1:]


def _rope_tables(positions):
    inv_freq = ROPE_THETA ** (-jnp.arange(0, ROPE_DIM, 2, dtype=F32) / ROPE_DIM)
    ang = positions.astype(F32)[..., None] * inv_freq
    cos, sin = jnp.cos(ang), jnp.sin(ang)
    ones = jnp.ones(cos.shape[:-1] + (HEAD_DIM - ROPE_DIM,), F32)
    zeros = jnp.zeros_like(ones)
    zh = jnp.zeros_like(sin)
    c = jnp.concatenate([cos, cos, ones], axis=-1)
    s1 = jnp.concatenate([-sin, zh, zeros], axis=-1)
    s2 = jnp.concatenate([zh, sin, zeros], axis=-1)
    reps = LANES // HEAD_DIM
    return jnp.concatenate([jnp.tile(c, reps), jnp.tile(s1, reps), jnp.tile(s2, reps)], axis=-1)


def _band_bias_table(rel_bias, tq):
    n_heads = rel_bias.shape[0]
    win = tq + BAND_PAD
    row_len = win + tq
    n_high = BAND_PAD - MAX_REL + 1
    n_ramp = 2 * MAX_REL - 1
    high = rel_bias[:, 2 * MAX_REL:]
    profile = jnp.concatenate([
        jnp.broadcast_to(high, (n_heads, n_high)),
        rel_bias[:, n_ramp:0:-1],
        jnp.broadcast_to(rel_bias[:, :1], (n_heads, win - n_high - n_ramp)),
        jnp.broadcast_to(high, (n_heads, tq)),
    ], axis=1)
    skew = jnp.broadcast_to(profile[:, None, :], (n_heads, tq, row_len)).reshape(n_heads, tq * row_len)
    skew = skew[:, :tq * (row_len - 1)].reshape(n_heads, tq, row_len - 1)[:, :, :win]
    qi = np.arange(tq)[:, None]
    kj = np.arange(win)[None, :]
    q_chunk = qi // CHUNK
    k_chunk = kj // CHUNK - (BAND_CHUNKS - 1)
    in_band = (k_chunk <= q_chunk) & (k_chunk >= q_chunk - (BAND_CHUNKS - 1))
    return jnp.where(jnp.asarray(in_band)[None], skew, NEG_INF)


def kernel(x, c, positions, w_ada, b_ada, w_in, rel_bias, idx_k_norm_g, idx_k_norm_b, w_branch_a,
           w_branch_b, w_out, ln1_g, ln1_b, w_router, router_bias, w_exp_gate, w_exp_up, w_exp_down,
           w_sh_gate, w_sh_up, w_sh_down, ln2_g, ln2_b):
    bsz, s, d = x.shape
    depth = w_ada.shape[0]
    alpha = (2.0 * depth) ** 0.25
    tm_proj = min(512, s)
    tq_band = min(256, s)
    tq_dsa = min(256, s)
    tm_merge = min(512, s)
    tm_moe = min(256, s)

    rope_tab = _rope_tables(positions)
    n7 = 7 * WIDTH
    for l in range(depth):
        mod = _ada(c, w_ada[l], b_ada[l]).reshape(bsz, N_MOD, d)
        w_l = w_in[l]
        w7 = w_l[:, :n7].astype(BF16)
        n_kw = IDX_DIM + N_IDX_HEADS
        wkw = jnp.pad(w_l[:, n7:n7 + n_kw], ((0, 0), (0, LANES - n_kw))).astype(BF16)
        wg = w_l[:, n7 + n_kw:].astype(BF16)
        lnk = jnp.pad(jnp.stack([idx_k_norm_g[l], idx_k_norm_b[l]]), ((0, 0), (0, LANES - IDX_DIM)))
        qa, ka, va, qb, kb, vb, qi, ki, wi, ga, gb = _proj(x, mod, rope_tab, w7, wkw, wg, lnk, tm_proj)

        oa = _band(qa, ka, va, _band_bias_table(rel_bias[l], tq_band), tq_band)
        ob = _dsa(qi, wi, qb, ki, kb, vb, tq_dsa)

        wr = jnp.pad(w_router[l], ((0, 0), (0, LANES - N_EXPERTS)))
        wr_hi = wr.astype(BF16)
        wr_lo = (wr - wr_hi.astype(F32)).astype(BF16)
        x1, u2, gk, u2_tiles, expk, rankk, seen = _merge(
            oa, ob, ga, gb, x, mod,
            w_branch_a[l].astype(BF16), w_branch_b[l].astype(BF16), w_out[l].astype(BF16),
            jnp.stack([ln1_g[l], ln1_b[l]]), jnp.concatenate([wr_hi, wr_lo], axis=1),
            router_bias[l].reshape(N_EXPERTS, 1), tm_merge, alpha)

        t = bsz * s
        n_tiles = t * TOPK_EXPERTS // MOE_ROW_TILE + N_EXPERTS
        n_rows = n_tiles * MOE_ROW_TILE
        dest, tile_expert, tile_valid, tiles_used = _sorted_layout(
            expk, rankk, seen[:, 0].astype(jnp.int32), n_tiles)
        xs = _sc_dispatch(u2_tiles.reshape(t, TILE_ROWS, LANES), dest, n_rows)
        ys = _ffn_grouped(tile_expert, tile_valid, tiles_used, xs.reshape(n_rows * TILE_ROWS, LANES),
                          w_exp_gate[l].astype(BF16), w_exp_up[l].astype(BF16), w_exp_down[l].astype(BF16),
                          n_tiles)
        ys = ys.reshape(n_rows, TILE_ROWS, LANES)
        shared = (w_sh_gate[l].astype(BF16), w_sh_up[l].astype(BF16), w_sh_down[l].astype(BF16))
        ln2 = jnp.stack([ln2_g[l], ln2_b[l]])
        chunk = t // MOE_COMBINE_CHUNKS
        out = x1.reshape(t, d)
        for c in range(MOE_COMBINE_CHUNKS):
            idx = dest[:, c * chunk:(c + 1) * chunk].reshape(1, TOPK_EXPERTS * chunk)
            yg = _sc_gather(ys, idx).reshape(TOPK_EXPERTS, chunk * TILE_ROWS, LANES)
            out = _moe_final(yg, gk.reshape(t, LANES), u2.reshape(t, d), out, mod,
                             *shared, ln2, tm_moe, s, c * chunk, alpha)
        x = out.reshape(bsz, s, d)
    return x
```

```python
import functools

import jax
import jax.numpy as jnp
import numpy as np
from jax import lax
from jax.experimental import pallas as pl
from jax.experimental.pallas import tpu as pltpu
from jax.experimental.pallas import tpu_sc as plsc

F32 = jnp.float32
BF16 = jnp.bfloat16
NEG_INF = float("-inf")

CHUNK = 64
HEAD_DIM = 64
N_HEADS = 8
WIDTH = N_HEADS * HEAD_DIM
BAND_CHUNKS = 9
BAND_PAD = (BAND_CHUNKS - 1) * CHUNK
MAX_REL = 128
ROPE_THETA = 500000.0
ROPE_DIM = HEAD_DIM // 4
ROPE_HALF = ROPE_DIM // 2
N_IDX_HEADS = 8
IDX_DIM = 64
TOPK_MAX = 256
N_EXPERTS = 64
N_GROUPS = 8
GROUP_SIZE = N_EXPERTS // N_GROUPS
TOPK_GROUPS = 4
TOPK_EXPERTS = 8
ROUTED_SCALE = 1.0
N_MOD = 6
LN_EPS = 1e-5
LANES = 128
KEY_TILE = 256

VMEM_LIMIT = 56 * 1024 * 1024

DSA_KEY_CLASSES = 8
TILE_ROWS = 4
MOE_ROW_TILE = 512
SC_WINDOW = 64
MOE_COMBINE_CHUNKS = 4
SC_CORES = 2
SC_SUBCORES = 16


def _cparams(sem):
    return pltpu.CompilerParams(dimension_semantics=sem, vmem_limit_bytes=VMEM_LIMIT)


def _ln(z):
    mu = jnp.mean(z, axis=-1, keepdims=True)
    zc = z - mu
    var = jnp.mean(zc * zc, axis=-1, keepdims=True)
    return zc * lax.rsqrt(var + LN_EPS)


def _sigmoid(z):
    return 1.0 / (1.0 + jnp.exp(-z))


def _dot(a, b):
    return jnp.dot(a, b, preferred_element_type=F32)


def _dot_nt(a, b):
    return lax.dot_general(a, b, (((1,), (1,)), ((), ())), preferred_element_type=F32)


def _split_bf16(z):
    hi = z.astype(BF16)
    lo = (z - hi.astype(F32)).astype(BF16)
    return hi, lo


HIGH_HALF = 0xFFFF0000


def _pack_pairs(z):
    half = z.shape[1] // 2

    def bits(v):
        return lax.bitcast_convert_type(v.astype(BF16).astype(F32), jnp.uint32)

    return (bits(z[:, :half]) >> 16) | (bits(z[:, half:]) & jnp.uint32(HIGH_HALF))


def _unpack_pairs(p):
    low = lax.bitcast_convert_type(p << 16, F32)
    high = lax.bitcast_convert_type(p & jnp.uint32(HIGH_HALF), F32)
    return jnp.concatenate([low, high], axis=1)


def _store_token_rows(ref, words, n_tok):
    for q in range(TILE_ROWS):
        ref[pl.ds(q, n_tok, stride=TILE_ROWS), :] = words[:, q * LANES:(q + 1) * LANES]


def _load_token_rows(ref, n_tok, lead=()):
    return jnp.concatenate([ref[lead + (pl.ds(q, n_tok, stride=TILE_ROWS), slice(None))]
                            for q in range(TILE_ROWS)], axis=1)


def _ada_kernel(c_ref, w_ref, b_ref, o_ref):
    c = c_ref[...]
    ca = c * _sigmoid(c)
    ca_hi, ca_lo = _split_bf16(ca)
    w = w_ref[...]
    w_hi, w_lo = _split_bf16(w)
    acc = _dot(ca_hi, w_hi) + _dot(ca_lo, w_hi) + _dot(ca_hi, w_lo)
    o_ref[...] = acc + b_ref[...]


def _ada(c, w_ada, b_ada):
    bsz, d = c.shape
    n = w_ada.shape[1]
    tn = 1024
    return pl.pallas_call(
        _ada_kernel,
        out_shape=jax.ShapeDtypeStruct((bsz, n), F32),
        grid=(n // tn,),
        in_specs=[
            pl.BlockSpec((bsz, d), lambda j: (0, 0)),
            pl.BlockSpec((d, tn), lambda j: (0, j)),
            pl.BlockSpec((1, tn), lambda j: (0, j)),
        ],
        out_specs=pl.BlockSpec((bsz, tn), lambda j: (0, j)),
        compiler_params=_cparams(("arbitrary",)),
        name="ada",
    )(c, w_ada, b_ada.reshape(1, n))


def _rope(z, c_t, s1_t, s2_t):
    n = z.shape[-1]
    return z * c_t + pltpu.roll(z, n - ROPE_HALF, 1) * s1_t + pltpu.roll(z, ROPE_HALF, 1) * s2_t


def _proj_kernel(x_ref, mod_ref, rope_ref, w7_ref, wkw_ref, wg_ref, lnk_ref,
                 qa_ref, ka_ref, va_ref, qb_ref, kb_ref, vb_ref, qi_ref, ki_ref, wi_ref,
                 ga_ref, gb_ref):
    x = x_ref[0]
    mod = mod_ref[0]
    sh1 = mod[0:1, :]
    sc1 = mod[1:2, :]
    u = (_ln(x) * (1.0 + sc1) + sh1).astype(BF16)

    rope = rope_ref[0]
    c1 = rope[:, 0:LANES]
    s1 = rope[:, LANES:2 * LANES]
    s2 = rope[:, 2 * LANES:3 * LANES]
    reps = WIDTH // LANES
    c_t = jnp.concatenate([c1] * reps, axis=1)
    s1_t = jnp.concatenate([s1] * reps, axis=1)
    s2_t = jnp.concatenate([s2] * reps, axis=1)

    att_scale = HEAD_DIM ** -0.5
    idx_scale = IDX_DIM ** -0.5

    def seg(k):
        return _dot(u, w7_ref[:, k * WIDTH:(k + 1) * WIDTH])

    qa_ref[0] = (seg(0) * att_scale).astype(BF16)
    ka_ref[0] = seg(1).astype(BF16)
    va_ref[0] = seg(2).astype(BF16)
    qb_ref[0] = (_rope(seg(3), c_t, s1_t, s2_t) * att_scale).astype(BF16)
    kb_ref[0] = _rope(seg(4), c_t, s1_t, s2_t).astype(BF16)
    vb_ref[0] = seg(5).astype(BF16)
    qi_ref[0] = (_rope(seg(6), c_t, s1_t, s2_t) * idx_scale).astype(BF16)

    z = _dot(u, wkw_ref[...])
    lane = lax.broadcasted_iota(jnp.int32, z.shape, 1)
    is_k = lane < IDX_DIM
    mu = jnp.sum(jnp.where(is_k, z, 0.0), axis=-1, keepdims=True) * (1.0 / IDX_DIM)
    zc = jnp.where(is_k, z - mu, 0.0)
    var = jnp.sum(zc * zc, axis=-1, keepdims=True) * (1.0 / IDX_DIM)
    lnk = lnk_ref[...]
    y = zc * lax.rsqrt(var + LN_EPS) * lnk[0:1, :] + lnk[1:2, :]
    y = _rope(y, c1, jnp.where(is_k, s1, 0.0), jnp.where(is_k, s2, 0.0))
    ki_ref[0] = y[:, 0:IDX_DIM].astype(BF16)
    wi_ref[0] = z.T[IDX_DIM:IDX_DIM + N_IDX_HEADS, :] * (N_IDX_HEADS ** -0.5)

    d = ga_ref.shape[-1]
    ga_ref[0] = _dot(u, wg_ref[:, 0:d]).astype(BF16)
    gb_ref[0] = _dot(u, wg_ref[:, d:2 * d]).astype(BF16)


def _proj(x, mod, rope_tab, w7, wkw, wg, lnk, tm):
    bsz, s, d = x.shape
    const = dict(pipeline_mode=pl.Buffered(1))
    row = lambda b, i: (b, i, 0)
    wspec = lambda shape: pl.BlockSpec(shape, lambda b, i: (0, 0), **const)
    out_w = jax.ShapeDtypeStruct((bsz, s, WIDTH), BF16)
    out_d = jax.ShapeDtypeStruct((bsz, s, d), BF16)
    return pl.pallas_call(
        _proj_kernel,
        out_shape=[out_w] * 7 + [
            jax.ShapeDtypeStruct((bsz, s, IDX_DIM), BF16),
            jax.ShapeDtypeStruct((bsz, N_IDX_HEADS, s), F32),
            out_d, out_d],
        grid=(bsz, s // tm),
        in_specs=[
            pl.BlockSpec((1, tm, d), row),
            pl.BlockSpec((1, N_MOD, d), lambda b, i: (b, 0, 0)),
            pl.BlockSpec((1, tm, 3 * LANES), row),
            wspec(w7.shape), wspec(wkw.shape), wspec(wg.shape), wspec(lnk.shape),
        ],
        out_specs=[pl.BlockSpec((1, tm, WIDTH), row)] * 7 + [
            pl.BlockSpec((1, tm, IDX_DIM), row),
            pl.BlockSpec((1, N_IDX_HEADS, tm), lambda b, i: (b, 0, i)),
            pl.BlockSpec((1, tm, d), row), pl.BlockSpec((1, tm, d), row)],
        compiler_params=_cparams(("arbitrary", "arbitrary")),
        name="proj",
    )(x, mod, rope_tab, w7, wkw, wg, lnk)


def _band_kernel(q_ref, k_ref, v_ref, bias_ref, o_ref, kpad, vpad, a_scr, p_scr, *, tq):
    i = pl.program_id(1)
    s = k_ref.shape[1]
    win = tq + BAND_PAD

    @pl.when(i == 0)
    def _():
        zeros = jnp.zeros((BAND_PAD, WIDTH), BF16)
        kpad[0:BAND_PAD, :] = zeros
        vpad[0:BAND_PAD, :] = zeros
        kpad[BAND_PAD:BAND_PAD + s, :] = k_ref[0]
        vpad[BAND_PAD:BAND_PAD + s, :] = v_ref[0]

    q0 = pl.multiple_of(i * tq, tq)
    n_lg = win // LANES
    col = lax.broadcasted_iota(jnp.int32, (tq, LANES), 1)
    first_frame = BAND_PAD - q0
    heads = [slice(h * HEAD_DIM, (h + 1) * HEAD_DIM) for h in range(N_HEADS)]

    def lanes(g):
        return slice(g * LANES, (g + 1) * LANES)

    row_max = []
    for h, sl in enumerate(heads):
        sc = _dot_nt(q_ref[0, :, sl], kpad[pl.ds(q0, win), sl])
        mx = jnp.full((tq, LANES), NEG_INF, F32)
        for g in range(n_lg):
            a = jnp.where(col >= first_frame - g * LANES, sc[:, lanes(g)] + bias_ref[h, :, lanes(g)], NEG_INF)
            a_scr[h, :, lanes(g)] = a
            mx = jnp.maximum(mx, a)
        row_max.append(jnp.max(mx, axis=-1, keepdims=True))
    row_sum = []
    for h in range(N_HEADS):
        m_b = jnp.broadcast_to(row_max[h], (tq, LANES))
        ls = jnp.zeros((tq, LANES), F32)
        for g in range(n_lg):
            p = jnp.exp(a_scr[h, :, lanes(g)] - m_b)
            ls = ls + p
            p_scr[h, :, lanes(g)] = p.astype(BF16)
        row_sum.append(jnp.sum(ls, axis=-1, keepdims=True))
    for h, sl in enumerate(heads):
        o = _dot(p_scr[h], vpad[pl.ds(q0, win), sl]) / row_sum[h]
        o_ref[0, :, sl] = o.astype(BF16)


def _band(qa, ka, va, bias_tab, tq):
    bsz, s, _ = qa.shape
    win = tq + BAND_PAD
    return pl.pallas_call(
        functools.partial(_band_kernel, tq=tq),
        out_shape=jax.ShapeDtypeStruct((bsz, s, WIDTH), BF16),
        grid=(bsz, s // tq),
        in_specs=[
            pl.BlockSpec((1, tq, WIDTH), lambda b, i: (b, i, 0)),
            pl.BlockSpec((1, s, WIDTH), lambda b, i: (b, 0, 0)),
            pl.BlockSpec((1, s, WIDTH), lambda b, i: (b, 0, 0)),
            pl.BlockSpec((N_HEADS, tq, win), lambda b, i: (0, 0, 0)),
        ],
        out_specs=pl.BlockSpec((1, tq, WIDTH), lambda b, i: (b, i, 0)),
        scratch_shapes=[pltpu.VMEM((BAND_PAD + s, WIDTH), BF16),
                        pltpu.VMEM((BAND_PAD + s, WIDTH), BF16),
                        pltpu.VMEM((N_HEADS, tq, win), F32),
                        pltpu.VMEM((N_HEADS, tq, win), BF16)],
        compiler_params=_cparams(("arbitrary", "arbitrary")),
        name="band",
    )(qa, ka, va, bias_tab)


BISECT_STEPS_PER_CHECK = 4
BISECT_MAX_CHECKS = 400
COUNT_ROWS = 64


def _dsa_kernel(qi_ref, wit_ref, qb_ref, ki_ref, kb_ref, vb_ref, o_ref, sct_scr, msk_scr, a_scr, p_scr,
                *, tq, k_sel, q_base):
    i = pl.program_id(1)
    sk = ki_ref.shape[1]
    n_kt = sk // KEY_TILE
    n_lg = sk // LANES
    q0 = q_base + i * tq
    kf = float(k_sel)

    def lanes(g):
        return slice(g * LANES, (g + 1) * LANES)

    def slab_reduce(fn, combine, init):
        acc = jnp.full((COUNT_ROWS, tq), init, F32)
        for r in range(sk // COUNT_ROWS):
            acc = combine(acc, fn(sct_scr[r * COUNT_ROWS:(r + 1) * COUNT_ROWS, :]))
        return acc

    def count(pred):
        part = slab_reduce(lambda t: jnp.where(pred(t), 1.0, 0.0), jnp.add, 0.0)
        return jnp.sum(part, axis=0, keepdims=True)

    w_t = wit_ref[0]
    t_pos = q0 + lax.broadcasted_iota(jnp.int32, (1, tq), 1)
    limit = (t_pos // CHUNK + 1) * CHUNK
    key_in_tile = lax.broadcasted_iota(jnp.int32, (KEY_TILE, tq), 0)
    for kt in range(n_kt):
        ks = slice(kt * KEY_TILE, (kt + 1) * KEY_TILE)
        ki_t = ki_ref[0, ks, :]
        acc = jnp.zeros((KEY_TILE, tq), F32)
        for h in range(N_IDX_HEADS):
            lg = _dot_nt(ki_t, qi_ref[0, :, h * IDX_DIM:(h + 1) * IDX_DIM])
            acc = acc + jnp.maximum(lg, 0.0) * w_t[h:h + 1, :]
        sct_scr[ks, :] = jnp.where(key_in_tile < limit - kt * KEY_TILE, acc, NEG_INF)

    smax = jnp.max(slab_reduce(lambda t: t, jnp.maximum, NEG_INF), axis=0, keepdims=True)
    smin = jnp.min(slab_reduce(lambda t: jnp.where(t == NEG_INF, jnp.inf, t), jnp.minimum, jnp.inf),
                   axis=0, keepdims=True)
    n_adm = limit.astype(F32)
    c_max = count(lambda t: t >= smax)
    c_pos = count(lambda t: t > 0.0)
    c_nn = count(lambda t: t >= 0.0)
    zero = jnp.zeros_like(smax)
    at_zero = (c_pos < kf) & (c_nn >= kf)
    below_zero = c_nn < kf
    lo = jnp.where(at_zero | ~below_zero, zero, smin)
    clo = jnp.where(at_zero | ~below_zero, c_nn, n_adm)
    hi = jnp.where(at_zero | below_zero, zero, smax)
    few = n_adm <= kf
    lo = jnp.where(few, smin, lo)
    clo = jnp.where(few, n_adm, clo)
    top_tied = (c_max >= kf) & ~few
    lo = jnp.where(top_tied, smax, lo)
    clo = jnp.where(top_tied, c_max, clo)
    hi = jnp.where(top_tied, smax, hi)
    done0 = jnp.where(few | at_zero | top_tied, 1.0, 0.0)

    def bisect(carry):
        lo, hi, clo, done, it = carry
        for _ in range(BISECT_STEPS_PER_CHECK):
            mid = 0.5 * lo + 0.5 * hi
            stuck = (mid <= lo) | (mid >= hi)
            c = count(lambda t: t >= mid)
            ge = c >= kf
            lo = jnp.where(ge, mid, lo)
            clo = jnp.where(ge, c, clo)
            hi = jnp.where(ge, hi, mid)
            done = jnp.where(stuck | (clo <= kf), 1.0, done)
        return lo, hi, clo, done, it + 1

    def not_converged(carry):
        _, _, _, done, it = carry
        return (jnp.min(done) < 0.5) & (it < BISECT_MAX_CHECKS)

    lo, hi, clo, _, _ = lax.while_loop(not_converged, bisect, (lo, hi, clo, done0, jnp.int32(0)))

    thr = lo
    c_gt = count(lambda t: t > thr)
    c_eq = count(lambda t: t == thr)
    need = kf - c_gt
    tie_cut = jnp.max(c_gt + c_eq - kf) > 0.0

    def with_ties():
        r_i = lax.broadcasted_iota(jnp.int32, (KEY_TILE, KEY_TILE), 0)
        c_i = lax.broadcasted_iota(jnp.int32, (KEY_TILE, KEY_TILE), 1)
        earlier = jnp.where(c_i < r_i, 1.0, 0.0).astype(BF16)
        carry = jnp.zeros((1, tq), F32)
        for kt in range(n_kt):
            ks = slice(kt * KEY_TILE, (kt + 1) * KEY_TILE)
            sc = sct_scr[ks, :]
            eq = sc == thr
            eq_f = jnp.where(eq, 1.0, 0.0)
            rank = _dot(earlier, eq_f.astype(BF16)) + carry
            keep = (sc > thr) | (eq & (rank < need))
            msk_scr[:, ks] = jnp.where(keep, 0.0, NEG_INF).T
            carry = carry + jnp.sum(eq_f, axis=0, keepdims=True)

    def without_ties():
        for kt in range(n_kt):
            ks = slice(kt * KEY_TILE, (kt + 1) * KEY_TILE)
            msk_scr[:, ks] = jnp.where(sct_scr[ks, :] >= thr, 0.0, NEG_INF).T

    lax.cond(tie_cut, with_ties, without_ties)

    heads = [slice(h * HEAD_DIM, (h + 1) * HEAD_DIM) for h in range(N_HEADS)]
    row_max = []
    for h, sl in enumerate(heads):
        qh = qb_ref[0, :, sl]
        mx = jnp.full((tq, LANES), NEG_INF, F32)
        for kt in range(n_kt):
            ks = slice(kt * KEY_TILE, (kt + 1) * KEY_TILE)
            a = _dot_nt(qh, kb_ref[0, ks, sl]) + msk_scr[:, ks]
            a_scr[h, :, ks] = a
            for g in range(KEY_TILE // LANES):
                mx = jnp.maximum(mx, a[:, lanes(g)])
        row_max.append(jnp.max(mx, axis=-1, keepdims=True))
    row_sum = []
    for h in range(N_HEADS):
        m_b = jnp.broadcast_to(row_max[h], (tq, LANES))
        ls = jnp.zeros((tq, LANES), F32)
        for g in range(n_lg):
            p = jnp.exp(a_scr[h, :, lanes(g)] - m_b)
            ls = ls + p
            p_scr[h, :, lanes(g)] = p.astype(BF16)
        row_sum.append(jnp.sum(ls, axis=-1, keepdims=True))
    for h, sl in enumerate(heads):
        o = _dot(p_scr[h], vb_ref[0, :, sl]) / row_sum[h]
        o_ref[0, :, sl] = o.astype(BF16)


def _dsa_class(qi, wit, qb, ki, kb, vb, tq, k_sel, q_base, q_len, sk):
    bsz = qb.shape[0]
    blk0 = q_base // tq
    row = lambda b, i: (b, blk0 + i, 0)
    keys = lambda b, i: (b, 0, 0)
    return pl.pallas_call(
        functools.partial(_dsa_kernel, tq=tq, k_sel=k_sel, q_base=q_base),
        out_shape=jax.ShapeDtypeStruct((bsz, q_len, WIDTH), BF16),
        grid=(bsz, q_len // tq),
        in_specs=[
            pl.BlockSpec((1, tq, N_IDX_HEADS * IDX_DIM), row),
            pl.BlockSpec((1, N_IDX_HEADS, tq), lambda b, i: (b, 0, blk0 + i)),
            pl.BlockSpec((1, tq, WIDTH), row),
            pl.BlockSpec((1, sk, IDX_DIM), keys),
            pl.BlockSpec((1, sk, WIDTH), keys),
            pl.BlockSpec((1, sk, WIDTH), keys),
        ],
        out_specs=pl.BlockSpec((1, tq, WIDTH), lambda b, i: (b, i, 0)),
        scratch_shapes=[pltpu.VMEM((sk, tq), F32), pltpu.VMEM((tq, sk), F32),
                        pltpu.VMEM((N_HEADS, tq, sk), F32), pltpu.VMEM((N_HEADS, tq, sk), BF16)],
        compiler_params=_cparams(("arbitrary", "arbitrary")),
        name=f"dsa_k{sk}",
    )(qi, wit, qb, ki, kb, vb)


def _dsa(qi, wit, qb, ki, kb, vb, tq):
    s = qb.shape[1]
    k_sel = min(TOPK_MAX, s // 4)
    n_cls = max(1, min(DSA_KEY_CLASSES, s // KEY_TILE))
    q_len = s // n_cls
    outs = [_dsa_class(qi, wit, qb, ki, kb, vb, tq, k_sel, c * q_len, q_len, (c + 1) * q_len)
            for c in range(n_cls)]
    return jnp.concatenate(outs, axis=1)


def _merge_kernel(oa_ref, ob_ref, ga_ref, gb_ref, x_ref, mod_ref, wba_ref, wbb_ref, wo_ref,
                  ln1_ref, wr_ref, rb_ref, x1_ref, u2_ref, gk_ref, u2t_ref, expk_ref, rankk_ref, seen_ref,
                  seen_scr, *, alpha):
    tm = x_ref.shape[1]
    ya = _dot(oa_ref[0], wba_ref[...])
    yb = _dot(ob_ref[0], wbb_ref[...])
    merged = _sigmoid(ga_ref[0].astype(F32)) * ya + _sigmoid(gb_ref[0].astype(F32)) * yb
    mix = _dot(merged.astype(BF16), wo_ref[...])
    mod = mod_ref[0]
    g1 = mod[2:3, :]
    sh2 = mod[3:4, :]
    sc2 = mod[4:5, :]
    ln1 = ln1_ref[...]
    x1 = _ln(alpha * x_ref[0] + g1 * mix) * ln1[0:1, :] + ln1[1:2, :]
    x1_ref[0] = x1
    u2 = _ln(x1) * (1.0 + sc2) + sh2
    u2_ref[0] = u2.astype(BF16)

    u_hi, u_lo = _split_bf16(u2)
    both = _dot(u_hi, wr_ref[...])
    logits = both[:, 0:LANES] + both[:, LANES:] + _dot(u_lo, wr_ref[:, 0:LANES])
    aff = _sigmoid(logits.T[0:N_EXPERTS, :])
    biased = aff + rb_ref[...]
    grp = biased.reshape(N_GROUPS, GROUP_SIZE, tm)
    sub = lax.broadcasted_iota(jnp.int32, grp.shape, 1)
    m1 = jnp.max(grp, axis=1, keepdims=True)
    first = jnp.min(jnp.where(grp == m1, sub, GROUP_SIZE), axis=1, keepdims=True)
    m2 = jnp.max(jnp.where(sub == first, NEG_INF, grp), axis=1, keepdims=True)
    gscore = (m1 + m2).reshape(N_GROUPS, tm)
    g_i = lax.broadcasted_iota(jnp.int32, (N_GROUPS, tm), 0)
    g_rank = jnp.zeros((N_GROUPS, tm), F32)
    for g in range(N_GROUPS):
        other = gscore[g:g + 1, :]
        beats = (other > gscore) | ((other == gscore) & (g < g_i))
        g_rank = g_rank + jnp.where(beats, 1.0, 0.0)
    g_keep = jnp.where(g_rank < TOPK_GROUPS, 1.0, 0.0).reshape(N_GROUPS, 1, tm)
    e_keep = jnp.broadcast_to(g_keep, (N_GROUPS, GROUP_SIZE, tm)).reshape(N_EXPERTS, tm)
    masked = jnp.where(e_keep > 0.5, biased, NEG_INF)
    e_i = lax.broadcasted_iota(jnp.int32, (N_EXPERTS, tm), 0)
    chosen = jnp.zeros((N_EXPERTS, tm), F32)
    picks = []
    for _ in range(TOPK_EXPERTS):
        best = jnp.max(masked, axis=0, keepdims=True)
        first = jnp.min(jnp.where(masked == best, e_i, N_EXPERTS), axis=0, keepdims=True)
        hit = e_i == first
        chosen = jnp.where(hit, 1.0, chosen)
        masked = jnp.where(hit, NEG_INF, masked)
        picks.append(first)
    top_aff = jnp.where(chosen > 0.5, aff, 0.0)
    comb_t = top_aff / jnp.sum(top_aff, axis=0, keepdims=True) * ROUTED_SCALE

    @pl.when((pl.program_id(0) == 0) & (pl.program_id(1) == 0))
    def _():
        seen_scr[...] = jnp.zeros_like(seen_scr)

    r_i = lax.broadcasted_iota(jnp.int32, (tm, tm), 0)
    c_i = lax.broadcasted_iota(jnp.int32, (tm, tm), 1)
    earlier = jnp.where(r_i < c_i, 1.0, 0.0).astype(BF16)
    arrival = _dot(chosen.astype(BF16), earlier) + seen_scr[:, 0:1]
    seen_scr[...] = seen_scr[...] + jnp.sum(chosen, axis=1, keepdims=True)
    seen_ref[...] = seen_scr[...]
    ranks, gates = [], []
    for first in picks:
        hit = e_i == first
        ranks.append(jnp.sum(jnp.where(hit, arrival, 0.0), axis=0, keepdims=True))
        gates.append(jnp.sum(jnp.where(hit, comb_t, 0.0), axis=0, keepdims=True))
    expk_ref[...] = jnp.concatenate(picks, axis=0)
    rankk_ref[...] = jnp.concatenate(ranks, axis=0).astype(jnp.int32)
    gate_rows = jnp.concatenate(gates + [jnp.zeros((LANES - TOPK_EXPERTS, tm), F32)], axis=0)
    gk_ref[0] = gate_rows.T
    _store_token_rows(u2t_ref, _pack_pairs(u2), tm)


def _merge(oa, ob, ga, gb, x, mod, wba, wbb, wo, ln1, wr, rb, tm, alpha):
    bsz, s, d = x.shape
    row = lambda b, i: (b, i, 0)
    per_batch = s // tm
    flat = lambda b, i: (b * per_batch + i, 0)
    cols = lambda b, i: (0, b * per_batch + i)
    w2 = lambda shape: pl.BlockSpec(shape, lambda b, i: (0,) * len(shape))
    return pl.pallas_call(
        functools.partial(_merge_kernel, alpha=alpha),
        out_shape=[jax.ShapeDtypeStruct((bsz, s, d), F32),
                   jax.ShapeDtypeStruct((bsz, s, d), BF16),
                   jax.ShapeDtypeStruct((bsz, s, LANES), F32),
                   jax.ShapeDtypeStruct((bsz * s * TILE_ROWS, LANES), jnp.uint32),
                   jax.ShapeDtypeStruct((TOPK_EXPERTS, bsz * s), jnp.int32),
                   jax.ShapeDtypeStruct((TOPK_EXPERTS, bsz * s), jnp.int32),
                   jax.ShapeDtypeStruct((N_EXPERTS, LANES), F32)],
        grid=(bsz, s // tm),
        in_specs=[
            pl.BlockSpec((1, tm, WIDTH), row), pl.BlockSpec((1, tm, WIDTH), row),
            pl.BlockSpec((1, tm, d), row), pl.BlockSpec((1, tm, d), row),
            pl.BlockSpec((1, tm, d), row),
            pl.BlockSpec((1, N_MOD, d), lambda b, i: (b, 0, 0)),
            w2(wba.shape), w2(wbb.shape), w2(wo.shape), w2(ln1.shape), w2(wr.shape), w2(rb.shape),
        ],
        out_specs=[pl.BlockSpec((1, tm, d), row), pl.BlockSpec((1, tm, d), row),
                   pl.BlockSpec((1, tm, LANES), row),
                   pl.BlockSpec((tm * TILE_ROWS, LANES), flat),
                   pl.BlockSpec((TOPK_EXPERTS, tm), cols),
                   pl.BlockSpec((TOPK_EXPERTS, tm), cols),
                   pl.BlockSpec((N_EXPERTS, LANES), lambda b, i: (0, 0))],
        scratch_shapes=[pltpu.VMEM((N_EXPERTS, LANES), F32)],
        compiler_params=_cparams(("arbitrary", "arbitrary")),
        name="merge",
    )(oa, ob, ga, gb, x, mod, wba, wbb, wo, ln1, wr, rb)


def _silu(z):
    return z * _sigmoid(z)


def _sc_params():
    return pltpu.CompilerParams(use_tc_tiling_on_sc=False)


def _sc_mesh():
    return plsc.VectorSubcoreMesh(core_axis_name="c", subcore_axis_name="s",
                                  num_cores=SC_CORES, num_subcores=SC_SUBCORES)


def _sc_dispatch(x_tiles, dest, n_rows):
    n_tok = x_tiles.shape[0]
    n_k = dest.shape[0]
    mesh = _sc_mesh()

    @pl.kernel(out_type=jax.ShapeDtypeStruct((n_rows,) + x_tiles.shape[1:], x_tiles.dtype), mesh=mesh,
               scratch_types=[], compiler_params=_sc_params())
    def scatter_rows(x_hbm, *rest):
        idx_hbm, o_hbm = rest[:n_k], rest[n_k]

        def body(x_vmem, *idx_vmem):
            for k in range(n_k):
                pltpu.sync_copy(x_vmem, o_hbm.at[idx_vmem[k].at[0]])

        pltpu.emit_pipeline(
            body, grid=(n_tok // SC_WINDOW,),
            in_specs=[pl.BlockSpec((SC_WINDOW,) + x_tiles.shape[1:], lambda i: (i, 0, 0))]
            + [pl.BlockSpec((1, SC_WINDOW), lambda i: (0, i))] * n_k,
            out_specs=[], core_axis_name=("c", "s"), dimension_semantics=(pltpu.PARALLEL,),
        )(x_hbm, *idx_hbm)

    return scatter_rows(x_tiles, *[dest[k:k + 1] for k in range(n_k)])


def _sc_gather(rows, idx):
    n = idx.shape[1]
    mesh = _sc_mesh()

    @pl.kernel(out_type=jax.ShapeDtypeStruct((n,) + rows.shape[1:], rows.dtype), mesh=mesh,
               scratch_types=[], compiler_params=_sc_params())
    def gather_rows(r_hbm, i_hbm, o_hbm):
        def body(i_vmem, o_vmem):
            pltpu.sync_copy(r_hbm.at[i_vmem.at[0]], o_vmem)

        pltpu.emit_pipeline(
            body, grid=(n // SC_WINDOW,),
            in_specs=[pl.BlockSpec((1, SC_WINDOW), lambda i: (0, i))],
            out_specs=[pl.BlockSpec((SC_WINDOW,) + rows.shape[1:], lambda i: (i, 0, 0))],
            core_axis_name=("c", "s"), dimension_semantics=(pltpu.PARALLEL,),
        )(i_hbm, o_hbm)

    return gather_rows(rows, idx)


def _ffn_grouped_kernel(expert_ref, valid_ref, used_ref, x_ref, wg_ref, wu_ref, wd_ref, y_ref):
    i = pl.program_id(0)
    n_rows = x_ref.shape[0] // TILE_ROWS

    @pl.when(i < used_ref[0])
    def _():
        row = lax.broadcasted_iota(jnp.int32, (n_rows, 1), 0)
        words = jnp.where(row < valid_ref[i], _load_token_rows(x_ref, n_rows), jnp.uint32(0))
        x = _unpack_pairs(words).astype(BF16)
        h = _silu(_dot(x, wg_ref[0])) * _dot(x, wu_ref[0])
        y = _dot(h.astype(BF16), wd_ref[0])
        _store_token_rows(y_ref, _pack_pairs(y), n_rows)


def _ffn_grouped(tile_expert, tile_valid, tiles_used, xs, wg, wu, wd, n_tiles):
    _, d, d_exp = wg.shape
    blk = pl.BlockSpec((MOE_ROW_TILE * TILE_ROWS, LANES), lambda i, te, tv, nu: (i, 0))
    weight = lambda shape: pl.BlockSpec(shape, lambda i, te, tv, nu: (te[i], 0, 0))
    return pl.pallas_call(
        _ffn_grouped_kernel,
        out_shape=jax.ShapeDtypeStruct(xs.shape, xs.dtype),
        grid_spec=pltpu.PrefetchScalarGridSpec(
            num_scalar_prefetch=3, grid=(n_tiles,),
            in_specs=[blk, weight((1, d, d_exp)), weight((1, d, d_exp)), weight((1, d_exp, d))],
            out_specs=blk),
        compiler_params=_cparams(("arbitrary",)),
        name="moe_ffn",
    )(tile_expert, tile_valid, tiles_used, xs, wg, wu, wd)


def _moe_final_kernel(yg_ref, gk_ref, u2_ref, x1_ref, mod_ref, wsg_ref, wsu_ref, wsd_ref, ln2_ref, o_ref,
                      *, alpha):
    tm = x1_ref.shape[0]
    t = u2_ref[...]
    ffn = _dot((_silu(_dot(t, wsg_ref[...])) * _dot(t, wsu_ref[...])).astype(BF16), wsd_ref[...])
    gk = gk_ref[...]
    for k in range(TOPK_EXPERTS):
        ffn = ffn + _unpack_pairs(_load_token_rows(yg_ref, tm, (k,))) * gk[:, k:k + 1]
    mod = mod_ref[0]
    g2 = mod[5:6, :]
    ln2 = ln2_ref[...]
    o_ref[...] = _ln(alpha * x1_ref[...] + g2 * ffn) * ln2[0:1, :] + ln2[1:2, :]


def _moe_final(yg, gk, u2, x1, mod, wsg, wsu, wsd, ln2, tm, tokens_per_batch, tok0, alpha):
    t, d = u2.shape
    n_tok = yg.shape[1] // TILE_ROWS
    blocks_per_batch = tokens_per_batch // tm
    blk0 = tok0 // tm
    row = lambda i: (blk0 + i, 0)
    w2 = lambda shape: pl.BlockSpec(shape, lambda i: (0,) * len(shape))
    return pl.pallas_call(
        functools.partial(_moe_final_kernel, alpha=alpha),
        out_shape=jax.ShapeDtypeStruct((t, d), F32),
        grid=(n_tok // tm,),
        in_specs=[
            pl.BlockSpec((TOPK_EXPERTS, tm * TILE_ROWS, LANES), lambda i: (0, i, 0)),
            pl.BlockSpec((tm, LANES), row),
            pl.BlockSpec((tm, d), row),
            pl.BlockSpec((tm, d), row),
            pl.BlockSpec((1, N_MOD, d), lambda i: ((blk0 + i) // blocks_per_batch, 0, 0)),
            w2(wsg.shape), w2(wsu.shape), w2(wsd.shape), w2(ln2.shape),
        ],
        out_specs=pl.BlockSpec((tm, d), row),
        input_output_aliases={3: 0},
        compiler_params=_cparams(("arbitrary",)),
        name="moe_final",
    )(yg, gk, u2, x1, mod, wsg, wsu, wsd, ln2)


def _sorted_layout(expk, rankk, counts, n_tiles):
    e_ids = jnp.arange(N_EXPERTS, dtype=jnp.int32)[:, None, None]
    is_e = expk[None] == e_ids
    tiles = (counts + MOE_ROW_TILE - 1) // MOE_ROW_TILE
    tile_end = jnp.cumsum(tiles)
    first_row = (tile_end - tiles) * MOE_ROW_TILE
    dest = jnp.sum(jnp.where(is_e, first_row[:, None, None], 0), axis=0, dtype=jnp.int32) + rankk
    tile_ids = jnp.arange(n_tiles, dtype=jnp.int32)[:, None]
    done = tile_ids >= tile_end[None, :]
    owner = ~done & (tile_ids >= (tile_end - tiles)[None, :])
    tile_expert = jnp.minimum(jnp.sum(done, axis=1, dtype=jnp.int32), N_EXPERTS - 1)
    rows_before = (tile_ids[:, 0] - jnp.sum(jnp.where(done, tiles[None, :], 0), axis=1)) * MOE_ROW_TILE
    owner_count = jnp.sum(jnp.where(owner, counts[None, :], 0), axis=1)
    tile_valid = jnp.clip(owner_count - rows_before, 0, MOE_ROW_TILE)
    return dest, tile_expert, tile_valid, tile_end[-1:]


def _rope_tables(positions):
    inv_freq = ROPE_THETA ** (-jnp.arange(0, ROPE_DIM, 2, dtype=F32) / ROPE_DIM)
    ang = positions.astype(F32)[..., None] * inv_freq
    cos, sin = jnp.cos(ang), jnp.sin(ang)
    ones = jnp.ones(cos.shape[:-1] + (HEAD_DIM - ROPE_DIM,), F32)
    zeros = jnp.zeros_like(ones)
    zh = jnp.zeros_like(sin)
    c = jnp.concatenate([cos, cos, ones], axis=-1)
    s1 = jnp.concatenate([-sin, zh, zeros], axis=-1)
    s2 = jnp.concatenate([zh, sin, zeros], axis=-1)
    reps = LANES // HEAD_DIM
    return jnp.concatenate([jnp.tile(c, reps), jnp.tile(s1, reps), jnp.tile(s2, reps)], axis=-1)


def _band_bias_table(rel_bias, tq):
    n_heads = rel_bias.shape[0]
    win = tq + BAND_PAD
    row_len = win + tq
    n_high = BAND_PAD - MAX_REL + 1
    n_ramp = 2 * MAX_REL - 1
    high = rel_bias[:, 2 * MAX_REL:]
    profile = jnp.concatenate([
        jnp.broadcast_to(high, (n_heads, n_high)),
        rel_bias[:, n_ramp:0:-1],
        jnp.broadcast_to(rel_bias[:, :1], (n_heads, win - n_high - n_ramp)),
        jnp.broadcast_to(high, (n_heads, tq)),
    ], axis=1)
    skew = jnp.broadcast_to(profile[:, None, :], (n_heads, tq, row_len)).reshape(n_heads, tq * row_len)
    skew = skew[:, :tq * (row_len - 1)].reshape(n_heads, tq, row_len - 1)[:, :, :win]
    qi = np.arange(tq)[:, None]
    kj = np.arange(win)[None, :]
    q_chunk = qi // CHUNK
    k_chunk = kj // CHUNK - (BAND_CHUNKS - 1)
    in_band = (k_chunk <= q_chunk) & (k_chunk >= q_chunk - (BAND_CHUNKS - 1))
    return jnp.where(jnp.asarray(in_band)[None], skew, NEG_INF)


def kernel(x, c, positions, w_ada, b_ada, w_in, rel_bias, idx_k_norm_g, idx_k_norm_b, w_branch_a,
           w_branch_b, w_out, ln1_g, ln1_b, w_router, router_bias, w_exp_gate, w_exp_up, w_exp_down,
           w_sh_gate, w_sh_up, w_sh_down, ln2_g, ln2_b):
    bsz, s, d = x.shape
    depth = w_ada.shape[0]
    alpha = (2.0 * depth) ** 0.25
    tm_proj = min(512, s)
    tq_band = min(256, s)
    tq_dsa = min(256, s)
    tm_merge = min(512, s)
    tm_moe = min(256, s)

    rope_tab = _rope_tables(positions)
    n7 = 7 * WIDTH
    for l in range(depth):
        mod = _ada(c, w_ada[l], b_ada[l]).reshape(bsz, N_MOD, d)
        w_l = w_in[l]
        w7 = w_l[:, :n7].astype(BF16)
        n_kw = IDX_DIM + N_IDX_HEADS
        wkw = jnp.pad(w_l[:, n7:n7 + n_kw], ((0, 0), (0, LANES - n_kw))).astype(BF16)
        wg = w_l[:, n7 + n_kw:].astype(BF16)
        lnk = jnp.pad(jnp.stack([idx_k_norm_g[l], idx_k_norm_b[l]]), ((0, 0), (0, LANES - IDX_DIM)))
        qa, ka, va, qb, kb, vb, qi, ki, wi, ga, gb = _proj(x, mod, rope_tab, w7, wkw, wg, lnk, tm_proj)

        oa = _band(qa, ka, va, _band_bias_table(rel_bias[l], tq_band), tq_band)
        ob = _dsa(qi, wi, qb, ki, kb, vb, tq_dsa)

        wr = jnp.pad(w_router[l], ((0, 0), (0, LANES - N_EXPERTS)))
        wr_hi = wr.astype(BF16)
        wr_lo = (wr - wr_hi.astype(F32)).astype(BF16)
        x1, u2, gk, u2_tiles, expk, rankk, seen = _merge(
            oa, ob, ga, gb, x, mod,
            w_branch_a[l].astype(BF16), w_branch_b[l].astype(BF16), w_out[l].astype(BF16),
            jnp.stack([ln1_g[l], ln1_b[l]]), jnp.concatenate([wr_hi, wr_lo], axis=1),
            router_bias[l].reshape(N_EXPERTS, 1), tm_merge, alpha)

        t = bsz * s
        n_tiles = t * TOPK_EXPERTS // MOE_ROW_TILE + N_EXPERTS
        n_rows = n_tiles * MOE_ROW_TILE
        dest, tile_expert, tile_valid, tiles_used = _sorted_layout(
            expk, rankk, seen[:, 0].astype(jnp.int32), n_tiles)
        xs = _sc_dispatch(u2_tiles.reshape(t, TILE_ROWS, LANES), dest, n_rows)
        ys = _ffn_grouped(tile_expert, tile_valid, tiles_used, xs.reshape(n_rows * TILE_ROWS, LANES),
                          w_exp_gate[l].astype(BF16), w_exp_up[l].astype(BF16), w_exp_down[l].astype(BF16),
                          n_tiles)
        ys = ys.reshape(n_rows, TILE_ROWS, LANES)
        shared = (w_sh_gate[l].astype(BF16), w_sh_up[l].astype(BF16), w_sh_down[l].astype(BF16))
        ln2 = jnp.stack([ln2_g[l], ln2_b[l]])
        chunk = t // MOE_COMBINE_CHUNKS
        out = x1.reshape(t, d)
        for c in range(MOE_COMBINE_CHUNKS):
            idx = dest[:, c * chunk:(c + 1) * chunk].reshape(1, TOPK_EXPERTS * chunk)
            yg = _sc_gather(ys, idx).reshape(TOPK_EXPERTS, chunk * TILE_ROWS, LANES)
            out = _moe_final(yg, gk.reshape(t, LANES), u2.reshape(t, d), out, mod,
                             *shared, ln2, tm_moe, s, c * chunk, alpha)
        x = out.reshape(bsz, s, d)
    return x
```

```python
import functools

import jax
import jax.numpy as jnp
import numpy as np
from jax import lax
from jax.experimental import pallas as pl
from jax.experimental.pallas import tpu as pltpu
from jax.experimental.pallas import tpu_sc as plsc

F32 = jnp.float32
BF16 = jnp.bfloat16
NEG_INF = float("-inf")

CHUNK = 64
HEAD_DIM = 64
N_HEADS = 8
WIDTH = N_HEADS * HEAD_DIM
BAND_CHUNKS = 9
BAND_PAD = (BAND_CHUNKS - 1) * CHUNK
MAX_REL = 128
ROPE_THETA = 500000.0
ROPE_DIM = HEAD_DIM // 4
ROPE_HALF = ROPE_DIM // 2
N_IDX_HEADS = 8
IDX_DIM = 64
TOPK_MAX = 256
N_EXPERTS = 64
N_GROUPS = 8
GROUP_SIZE = N_EXPERTS // N_GROUPS
TOPK_GROUPS = 4
TOPK_EXPERTS = 8
ROUTED_SCALE = 1.0
N_MOD = 6
LN_EPS = 1e-5
LANES = 128
KEY_TILE = 256

VMEM_LIMIT = 56 * 1024 * 1024

DSA_KEY_CLASSES = 8
TILE_ROWS = 4
MOE_ROW_TILE = 512
SC_WINDOW = 64
MOE_COMBINE_CHUNKS = 4
SC_CORES = 2
SC_SUBCORES = 16


def _cparams(sem):
    return pltpu.CompilerParams(dimension_semantics=sem, vmem_limit_bytes=VMEM_LIMIT)


def _ln(z):
    mu = jnp.mean(z, axis=-1, keepdims=True)
    zc = z - mu
    var = jnp.mean(zc * zc, axis=-1, keepdims=True)
    return zc * lax.rsqrt(var + LN_EPS)


def _sigmoid(z):
    return 1.0 / (1.0 + jnp.exp(-z))


def _dot(a, b):
    return jnp.dot(a, b, preferred_element_type=F32)


def _dot_nt(a, b):
    return lax.dot_general(a, b, (((1,), (1,)), ((), ())), preferred_element_type=F32)


def _split_bf16(z):
    hi = z.astype(BF16)
    lo = (z - hi.astype(F32)).astype(BF16)
    return hi, lo


HIGH_HALF = 0xFFFF0000


def _pack_pairs(z):
    half = z.shape[1] // 2

    def bits(v):
        return lax.bitcast_convert_type(v.astype(BF16).astype(F32), jnp.uint32)

    return (bits(z[:, :half]) >> 16) | (bits(z[:, half:]) & jnp.uint32(HIGH_HALF))


def _unpack_pairs(p):
    low = lax.bitcast_convert_type(p << 16, F32)
    high = lax.bitcast_convert_type(p & jnp.uint32(HIGH_HALF), F32)
    return jnp.concatenate([low, high], axis=1)


def _store_token_rows(ref, words, n_tok):
    for q in range(TILE_ROWS):
        ref[pl.ds(q, n_tok, stride=TILE_ROWS), :] = words[:, q * LANES:(q + 1) * LANES]


def _load_token_rows(ref, n_tok, lead=()):
    return jnp.concatenate([ref[lead + (pl.ds(q, n_tok, stride=TILE_ROWS), slice(None))]
                            for q in range(TILE_ROWS)], axis=1)


def _ada_kernel(c_ref, w_ref, b_ref, o_ref):
    c = c_ref[...]
    ca = c * _sigmoid(c)
    ca_hi, ca_lo = _split_bf16(ca)
    w = w_ref[...]
    w_hi, w_lo = _split_bf16(w)
    acc = _dot(ca_hi, w_hi) + _dot(ca_lo, w_hi) + _dot(ca_hi, w_lo)
    o_ref[...] = acc + b_ref[...]


def _ada(c, w_ada, b_ada):
    bsz, d = c.shape
    n = w_ada.shape[1]
    tn = 1024
    return pl.pallas_call(
        _ada_kernel,
        out_shape=jax.ShapeDtypeStruct((bsz, n), F32),
        grid=(n // tn,),
        in_specs=[
            pl.BlockSpec((bsz, d), lambda j: (0, 0)),
            pl.BlockSpec((d, tn), lambda j: (0, j)),
            pl.BlockSpec((1, tn), lambda j: (0, j)),
        ],
        out_specs=pl.BlockSpec((bsz, tn), lambda j: (0, j)),
        compiler_params=_cparams(("arbitrary",)),
        name="ada",
    )(c, w_ada, b_ada.reshape(1, n))


def _rope(z, c_t, s1_t, s2_t):
    n = z.shape[-1]
    return z * c_t + pltpu.roll(z, n - ROPE_HALF, 1) * s1_t + pltpu.roll(z, ROPE_HALF, 1) * s2_t


def _proj_kernel(x_ref, mod_ref, rope_ref, w7_ref, wkw_ref, wg_ref, lnk_ref,
                 qa_ref, ka_ref, va_ref, qb_ref, kb_ref, vb_ref, qi_ref, ki_ref, wi_ref,
                 ga_ref, gb_ref):
    x = x_ref[0]
    mod = mod_ref[0]
    sh1 = mod[0:1, :]
    sc1 = mod[1:2, :]
    u = (_ln(x) * (1.0 + sc1) + sh1).astype(BF16)

    rope = rope_ref[0]
    c1 = rope[:, 0:LANES]
    s1 = rope[:, LANES:2 * LANES]
    s2 = rope[:, 2 * LANES:3 * LANES]
    reps = WIDTH // LANES
    c_t = jnp.concatenate([c1] * reps, axis=1)
    s1_t = jnp.concatenate([s1] * reps, axis=1)
    s2_t = jnp.concatenate([s2] * reps, axis=1)

    att_scale = HEAD_DIM ** -0.5
    idx_scale = IDX_DIM ** -0.5

    def seg(k):
        return _dot(u, w7_ref[:, k * WIDTH:(k + 1) * WIDTH])

    qa_ref[0] = (seg(0) * att_scale).astype(BF16)
    ka_ref[0] = seg(1).astype(BF16)
    va_ref[0] = seg(2).astype(BF16)
    qb_ref[0] = (_rope(seg(3), c_t, s1_t, s2_t) * att_scale).astype(BF16)
    kb_ref[0] = _rope(seg(4), c_t, s1_t, s2_t).astype(BF16)
    vb_ref[0] = seg(5).astype(BF16)
    qi_ref[0] = (_rope(seg(6), c_t, s1_t, s2_t) * idx_scale).astype(BF16)

    z = _dot(u, wkw_ref[...])
    lane = lax.broadcasted_iota(jnp.int32, z.shape, 1)
    is_k = lane < IDX_DIM
    mu = jnp.sum(jnp.where(is_k, z, 0.0), axis=-1, keepdims=True) * (1.0 / IDX_DIM)
    zc = jnp.where(is_k, z - mu, 0.0)
    var = jnp.sum(zc * zc, axis=-1, keepdims=True) * (1.0 / IDX_DIM)
    lnk = lnk_ref[...]
    y = zc * lax.rsqrt(var + LN_EPS) * lnk[0:1, :] + lnk[1:2, :]
    y = _rope(y, c1, jnp.where(is_k, s1, 0.0), jnp.where(is_k, s2, 0.0))
    ki_ref[0] = y[:, 0:IDX_DIM].astype(BF16)
    wi_ref[0] = z.T[IDX_DIM:IDX_DIM + N_IDX_HEADS, :] * (N_IDX_HEADS ** -0.5)

    d = ga_ref.shape[-1]
    ga_ref[0] = _dot(u, wg_ref[:, 0:d]).astype(BF16)
    gb_ref[0] = _dot(u, wg_ref[:, d:2 * d]).astype(BF16)


def _proj(x, mod, rope_tab, w7, wkw, wg, lnk, tm):
    bsz, s, d = x.shape
    const = dict(pipeline_mode=pl.Buffered(1))
    row = lambda b, i: (b, i, 0)
    wspec = lambda shape: pl.BlockSpec(shape, lambda b, i: (0, 0), **const)
    out_w = jax.ShapeDtypeStruct((bsz, s, WIDTH), BF16)
    out_d = jax.ShapeDtypeStruct((bsz, s, d), BF16)
    return pl.pallas_call(
        _proj_kernel,
        out_shape=[out_w] * 7 + [
            jax.ShapeDtypeStruct((bsz, s, IDX_DIM), BF16),
            jax.ShapeDtypeStruct((bsz, N_IDX_HEADS, s), F32),
            out_d, out_d],
        grid=(bsz, s // tm),
        in_specs=[
            pl.BlockSpec((1, tm, d), row),
            pl.BlockSpec((1, N_MOD, d), lambda b, i: (b, 0, 0)),
            pl.BlockSpec((1, tm, 3 * LANES), row),
            wspec(w7.shape), wspec(wkw.shape), wspec(wg.shape), wspec(lnk.shape),
        ],
        out_specs=[pl.BlockSpec((1, tm, WIDTH), row)] * 7 + [
            pl.BlockSpec((1, tm, IDX_DIM), row),
            pl.BlockSpec((1, N_IDX_HEADS, tm), lambda b, i: (b, 0, i)),
            pl.BlockSpec((1, tm, d), row), pl.BlockSpec((1, tm, d), row)],
        compiler_params=_cparams(("arbitrary", "arbitrary")),
        name="proj",
    )(x, mod, rope_tab, w7, wkw, wg, lnk)


def _band_kernel(q_ref, k_ref, v_ref, bias_ref, o_ref, kpad, vpad, a_scr, p_scr, *, tq):
    i = pl.program_id(1)
    s = k_ref.shape[1]
    win = tq + BAND_PAD

    @pl.when(i == 0)
    def _():
        zeros = jnp.zeros((BAND_PAD, WIDTH), BF16)
        kpad[0:BAND_PAD, :] = zeros
        vpad[0:BAND_PAD, :] = zeros
        kpad[BAND_PAD:BAND_PAD + s, :] = k_ref[0]
        vpad[BAND_PAD:BAND_PAD + s, :] = v_ref[0]

    q0 = pl.multiple_of(i * tq, tq)
    n_lg = win // LANES
    col = lax.broadcasted_iota(jnp.int32, (tq, LANES), 1)
    first_frame = BAND_PAD - q0
    heads = [slice(h * HEAD_DIM, (h + 1) * HEAD_DIM) for h in range(N_HEADS)]

    def lanes(g):
        return slice(g * LANES, (g + 1) * LANES)

    row_max = []
    for h, sl in enumerate(heads):
        sc = _dot_nt(q_ref[0, :, sl], kpad[pl.ds(q0, win), sl])
        mx = jnp.full((tq, LANES), NEG_INF, F32)
        for g in range(n_lg):
            a = jnp.where(col >= first_frame - g * LANES, sc[:, lanes(g)] + bias_ref[h, :, lanes(g)], NEG_INF)
            a_scr[h, :, lanes(g)] = a
            mx = jnp.maximum(mx, a)
        row_max.append(jnp.max(mx, axis=-1, keepdims=True))
    row_sum = []
    for h in range(N_HEADS):
        m_b = jnp.broadcast_to(row_max[h], (tq, LANES))
        ls = jnp.zeros((tq, LANES), F32)
        for g in range(n_lg):
            p = jnp.exp(a_scr[h, :, lanes(g)] - m_b)
            ls = ls + p
            p_scr[h, :, lanes(g)] = p.astype(BF16)
        row_sum.append(jnp.sum(ls, axis=-1, keepdims=True))
    for h, sl in enumerate(heads):
        o = _dot(p_scr[h], vpad[pl.ds(q0, win), sl]) / row_sum[h]
        o_ref[0, :, sl] = o.astype(BF16)


def _band(qa, ka, va, bias_tab, tq):
    bsz, s, _ = qa.shape
    win = tq + BAND_PAD
    return pl.pallas_call(
        functools.partial(_band_kernel, tq=tq),
        out_shape=jax.ShapeDtypeStruct((bsz, s, WIDTH), BF16),
        grid=(bsz, s // tq),
        in_specs=[
            pl.BlockSpec((1, tq, WIDTH), lambda b, i: (b, i, 0)),
            pl.BlockSpec((1, s, WIDTH), lambda b, i: (b, 0, 0)),
            pl.BlockSpec((1, s, WIDTH), lambda b, i: (b, 0, 0)),
            pl.BlockSpec((N_HEADS, tq, win), lambda b, i: (0, 0, 0)),
        ],
        out_specs=pl.BlockSpec((1, tq, WIDTH), lambda b, i: (b, i, 0)),
        scratch_shapes=[pltpu.VMEM((BAND_PAD + s, WIDTH), BF16),
                        pltpu.VMEM((BAND_PAD + s, WIDTH), BF16),
                        pltpu.VMEM((N_HEADS, tq, win), F32),
                        pltpu.VMEM((N_HEADS, tq, win), BF16)],
        compiler_params=_cparams(("arbitrary", "arbitrary")),
        name="band",
    )(qa, ka, va, bias_tab)


BISECT_STEPS_PER_CHECK = 4
BISECT_MAX_CHECKS = 400
COUNT_ROWS = 64


def _dsa_kernel(qi_ref, wit_ref, qb_ref, ki_ref, kb_ref, vb_ref, o_ref, sct_scr, msk_scr, a_scr, p_scr,
                *, tq, k_sel, q_base):
    i = pl.program_id(1)
    sk = ki_ref.shape[1]
    n_kt = sk // KEY_TILE
    n_lg = sk // LANES
    q0 = q_base + i * tq
    kf = float(k_sel)

    def lanes(g):
        return slice(g * LANES, (g + 1) * LANES)

    def slab_reduce(fn, combine, init):
        acc = jnp.full((COUNT_ROWS, tq), init, F32)
        for r in range(sk // COUNT_ROWS):
            acc = combine(acc, fn(sct_scr[r * COUNT_ROWS:(r + 1) * COUNT_ROWS, :]))
        return acc

    def count(pred):
        part = slab_reduce(lambda t: jnp.where(pred(t), 1.0, 0.0), jnp.add, 0.0)
        return jnp.sum(part, axis=0, keepdims=True)

    w_t = wit_ref[0]
    t_pos = q0 + lax.broadcasted_iota(jnp.int32, (1, tq), 1)
    limit = (t_pos // CHUNK + 1) * CHUNK
    key_in_tile = lax.broadcasted_iota(jnp.int32, (KEY_TILE, tq), 0)
    for kt in range(n_kt):
        ks = slice(kt * KEY_TILE, (kt + 1) * KEY_TILE)
        ki_t = ki_ref[0, ks, :]
        acc = jnp.zeros((KEY_TILE, tq), F32)
        for h in range(N_IDX_HEADS):
            lg = _dot_nt(ki_t, qi_ref[0, :, h * IDX_DIM:(h + 1) * IDX_DIM])
            acc = acc + jnp.maximum(lg, 0.0) * w_t[h:h + 1, :]
        sct_scr[ks, :] = jnp.where(key_in_tile < limit - kt * KEY_TILE, acc, NEG_INF)

    smax = jnp.max(slab_reduce(lambda t: t, jnp.maximum, NEG_INF), axis=0, keepdims=True)
    smin = jnp.min(slab_reduce(lambda t: jnp.where(t == NEG_INF, jnp.inf, t), jnp.minimum, jnp.inf),
                   axis=0, keepdims=True)
    n_adm = limit.astype(F32)
    c_max = count(lambda t: t >= smax)
    c_pos = count(lambda t: t > 0.0)
    c_nn = count(lambda t: t >= 0.0)
    zero = jnp.zeros_like(smax)
    at_zero = (c_pos < kf) & (c_nn >= kf)
    below_zero = c_nn < kf
    lo = jnp.where(at_zero | ~below_zero, zero, smin)
    clo = jnp.where(at_zero | ~below_zero, c_nn, n_adm)
    hi = jnp.where(at_zero | below_zero, zero, smax)
    few = n_adm <= kf
    lo = jnp.where(few, smin, lo)
    clo = jnp.where(few, n_adm, clo)
    top_tied = (c_max >= kf) & ~few
    lo = jnp.where(top_tied, smax, lo)
    clo = jnp.where(top_tied, c_max, clo)
    hi = jnp.where(top_tied, smax, hi)
    done0 = jnp.where(few | at_zero | top_tied, 1.0, 0.0)

    def bisect(carry):
        lo, hi, clo, done, it = carry
        for _ in range(BISECT_STEPS_PER_CHECK):
            mid = 0.5 * lo + 0.5 * hi
            stuck = (mid <= lo) | (mid >= hi)
            c = count(lambda t: t >= mid)
            ge = c >= kf
            lo = jnp.where(ge, mid, lo)
            clo = jnp.where(ge, c, clo)
            hi = jnp.where(ge, hi, mid)
            done = jnp.where(stuck | (clo <= kf), 1.0, done)
        return lo, hi, clo, done, it + 1

    def not_converged(carry):
        _, _, _, done, it = carry
        return (jnp.min(done) < 0.5) & (it < BISECT_MAX_CHECKS)

    lo, hi, clo, _, _ = lax.while_loop(not_converged, bisect, (lo, hi, clo, done0, jnp.int32(0)))

    thr = lo
    c_gt = count(lambda t: t > thr)
    c_eq = count(lambda t: t == thr)
    need = kf - c_gt
    tie_cut = jnp.max(c_gt + c_eq - kf) > 0.0

    def with_ties():
        r_i = lax.broadcasted_iota(jnp.int32, (KEY_TILE, KEY_TILE), 0)
        c_i = lax.broadcasted_iota(jnp.int32, (KEY_TILE, KEY_TILE), 1)
        earlier = jnp.where(c_i < r_i, 1.0, 0.0).astype(BF16)
        carry = jnp.zeros((1, tq), F32)
        for kt in range(n_kt):
            ks = slice(kt * KEY_TILE, (kt + 1) * KEY_TILE)
            sc = sct_scr[ks, :]
            eq = sc == thr
            eq_f = jnp.where(eq, 1.0, 0.0)
            rank = _dot(earlier, eq_f.astype(BF16)) + carry
            keep = (sc > thr) | (eq & (rank < need))
            msk_scr[:, ks] = jnp.where(keep, 0.0, NEG_INF).T
            carry = carry + jnp.sum(eq_f, axis=0, keepdims=True)

    def without_ties():
        for kt in range(n_kt):
            ks = slice(kt * KEY_TILE, (kt + 1) * KEY_TILE)
            msk_scr[:, ks] = jnp.where(sct_scr[ks, :] >= thr, 0.0, NEG_INF).T

    lax.cond(tie_cut, with_ties, without_ties)

    heads = [slice(h * HEAD_DIM, (h + 1) * HEAD_DIM) for h in range(N_HEADS)]
    row_max = []
    for h, sl in enumerate(heads):
        qh = qb_ref[0, :, sl]
        mx = jnp.full((tq, LANES), NEG_INF, F32)
        for kt in range(n_kt):
            ks = slice(kt * KEY_TILE, (kt + 1) * KEY_TILE)
            a = _dot_nt(qh, kb_ref[0, ks, sl]) + msk_scr[:, ks]
            a_scr[h, :, ks] = a
            for g in range(KEY_TILE // LANES):
                mx = jnp.maximum(mx, a[:, lanes(g)])
        row_max.append(jnp.max(mx, axis=-1, keepdims=True))
    row_sum = []
    for h in range(N_HEADS):
        m_b = jnp.broadcast_to(row_max[h], (tq, LANES))
        ls = jnp.zeros((tq, LANES), F32)
        for g in range(n_lg):
            p = jnp.exp(a_scr[h, :, lanes(g)] - m_b)
            ls = ls + p
            p_scr[h, :, lanes(g)] = p.astype(BF16)
        row_sum.append(jnp.sum(ls, axis=-1, keepdims=True))
    for h, sl in enumerate(heads):
        o = _dot(p_scr[h], vb_ref[0, :, sl]) / row_sum[h]
        o_ref[0, :, sl] = o.astype(BF16)


def _dsa_class(qi, wit, qb, ki, kb, vb, tq, k_sel, q_base, q_len, sk):
    bsz = qb.shape[0]
    blk0 = q_base // tq
    row = lambda b, i: (b, blk0 + i, 0)
    keys = lambda b, i: (b, 0, 0)
    return pl.pallas_call(
        functools.partial(_dsa_kernel, tq=tq, k_sel=k_sel, q_base=q_base),
        out_shape=jax.ShapeDtypeStruct((bsz, q_len, WIDTH), BF16),
        grid=(bsz, q_len // tq),
        in_specs=[
            pl.BlockSpec((1, tq, N_IDX_HEADS * IDX_DIM), row),
            pl.BlockSpec((1, N_IDX_HEADS, tq), lambda b, i: (b, 0, blk0 + i)),
            pl.BlockSpec((1, tq, WIDTH), row),
            pl.BlockSpec((1, sk, IDX_DIM), keys),
            pl.BlockSpec((1, sk, WIDTH), keys),
            pl.BlockSpec((1, sk, WIDTH), keys),
        ],
        out_specs=pl.BlockSpec((1, tq, WIDTH), lambda b, i: (b, i, 0)),
        scratch_shapes=[pltpu.VMEM((sk, tq), F32), pltpu.VMEM((tq, sk), F32),
                        pltpu.VMEM((N_HEADS, tq, sk), F32), pltpu.VMEM((N_HEADS, tq, sk), BF16)],
        compiler_params=_cparams(("arbitrary", "arbitrary")),
        name=f"dsa_k{sk}",
    )(qi, wit, qb, ki, kb, vb)


def _dsa(qi, wit, qb, ki, kb, vb, tq):
    s = qb.shape[1]
    k_sel = min(TOPK_MAX, s // 4)
    n_cls = max(1, min(DSA_KEY_CLASSES, s // KEY_TILE))
    q_len = s // n_cls
    outs = [_dsa_class(qi, wit, qb, ki, kb, vb, tq, k_sel, c * q_len, q_len, (c + 1) * q_len)
            for c in range(n_cls)]
    return jnp.concatenate(outs, axis=1)


def _merge_kernel(oa_ref, ob_ref, ga_ref, gb_ref, x_ref, mod_ref, wba_ref, wbb_ref, wo_ref,
                  ln1_ref, wr_ref, rb_ref, x1_ref, u2_ref, gk_ref, u2t_ref, expk_ref, rankk_ref, seen_ref,
                  seen_scr, *, alpha):
    tm = x_ref.shape[1]
    ya = _dot(oa_ref[0], wba_ref[...])
    yb = _dot(ob_ref[0], wbb_ref[...])
    merged = _sigmoid(ga_ref[0].astype(F32)) * ya + _sigmoid(gb_ref[0].astype(F32)) * yb
    mix = _dot(merged.astype(BF16), wo_ref[...])
    mod = mod_ref[0]
    g1 = mod[2:3, :]
    sh2 = mod[3:4, :]
    sc2 = mod[4:5, :]
    ln1 = ln1_ref[...]
    x1 = _ln(alpha * x_ref[0] + g1 * mix) * ln1[0:1, :] + ln1[1:2, :]
    x1_ref[0] = x1
    u2 = _ln(x1) * (1.0 + sc2) + sh2
    u2_ref[0] = u2.astype(BF16)

    u_hi, u_lo = _split_bf16(u2)
    both = _dot(u_hi, wr_ref[...])
    logits = both[:, 0:LANES] + both[:, LANES:] + _dot(u_lo, wr_ref[:, 0:LANES])
    aff = _sigmoid(logits.T[0:N_EXPERTS, :])
    biased = aff + rb_ref[...]
    grp = biased.reshape(N_GROUPS, GROUP_SIZE, tm)
    sub = lax.broadcasted_iota(jnp.int32, grp.shape, 1)
    m1 = jnp.max(grp, axis=1, keepdims=True)
    first = jnp.min(jnp.where(grp == m1, sub, GROUP_SIZE), axis=1, keepdims=True)
    m2 = jnp.max(jnp.where(sub == first, NEG_INF, grp), axis=1, keepdims=True)
    gscore = (m1 + m2).reshape(N_GROUPS, tm)
    g_i = lax.broadcasted_iota(jnp.int32, (N_GROUPS, tm), 0)
    g_rank = jnp.zeros((N_GROUPS, tm), F32)
    for g in range(N_GROUPS):
        other = gscore[g:g + 1, :]
        beats = (other > gscore) | ((other == gscore) & (g < g_i))
        g_rank = g_rank + jnp.where(beats, 1.0, 0.0)
    g_keep = jnp.where(g_rank < TOPK_GROUPS, 1.0, 0.0).reshape(N_GROUPS, 1, tm)
    e_keep = jnp.broadcast_to(g_keep, (N_GROUPS, GROUP_SIZE, tm)).reshape(N_EXPERTS, tm)
    masked = jnp.where(e_keep > 0.5, biased, NEG_INF)
    e_i = lax.broadcasted_iota(jnp.int32, (N_EXPERTS, tm), 0)
    chosen = jnp.zeros((N_EXPERTS, tm), F32)
    picks = []
    for _ in range(TOPK_EXPERTS):
        best = jnp.max(masked, axis=0, keepdims=True)
        first = jnp.min(jnp.where(masked == best, e_i, N_EXPERTS), axis=0, keepdims=True)
        hit = e_i == first
        chosen = jnp.where(hit, 1.0, chosen)
        masked = jnp.where(hit, NEG_INF, masked)
        picks.append(first)
    top_aff = jnp.where(chosen > 0.5, aff, 0.0)
    comb_t = top_aff / jnp.sum(top_aff, axis=0, keepdims=True) * ROUTED_SCALE

    @pl.when((pl.program_id(0) == 0) & (pl.program_id(1) == 0))
    def _():
        seen_scr[...] = jnp.zeros_like(seen_scr)

    r_i = lax.broadcasted_iota(jnp.int32, (tm, tm), 0)
    c_i = lax.broadcasted_iota(jnp.int32, (tm, tm), 1)
    earlier = jnp.where(r_i < c_i, 1.0, 0.0).astype(BF16)
    arrival = _dot(chosen.astype(BF16), earlier) + seen_scr[:, 0:1]
    seen_scr[...] = seen_scr[...] + jnp.sum(chosen, axis=1, keepdims=True)
    seen_ref[...] = seen_scr[...]
    ranks, gates = [], []
    for first in picks:
        hit = e_i == first
        ranks.append(jnp.sum(jnp.where(hit, arrival, 0.0), axis=0, keepdims=True))
        gates.append(jnp.sum(jnp.where(hit, comb_t, 0.0), axis=0, keepdims=True))
    expk_ref[...] = jnp.concatenate(picks, axis=0)
    rankk_ref[...] = jnp.concatenate(ranks, axis=0).astype(jnp.int32)
    gate_rows = jnp.concatenate(gates + [jnp.zeros((LANES - TOPK_EXPERTS, tm), F32)], axis=0)
    gk_ref[0] = gate_rows.T
    _store_token_rows(u2t_ref, _pack_pairs(u2), tm)


def _merge(oa, ob, ga, gb, x, mod, wba, wbb, wo, ln1, wr, rb, tm, alpha):
    bsz, s, d = x.shape
    row = lambda b, i: (b, i, 0)
    per_batch = s // tm
    flat = lambda b, i: (b * per_batch + i, 0)
    cols = lambda b, i: (0, b * per_batch + i)
    w2 = lambda shape: pl.BlockSpec(shape, lambda b, i: (0,) * len(shape))
    return pl.pallas_call(
        functools.partial(_merge_kernel, alpha=alpha),
        out_shape=[jax.ShapeDtypeStruct((bsz, s, d), F32),
                   jax.ShapeDtypeStruct((bsz, s, d), BF16),
                   jax.ShapeDtypeStruct((bsz, s, LANES), F32),
                   jax.ShapeDtypeStruct((bsz * s * TILE_ROWS, LANES), jnp.uint32),
                   jax.ShapeDtypeStruct((TOPK_EXPERTS, bsz * s), jnp.int32),
                   jax.ShapeDtypeStruct((TOPK_EXPERTS, bsz * s), jnp.int32),
                   jax.ShapeDtypeStruct((N_EXPERTS, LANES), F32)],
        grid=(bsz, s // tm),
        in_specs=[
            pl.BlockSpec((1, tm, WIDTH), row), pl.BlockSpec((1, tm, WIDTH), row),
            pl.BlockSpec((1, tm, d), row), pl.BlockSpec((1, tm, d), row),
            pl.BlockSpec((1, tm, d), row),
            pl.BlockSpec((1, N_MOD, d), lambda b, i: (b, 0, 0)),
            w2(wba.shape), w2(wbb.shape), w2(wo.shape), w2(ln1.shape), w2(wr.shape), w2(rb.shape),
        ],
        out_specs=[pl.BlockSpec((1, tm, d), row), pl.BlockSpec((1, tm, d), row),
                   pl.BlockSpec((1, tm, LANES), row),
                   pl.BlockSpec((tm * TILE_ROWS, LANES), flat),
                   pl.BlockSpec((TOPK_EXPERTS, tm), cols),
                   pl.BlockSpec((TOPK_EXPERTS, tm), cols),
                   pl.BlockSpec((N_EXPERTS, LANES), lambda b, i: (0, 0))],
        scratch_shapes=[pltpu.VMEM((N_EXPERTS, LANES), F32)],
        compiler_params=_cparams(("arbitrary", "arbitrary")),
        name="merge",
    )(oa, ob, ga, gb, x, mod, wba, wbb, wo, ln1, wr, rb)


def _silu(z):
    return z * _sigmoid(z)


def _sc_params():
    return pltpu.CompilerParams(use_tc_tiling_on_sc=False)


def _sc_mesh():
    return plsc.VectorSubcoreMesh(core_axis_name="c", subcore_axis_name="s",
                                  num_cores=SC_CORES, num_subcores=SC_SUBCORES)


def _sc_dispatch(x_tiles, dest, n_rows):
    n_tok = x_tiles.shape[0]
    n_k = dest.shape[0]
    mesh = _sc_mesh()

    @pl.kernel(out_type=jax.ShapeDtypeStruct((n_rows,) + x_tiles.shape[1:], x_tiles.dtype), mesh=mesh,
               scratch_types=[], compiler_params=_sc_params())
    def scatter_rows(x_hbm, *rest):
        idx_hbm, o_hbm = rest[:n_k], rest[n_k]

        def body(x_vmem, *idx_vmem):
            for k in range(n_k):
                pltpu.sync_copy(x_vmem, o_hbm.at[idx_vmem[k].at[0]])

        pltpu.emit_pipeline(
            body, grid=(n_tok // SC_WINDOW,),
            in_specs=[pl.BlockSpec((SC_WINDOW,) + x_tiles.shape[1:], lambda i: (i, 0, 0))]
            + [pl.BlockSpec((1, SC_WINDOW), lambda i: (0, i))] * n_k,
            out_specs=[], core_axis_name=("c", "s"), dimension_semantics=(pltpu.PARALLEL,),
        )(x_hbm, *idx_hbm)

    return scatter_rows(x_tiles, *[dest[k:k + 1] for k in range(n_k)])


def _sc_gather(rows, idx):
    n = idx.shape[1]
    mesh = _sc_mesh()

    @pl.kernel(out_type=jax.ShapeDtypeStruct((n,) + rows.shape[1:], rows.dtype), mesh=mesh,
               scratch_types=[], compiler_params=_sc_params())
    def gather_rows(r_hbm, i_hbm, o_hbm):
        def body(i_vmem, o_vmem):
            pltpu.sync_copy(r_hbm.at[i_vmem.at[0]], o_vmem)

        pltpu.emit_pipeline(
            body, grid=(n // SC_WINDOW,),
            in_specs=[pl.BlockSpec((1, SC_WINDOW), lambda i: (0, i))],
            out_specs=[pl.BlockSpec((SC_WINDOW,) + rows.shape[1:], lambda i: (i, 0, 0))],
            core_axis_name=("c", "s"), dimension_semantics=(pltpu.PARALLEL,),
        )(i_hbm, o_hbm)

    return gather_rows(rows, idx)


def _ffn_grouped_kernel(expert_ref, valid_ref, used_ref, x_ref, wg_ref, wu_ref, wd_ref, y_ref,
                        wg_bf, wu_bf, wd_bf):
    i = pl.program_id(0)
    n_rows = x_ref.shape[0] // TILE_ROWS
    in_use = i < used_ref[0]
    new_expert = (i == 0) | (expert_ref[i] != expert_ref[jnp.maximum(i - 1, 0)])

    @pl.when(in_use & new_expert)
    def _():
        wg_bf[...] = wg_ref[0].astype(BF16)
        wu_bf[...] = wu_ref[0].astype(BF16)
        wd_bf[...] = wd_ref[0].astype(BF16)

    @pl.when(in_use)
    def _():
        row = lax.broadcasted_iota(jnp.int32, (n_rows, 1), 0)
        words = jnp.where(row < valid_ref[i], _load_token_rows(x_ref, n_rows), jnp.uint32(0))
        x = _unpack_pairs(words).astype(BF16)
        h = _silu(_dot(x, wg_bf[...])) * _dot(x, wu_bf[...])
        y = _dot(h.astype(BF16), wd_bf[...])
        _store_token_rows(y_ref, _pack_pairs(y), n_rows)


def _ffn_grouped(tile_expert, tile_valid, tiles_used, xs, wg, wu, wd, n_tiles):
    _, d, d_exp = wg.shape
    blk = pl.BlockSpec((MOE_ROW_TILE * TILE_ROWS, LANES), lambda i, te, tv, nu: (i, 0))
    weight = lambda shape: pl.BlockSpec(shape, lambda i, te, tv, nu: (te[i], 0, 0))
    return pl.pallas_call(
        _ffn_grouped_kernel,
        out_shape=jax.ShapeDtypeStruct(xs.shape, xs.dtype),
        grid_spec=pltpu.PrefetchScalarGridSpec(
            num_scalar_prefetch=3, grid=(n_tiles,),
            in_specs=[blk, weight((1, d, d_exp)), weight((1, d, d_exp)), weight((1, d_exp, d))],
            out_specs=blk,
            scratch_shapes=[pltpu.VMEM((d, d_exp), BF16), pltpu.VMEM((d, d_exp), BF16),
                            pltpu.VMEM((d_exp, d), BF16)]),
        compiler_params=_cparams(("arbitrary",)),
        name="moe_ffn",
    )(tile_expert, tile_valid, tiles_used, xs, wg, wu, wd)


def _moe_final_kernel(yg_ref, gk_ref, u2_ref, x1_ref, mod_ref, wsg_ref, wsu_ref, wsd_ref, ln2_ref, o_ref,
                      *, alpha):
    tm = x1_ref.shape[0]
    t = u2_ref[...]
    ffn = _dot((_silu(_dot(t, wsg_ref[...])) * _dot(t, wsu_ref[...])).astype(BF16), wsd_ref[...])
    gk = gk_ref[...]
    for k in range(TOPK_EXPERTS):
        ffn = ffn + _unpack_pairs(_load_token_rows(yg_ref, tm, (k,))) * gk[:, k:k + 1]
    mod = mod_ref[0]
    g2 = mod[5:6, :]
    ln2 = ln2_ref[...]
    o_ref[...] = _ln(alpha * x1_ref[...] + g2 * ffn) * ln2[0:1, :] + ln2[1:2, :]


def _moe_final(yg, gk, u2, x1, mod, wsg, wsu, wsd, ln2, tm, tokens_per_batch, tok0, alpha):
    t, d = u2.shape
    n_tok = yg.shape[1] // TILE_ROWS
    blocks_per_batch = tokens_per_batch // tm
    blk0 = tok0 // tm
    row = lambda i: (blk0 + i, 0)
    w2 = lambda shape: pl.BlockSpec(shape, lambda i: (0,) * len(shape))
    return pl.pallas_call(
        functools.partial(_moe_final_kernel, alpha=alpha),
        out_shape=jax.ShapeDtypeStruct((t, d), F32),
        grid=(n_tok // tm,),
        in_specs=[
            pl.BlockSpec((TOPK_EXPERTS, tm * TILE_ROWS, LANES), lambda i: (0, i, 0)),
            pl.BlockSpec((tm, LANES), row),
            pl.BlockSpec((tm, d), row),
            pl.BlockSpec((tm, d), row),
            pl.BlockSpec((1, N_MOD, d), lambda i: ((blk0 + i) // blocks_per_batch, 0, 0)),
            w2(wsg.shape), w2(wsu.shape), w2(wsd.shape), w2(ln2.shape),
        ],
        out_specs=pl.BlockSpec((tm, d), row),
        input_output_aliases={3: 0},
        compiler_params=_cparams(("arbitrary",)),
        name="moe_final",
    )(yg, gk, u2, x1, mod, wsg, wsu, wsd, ln2)


def _sorted_layout(expk, rankk, counts, n_tiles):
    e_ids = jnp.arange(N_EXPERTS, dtype=jnp.int32)[:, None, None]
    is_e = expk[None] == e_ids
    tiles = (counts + MOE_ROW_TILE - 1) // MOE_ROW_TILE
    tile_end = jnp.cumsum(tiles)
    first_row = (tile_end - tiles) * MOE_ROW_TILE
    dest = jnp.sum(jnp.where(is_e, first_row[:, None, None], 0), axis=0, dtype=jnp.int32) + rankk
    tile_ids = jnp.arange(n_tiles, dtype=jnp.int32)[:, None]
    done = tile_ids >= tile_end[None, :]
    owner = ~done & (tile_ids >= (tile_end - tiles)[None, :])
    tile_expert = jnp.minimum(jnp.sum(done, axis=1, dtype=jnp.int32), N_EXPERTS - 1)
    rows_before = (tile_ids[:, 0] - jnp.sum(jnp.where(done, tiles[None, :], 0), axis=1)) * MOE_ROW_TILE
    owner_count = jnp.sum(jnp.where(owner, counts[None, :], 0), axis=1)
    tile_valid = jnp.clip(owner_count - rows_before, 0, MOE_ROW_TILE)
    return dest, tile_expert, tile_valid, tile_end[-1:]


def _rope_tables(positions):
    inv_freq = ROPE_THETA ** (-jnp.arange(0, ROPE_DIM, 2, dtype=F32) / ROPE_DIM)
    ang = positions.astype(F32)[..., None] * inv_freq
    cos, sin = jnp.cos(ang), jnp.sin(ang)
    ones = jnp.ones(cos.shape[:-1] + (HEAD_DIM - ROPE_DIM,), F32)
    zeros = jnp.zeros_like(ones)
    zh = jnp.zeros_like(sin)
    c = jnp.concatenate([cos, cos, ones], axis=-1)
    s1 = jnp.concatenate([-sin, zh, zeros], axis=-1)
    s2 = jnp.concatenate([zh, sin, zeros], axis=-1)
    reps = LANES // HEAD_DIM
    return jnp.concatenate([jnp.tile(c, reps), jnp.tile(s1, reps), jnp.tile(s2, reps)], axis=-1)


def _band_bias_table(rel_bias, tq):
    n_heads = rel_bias.shape[0]
    win = tq + BAND_PAD
    row_len = win + tq
    n_high = BAND_PAD - MAX_REL + 1
    n_ramp = 2 * MAX_REL - 1
    high = rel_bias[:, 2 * MAX_REL:]
    profile = jnp.concatenate([
        jnp.broadcast_to(high, (n_heads, n_high)),
        rel_bias[:, n_ramp:0:-1],
        jnp.broadcast_to(rel_bias[:, :1], (n_heads, win - n_high - n_ramp)),
        jnp.broadcast_to(high, (n_heads, tq)),
    ], axis=1)
    skew = jnp.broadcast_to(profile[:, None, :], (n_heads, tq, row_len)).reshape(n_heads, tq * row_len)
    skew = skew[:, :tq * (row_len - 1)].reshape(n_heads, tq, row_len - 1)[:, :, :win]
    qi = np.arange(tq)[:, None]
    kj = np.arange(win)[None, :]
    q_chunk = qi // CHUNK
    k_chunk = kj // CHUNK - (BAND_CHUNKS - 1)
    in_band = (k_chunk <= q_chunk) & (k_chunk >= q_chunk - (BAND_CHUNKS - 1))
    return jnp.where(jnp.asarray(in_band)[None], skew, NEG_INF)


def kernel(x, c, positions, w_ada, b_ada, w_in, rel_bias, idx_k_norm_g, idx_k_norm_b, w_branch_a,
           w_branch_b, w_out, ln1_g, ln1_b, w_router, router_bias, w_exp_gate, w_exp_up, w_exp_down,
           w_sh_gate, w_sh_up, w_sh_down, ln2_g, ln2_b):
    bsz, s, d = x.shape
    depth = w_ada.shape[0]
    alpha = (2.0 * depth) ** 0.25
    tm_proj = min(512, s)
    tq_band = min(256, s)
    tq_dsa = min(256, s)
    tm_merge = min(512, s)
    tm_moe = min(256, s)

    rope_tab = _rope_tables(positions)
    n7 = 7 * WIDTH
    for l in range(depth):
        mod = _ada(c, w_ada[l], b_ada[l]).reshape(bsz, N_MOD, d)
        w_l = w_in[l]
        w7 = w_l[:, :n7].astype(BF16)
        n_kw = IDX_DIM + N_IDX_HEADS
        wkw = jnp.pad(w_l[:, n7:n7 + n_kw], ((0, 0), (0, LANES - n_kw))).astype(BF16)
        wg = w_l[:, n7 + n_kw:].astype(BF16)
        lnk = jnp.pad(jnp.stack([idx_k_norm_g[l], idx_k_norm_b[l]]), ((0, 0), (0, LANES - IDX_DIM)))
        qa, ka, va, qb, kb, vb, qi, ki, wi, ga, gb = _proj(x, mod, rope_tab, w7, wkw, wg, lnk, tm_proj)

        oa = _band(qa, ka, va, _band_bias_table(rel_bias[l], tq_band), tq_band)
        ob = _dsa(qi, wi, qb, ki, kb, vb, tq_dsa)

        wr = jnp.pad(w_router[l], ((0, 0), (0, LANES - N_EXPERTS)))
        wr_hi = wr.astype(BF16)
        wr_lo = (wr - wr_hi.astype(F32)).astype(BF16)
        x1, u2, gk, u2_tiles, expk, rankk, seen = _merge(
            oa, ob, ga, gb, x, mod,
            w_branch_a[l].astype(BF16), w_branch_b[l].astype(BF16), w_out[l].astype(BF16),
            jnp.stack([ln1_g[l], ln1_b[l]]), jnp.concatenate([wr_hi, wr_lo], axis=1),
            router_bias[l].reshape(N_EXPERTS, 1), tm_merge, alpha)

        t = bsz * s
        n_tiles = t * TOPK_EXPERTS // MOE_ROW_TILE + N_EXPERTS
        n_rows = n_tiles * MOE_ROW_TILE
        dest, tile_expert, tile_valid, tiles_used = _sorted_layout(
            expk, rankk, seen[:, 0].astype(jnp.int32), n_tiles)
        xs = _sc_dispatch(u2_tiles.reshape(t, TILE_ROWS, LANES), dest, n_rows)
        ys = _ffn_grouped(tile_expert, tile_valid, tiles_used, xs.reshape(n_rows * TILE_ROWS, LANES),
                          w_exp_gate[l], w_exp_up[l], w_exp_down[l], n_tiles)
        ys = ys.reshape(n_rows, TILE_ROWS, LANES)
        shared = (w_sh_gate[l].astype(BF16), w_sh_up[l].astype(BF16), w_sh_down[l].astype(BF16))
        ln2 = jnp.stack([ln2_g[l], ln2_b[l]])
        chunk = t // MOE_COMBINE_CHUNKS
        out = x1.reshape(t, d)
        for c in range(MOE_COMBINE_CHUNKS):
            idx = dest[:, c * chunk:(c + 1) * chunk].reshape(1, TOPK_EXPERTS * chunk)
            yg = _sc_gather(ys, idx).reshape(TOPK_EXPERTS, chunk * TILE_ROWS, LANES)
            out = _moe_final(yg, gk.reshape(t, LANES), u2.reshape(t, d), out, mod,
                             *shared, ln2, tm_moe, s, c * chunk, alpha)
        x = out.reshape(bsz, s, d)
    return x
```

```python
import functools

import jax
import jax.numpy as jnp
import numpy as np
from jax import lax
from jax.experimental import pallas as pl
from jax.experimental.pallas import tpu as pltpu
from jax.experimental.pallas import tpu_sc as plsc

F32 = jnp.float32
BF16 = jnp.bfloat16
NEG_INF = float("-inf")

CHUNK = 64
HEAD_DIM = 64
N_HEADS = 8
WIDTH = N_HEADS * HEAD_DIM
BAND_CHUNKS = 9
BAND_PAD = (BAND_CHUNKS - 1) * CHUNK
MAX_REL = 128
ROPE_THETA = 500000.0
ROPE_DIM = HEAD_DIM // 4
ROPE_HALF = ROPE_DIM // 2
N_IDX_HEADS = 8
IDX_DIM = 64
TOPK_MAX = 256
N_EXPERTS = 64
N_GROUPS = 8
GROUP_SIZE = N_EXPERTS // N_GROUPS
TOPK_GROUPS = 4
TOPK_EXPERTS = 8
ROUTED_SCALE = 1.0
N_MOD = 6
LN_EPS = 1e-5
LANES = 128
KEY_TILE = 256

VMEM_LIMIT = 56 * 1024 * 1024

DSA_KEY_CLASSES = 8
TILE_ROWS = 4
MOE_ROW_TILE = 1024
SC_WINDOW = 64
MOE_COMBINE_CHUNKS = 4
SC_CORES = 2
SC_SUBCORES = 16


def _cparams(sem):
    return pltpu.CompilerParams(dimension_semantics=sem, vmem_limit_bytes=VMEM_LIMIT)


def _ln(z):
    mu = jnp.mean(z, axis=-1, keepdims=True)
    zc = z - mu
    var = jnp.mean(zc * zc, axis=-1, keepdims=True)
    return zc * lax.rsqrt(var + LN_EPS)


def _sigmoid(z):
    return 1.0 / (1.0 + jnp.exp(-z))


def _dot(a, b):
    return jnp.dot(a, b, preferred_element_type=F32)


def _dot_nt(a, b):
    return lax.dot_general(a, b, (((1,), (1,)), ((), ())), preferred_element_type=F32)


def _split_bf16(z):
    hi = z.astype(BF16)
    lo = (z - hi.astype(F32)).astype(BF16)
    return hi, lo


HIGH_HALF = 0xFFFF0000


def _pack_pairs(z):
    half = z.shape[1] // 2

    def bits(v):
        return lax.bitcast_convert_type(v.astype(BF16).astype(F32), jnp.uint32)

    return (bits(z[:, :half]) >> 16) | (bits(z[:, half:]) & jnp.uint32(HIGH_HALF))


def _unpack_pairs(p):
    low = lax.bitcast_convert_type(p << 16, F32)
    high = lax.bitcast_convert_type(p & jnp.uint32(HIGH_HALF), F32)
    return jnp.concatenate([low, high], axis=1)


def _store_token_rows(ref, words, n_tok):
    for q in range(TILE_ROWS):
        ref[pl.ds(q, n_tok, stride=TILE_ROWS), :] = words[:, q * LANES:(q + 1) * LANES]


def _load_token_rows(ref, n_tok, lead=()):
    return jnp.concatenate([ref[lead + (pl.ds(q, n_tok, stride=TILE_ROWS), slice(None))]
                            for q in range(TILE_ROWS)], axis=1)


def _ada_kernel(c_ref, w_ref, b_ref, o_ref):
    c = c_ref[...]
    ca = c * _sigmoid(c)
    ca_hi, ca_lo = _split_bf16(ca)
    w = w_ref[...]
    w_hi, w_lo = _split_bf16(w)
    acc = _dot(ca_hi, w_hi) + _dot(ca_lo, w_hi) + _dot(ca_hi, w_lo)
    o_ref[...] = acc + b_ref[...]


def _ada(c, w_ada, b_ada):
    bsz, d = c.shape
    n = w_ada.shape[1]
    tn = 1024
    return pl.pallas_call(
        _ada_kernel,
        out_shape=jax.ShapeDtypeStruct((bsz, n), F32),
        grid=(n // tn,),
        in_specs=[
            pl.BlockSpec((bsz, d), lambda j: (0, 0)),
            pl.BlockSpec((d, tn), lambda j: (0, j)),
            pl.BlockSpec((1, tn), lambda j: (0, j)),
        ],
        out_specs=pl.BlockSpec((bsz, tn), lambda j: (0, j)),
        compiler_params=_cparams(("arbitrary",)),
        name="ada",
    )(c, w_ada, b_ada.reshape(1, n))


def _rope(z, c_t, s1_t, s2_t):
    n = z.shape[-1]
    return z * c_t + pltpu.roll(z, n - ROPE_HALF, 1) * s1_t + pltpu.roll(z, ROPE_HALF, 1) * s2_t


def _proj_kernel(x_ref, mod_ref, rope_ref, w7_ref, wkw_ref, wg_ref, lnk_ref,
                 qa_ref, ka_ref, va_ref, qb_ref, kb_ref, vb_ref, qi_ref, ki_ref, wi_ref,
                 ga_ref, gb_ref):
    x = x_ref[0]
    mod = mod_ref[0]
    sh1 = mod[0:1, :]
    sc1 = mod[1:2, :]
    u = (_ln(x) * (1.0 + sc1) + sh1).astype(BF16)

    rope = rope_ref[0]
    c1 = rope[:, 0:LANES]
    s1 = rope[:, LANES:2 * LANES]
    s2 = rope[:, 2 * LANES:3 * LANES]
    reps = WIDTH // LANES
    c_t = jnp.concatenate([c1] * reps, axis=1)
    s1_t = jnp.concatenate([s1] * reps, axis=1)
    s2_t = jnp.concatenate([s2] * reps, axis=1)

    att_scale = HEAD_DIM ** -0.5
    idx_scale = IDX_DIM ** -0.5

    def seg(k):
        return _dot(u, w7_ref[:, k * WIDTH:(k + 1) * WIDTH])

    qa_ref[0] = (seg(0) * att_scale).astype(BF16)
    ka_ref[0] = seg(1).astype(BF16)
    va_ref[0] = seg(2).astype(BF16)
    qb_ref[0] = (_rope(seg(3), c_t, s1_t, s2_t) * att_scale).astype(BF16)
    kb_ref[0] = _rope(seg(4), c_t, s1_t, s2_t).astype(BF16)
    vb_ref[0] = seg(5).astype(BF16)
    qi_ref[0] = (_rope(seg(6), c_t, s1_t, s2_t) * idx_scale).astype(BF16)

    z = _dot(u, wkw_ref[...])
    lane = lax.broadcasted_iota(jnp.int32, z.shape, 1)
    is_k = lane < IDX_DIM
    mu = jnp.sum(jnp.where(is_k, z, 0.0), axis=-1, keepdims=True) * (1.0 / IDX_DIM)
    zc = jnp.where(is_k, z - mu, 0.0)
    var = jnp.sum(zc * zc, axis=-1, keepdims=True) * (1.0 / IDX_DIM)
    lnk = lnk_ref[...]
    y = zc * lax.rsqrt(var + LN_EPS) * lnk[0:1, :] + lnk[1:2, :]
    y = _rope(y, c1, jnp.where(is_k, s1, 0.0), jnp.where(is_k, s2, 0.0))
    ki_ref[0] = y[:, 0:IDX_DIM].astype(BF16)
    wi_ref[0] = z.T[IDX_DIM:IDX_DIM + N_IDX_HEADS, :] * (N_IDX_HEADS ** -0.5)

    d = ga_ref.shape[-1]
    ga_ref[0] = _dot(u, wg_ref[:, 0:d]).astype(BF16)
    gb_ref[0] = _dot(u, wg_ref[:, d:2 * d]).astype(BF16)


def _proj(x, mod, rope_tab, w7, wkw, wg, lnk, tm):
    bsz, s, d = x.shape
    const = dict(pipeline_mode=pl.Buffered(1))
    row = lambda b, i: (b, i, 0)
    wspec = lambda shape: pl.BlockSpec(shape, lambda b, i: (0, 0), **const)
    out_w = jax.ShapeDtypeStruct((bsz, s, WIDTH), BF16)
    out_d = jax.ShapeDtypeStruct((bsz, s, d), BF16)
    return pl.pallas_call(
        _proj_kernel,
        out_shape=[out_w] * 7 + [
            jax.ShapeDtypeStruct((bsz, s, IDX_DIM), BF16),
            jax.ShapeDtypeStruct((bsz, N_IDX_HEADS, s), F32),
            out_d, out_d],
        grid=(bsz, s // tm),
        in_specs=[
            pl.BlockSpec((1, tm, d), row),
            pl.BlockSpec((1, N_MOD, d), lambda b, i: (b, 0, 0)),
            pl.BlockSpec((1, tm, 3 * LANES), row),
            wspec(w7.shape), wspec(wkw.shape), wspec(wg.shape), wspec(lnk.shape),
        ],
        out_specs=[pl.BlockSpec((1, tm, WIDTH), row)] * 7 + [
            pl.BlockSpec((1, tm, IDX_DIM), row),
            pl.BlockSpec((1, N_IDX_HEADS, tm), lambda b, i: (b, 0, i)),
            pl.BlockSpec((1, tm, d), row), pl.BlockSpec((1, tm, d), row)],
        compiler_params=_cparams(("arbitrary", "arbitrary")),
        name="proj",
    )(x, mod, rope_tab, w7, wkw, wg, lnk)


def _band_kernel(q_ref, k_ref, v_ref, bias_ref, o_ref, kpad, vpad, a_scr, p_scr, *, tq):
    i = pl.program_id(1)
    s = k_ref.shape[1]
    win = tq + BAND_PAD

    @pl.when(i == 0)
    def _():
        zeros = jnp.zeros((BAND_PAD, WIDTH), BF16)
        kpad[0:BAND_PAD, :] = zeros
        vpad[0:BAND_PAD, :] = zeros
        kpad[BAND_PAD:BAND_PAD + s, :] = k_ref[0]
        vpad[BAND_PAD:BAND_PAD + s, :] = v_ref[0]

    q0 = pl.multiple_of(i * tq, tq)
    n_lg = win // LANES
    col = lax.broadcasted_iota(jnp.int32, (tq, LANES), 1)
    first_frame = BAND_PAD - q0
    heads = [slice(h * HEAD_DIM, (h + 1) * HEAD_DIM) for h in range(N_HEADS)]

    def lanes(g):
        return slice(g * LANES, (g + 1) * LANES)

    def scores(mask_padding):
        row_max = []
        for h, sl in enumerate(heads):
            sc = _dot_nt(q_ref[0, :, sl], kpad[pl.ds(q0, win), sl])
            mx = jnp.full((tq, LANES), NEG_INF, F32)
            for g in range(n_lg):
                a = sc[:, lanes(g)] + bias_ref[h, :, lanes(g)]
                if mask_padding:
                    a = jnp.where(col >= first_frame - g * LANES, a, NEG_INF)
                a_scr[h, :, lanes(g)] = a
                mx = jnp.maximum(mx, a)
            row_max.append(jnp.max(mx, axis=-1, keepdims=True))
        return row_max

    row_max = lax.cond(first_frame > 0, lambda: scores(True), lambda: scores(False))
    row_sum = []
    for h in range(N_HEADS):
        m_b = jnp.broadcast_to(row_max[h], (tq, LANES))
        ls = jnp.zeros((tq, LANES), F32)
        for g in range(n_lg):
            p = jnp.exp(a_scr[h, :, lanes(g)] - m_b)
            ls = ls + p
            p_scr[h, :, lanes(g)] = p.astype(BF16)
        row_sum.append(jnp.sum(ls, axis=-1, keepdims=True))
    for h, sl in enumerate(heads):
        o = _dot(p_scr[h], vpad[pl.ds(q0, win), sl]) / row_sum[h]
        o_ref[0, :, sl] = o.astype(BF16)


def _band(qa, ka, va, bias_tab, tq):
    bsz, s, _ = qa.shape
    win = tq + BAND_PAD
    return pl.pallas_call(
        functools.partial(_band_kernel, tq=tq),
        out_shape=jax.ShapeDtypeStruct((bsz, s, WIDTH), BF16),
        grid=(bsz, s // tq),
        in_specs=[
            pl.BlockSpec((1, tq, WIDTH), lambda b, i: (b, i, 0)),
            pl.BlockSpec((1, s, WIDTH), lambda b, i: (b, 0, 0)),
            pl.BlockSpec((1, s, WIDTH), lambda b, i: (b, 0, 0)),
            pl.BlockSpec((N_HEADS, tq, win), lambda b, i: (0, 0, 0)),
        ],
        out_specs=pl.BlockSpec((1, tq, WIDTH), lambda b, i: (b, i, 0)),
        scratch_shapes=[pltpu.VMEM((BAND_PAD + s, WIDTH), BF16),
                        pltpu.VMEM((BAND_PAD + s, WIDTH), BF16),
                        pltpu.VMEM((N_HEADS, tq, win), F32),
                        pltpu.VMEM((N_HEADS, tq, win), BF16)],
        compiler_params=_cparams(("arbitrary", "arbitrary")),
        name="band",
    )(qa, ka, va, bias_tab)


BISECT_STEPS_PER_CHECK = 4
BISECT_MAX_CHECKS = 400
COUNT_ROWS = 64


def _dsa_kernel(qi_ref, wit_ref, qb_ref, ki_ref, kb_ref, vb_ref, o_ref, sct_scr, msk_scr, a_scr, p_scr,
                *, tq, k_sel, q_base):
    i = pl.program_id(1)
    sk = ki_ref.shape[1]
    n_kt = sk // KEY_TILE
    n_lg = sk // LANES
    q0 = q_base + i * tq
    kf = float(k_sel)

    def lanes(g):
        return slice(g * LANES, (g + 1) * LANES)

    def slab_reduce(fn, combine, init):
        acc = jnp.full((COUNT_ROWS, tq), init, F32)
        for r in range(sk // COUNT_ROWS):
            acc = combine(acc, fn(sct_scr[r * COUNT_ROWS:(r + 1) * COUNT_ROWS, :]))
        return acc

    def count(pred):
        part = slab_reduce(lambda t: jnp.where(pred(t), 1.0, 0.0), jnp.add, 0.0)
        return jnp.sum(part, axis=0, keepdims=True)

    w_t = wit_ref[0]
    t_pos = q0 + lax.broadcasted_iota(jnp.int32, (1, tq), 1)
    limit = (t_pos // CHUNK + 1) * CHUNK
    key_in_tile = lax.broadcasted_iota(jnp.int32, (KEY_TILE, tq), 0)
    for kt in range(n_kt):
        ks = slice(kt * KEY_TILE, (kt + 1) * KEY_TILE)
        ki_t = ki_ref[0, ks, :]
        acc = jnp.zeros((KEY_TILE, tq), F32)
        for h in range(N_IDX_HEADS):
            lg = _dot_nt(ki_t, qi_ref[0, :, h * IDX_DIM:(h + 1) * IDX_DIM])
            acc = acc + jnp.maximum(lg, 0.0) * w_t[h:h + 1, :]
        sct_scr[ks, :] = jnp.where(key_in_tile < limit - kt * KEY_TILE, acc, NEG_INF)

    smax = jnp.max(slab_reduce(lambda t: t, jnp.maximum, NEG_INF), axis=0, keepdims=True)
    smin = jnp.min(slab_reduce(lambda t: jnp.where(t == NEG_INF, jnp.inf, t), jnp.minimum, jnp.inf),
                   axis=0, keepdims=True)
    n_adm = limit.astype(F32)
    c_max = count(lambda t: t >= smax)
    c_pos = count(lambda t: t > 0.0)
    c_nn = count(lambda t: t >= 0.0)
    zero = jnp.zeros_like(smax)
    at_zero = (c_pos < kf) & (c_nn >= kf)
    below_zero = c_nn < kf
    lo = jnp.where(at_zero | ~below_zero, zero, smin)
    clo = jnp.where(at_zero | ~below_zero, c_nn, n_adm)
    hi = jnp.where(at_zero | below_zero, zero, smax)
    few = n_adm <= kf
    lo = jnp.where(few, smin, lo)
    clo = jnp.where(few, n_adm, clo)
    top_tied = (c_max >= kf) & ~few
    lo = jnp.where(top_tied, smax, lo)
    clo = jnp.where(top_tied, c_max, clo)
    hi = jnp.where(top_tied, smax, hi)
    done0 = jnp.where(few | at_zero | top_tied, 1.0, 0.0)

    def bisect(carry):
        lo, hi, clo, done, it = carry
        for _ in range(BISECT_STEPS_PER_CHECK):
            mid = 0.5 * lo + 0.5 * hi
            stuck = (mid <= lo) | (mid >= hi)
            c = count(lambda t: t >= mid)
            ge = c >= kf
            lo = jnp.where(ge, mid, lo)
            clo = jnp.where(ge, c, clo)
            hi = jnp.where(ge, hi, mid)
            done = jnp.where(stuck | (clo <= kf), 1.0, done)
        return lo, hi, clo, done, it + 1

    def not_converged(carry):
        _, _, _, done, it = carry
        return (jnp.min(done) < 0.5) & (it < BISECT_MAX_CHECKS)

    lo, hi, clo, _, _ = lax.while_loop(not_converged, bisect, (lo, hi, clo, done0, jnp.int32(0)))

    thr = lo
    c_gt = count(lambda t: t > thr)
    c_eq = count(lambda t: t == thr)
    need = kf - c_gt
    tie_cut = jnp.max(c_gt + c_eq - kf) > 0.0

    def with_ties():
        r_i = lax.broadcasted_iota(jnp.int32, (KEY_TILE, KEY_TILE), 0)
        c_i = lax.broadcasted_iota(jnp.int32, (KEY_TILE, KEY_TILE), 1)
        earlier = jnp.where(c_i < r_i, 1.0, 0.0).astype(BF16)
        carry = jnp.zeros((1, tq), F32)
        for kt in range(n_kt):
            ks = slice(kt * KEY_TILE, (kt + 1) * KEY_TILE)
            sc = sct_scr[ks, :]
            eq = sc == thr
            eq_f = jnp.where(eq, 1.0, 0.0)
            rank = _dot(earlier, eq_f.astype(BF16)) + carry
            keep = (sc > thr) | (eq & (rank < need))
            msk_scr[:, ks] = jnp.where(keep, 0.0, NEG_INF).T
            carry = carry + jnp.sum(eq_f, axis=0, keepdims=True)

    def without_ties():
        for kt in range(n_kt):
            ks = slice(kt * KEY_TILE, (kt + 1) * KEY_TILE)
            msk_scr[:, ks] = jnp.where(sct_scr[ks, :] >= thr, 0.0, NEG_INF).T

    lax.cond(tie_cut, with_ties, without_ties)

    heads = [slice(h * HEAD_DIM, (h + 1) * HEAD_DIM) for h in range(N_HEADS)]
    row_max = []
    for h, sl in enumerate(heads):
        qh = qb_ref[0, :, sl]
        mx = jnp.full((tq, LANES), NEG_INF, F32)
        for kt in range(n_kt):
            ks = slice(kt * KEY_TILE, (kt + 1) * KEY_TILE)
            a = _dot_nt(qh, kb_ref[0, ks, sl]) + msk_scr[:, ks]
            a_scr[h, :, ks] = a
            for g in range(KEY_TILE // LANES):
                mx = jnp.maximum(mx, a[:, lanes(g)])
        row_max.append(jnp.max(mx, axis=-1, keepdims=True))
    row_sum = []
    for h in range(N_HEADS):
        m_b = jnp.broadcast_to(row_max[h], (tq, LANES))
        ls = jnp.zeros((tq, LANES), F32)
        for g in range(n_lg):
            p = jnp.exp(a_scr[h, :, lanes(g)] - m_b)
            ls = ls + p
            p_scr[h, :, lanes(g)] = p.astype(BF16)
        row_sum.append(jnp.sum(ls, axis=-1, keepdims=True))
    for h, sl in enumerate(heads):
        o = _dot(p_scr[h], vb_ref[0, :, sl]) / row_sum[h]
        o_ref[0, :, sl] = o.astype(BF16)


def _dsa_class(qi, wit, qb, ki, kb, vb, tq, k_sel, q_base, q_len, sk):
    bsz = qb.shape[0]
    blk0 = q_base // tq
    row = lambda b, i: (b, blk0 + i, 0)
    keys = lambda b, i: (b, 0, 0)
    return pl.pallas_call(
        functools.partial(_dsa_kernel, tq=tq, k_sel=k_sel, q_base=q_base),
        out_shape=jax.ShapeDtypeStruct((bsz, q_len, WIDTH), BF16),
        grid=(bsz, q_len // tq),
        in_specs=[
            pl.BlockSpec((1, tq, N_IDX_HEADS * IDX_DIM), row),
            pl.BlockSpec((1, N_IDX_HEADS, tq), lambda b, i: (b, 0, blk0 + i)),
            pl.BlockSpec((1, tq, WIDTH), row),
            pl.BlockSpec((1, sk, IDX_DIM), keys),
            pl.BlockSpec((1, sk, WIDTH), keys),
            pl.BlockSpec((1, sk, WIDTH), keys),
        ],
        out_specs=pl.BlockSpec((1, tq, WIDTH), lambda b, i: (b, i, 0)),
        scratch_shapes=[pltpu.VMEM((sk, tq), F32), pltpu.VMEM((tq, sk), F32),
                        pltpu.VMEM((N_HEADS, tq, sk), F32), pltpu.VMEM((N_HEADS, tq, sk), BF16)],
        compiler_params=_cparams(("arbitrary", "arbitrary")),
        name=f"dsa_k{sk}",
    )(qi, wit, qb, ki, kb, vb)


def _dsa(qi, wit, qb, ki, kb, vb, tq):
    s = qb.shape[1]
    k_sel = min(TOPK_MAX, s // 4)
    n_cls = max(1, min(DSA_KEY_CLASSES, s // KEY_TILE))
    q_len = s // n_cls
    outs = [_dsa_class(qi, wit, qb, ki, kb, vb, tq, k_sel, c * q_len, q_len, (c + 1) * q_len)
            for c in range(n_cls)]
    return jnp.concatenate(outs, axis=1)


def _merge_kernel(oa_ref, ob_ref, ga_ref, gb_ref, x_ref, mod_ref, wba_ref, wbb_ref, wo_ref,
                  ln1_ref, wr_ref, rb_ref, x1_ref, u2_ref, gk_ref, u2t_ref, expk_ref, rankk_ref, seen_ref,
                  seen_scr, *, alpha):
    tm = x_ref.shape[1]
    ya = _dot(oa_ref[0], wba_ref[...])
    yb = _dot(ob_ref[0], wbb_ref[...])
    merged = _sigmoid(ga_ref[0].astype(F32)) * ya + _sigmoid(gb_ref[0].astype(F32)) * yb
    mix = _dot(merged.astype(BF16), wo_ref[...])
    mod = mod_ref[0]
    g1 = mod[2:3, :]
    sh2 = mod[3:4, :]
    sc2 = mod[4:5, :]
    ln1 = ln1_ref[...]
    x1 = _ln(alpha * x_ref[0] + g1 * mix) * ln1[0:1, :] + ln1[1:2, :]
    x1_ref[0] = x1
    u2 = _ln(x1) * (1.0 + sc2) + sh2
    u2_ref[0] = u2.astype(BF16)

    u_hi, u_lo = _split_bf16(u2)
    both = _dot(u_hi, wr_ref[...])
    logits = both[:, 0:LANES] + both[:, LANES:] + _dot(u_lo, wr_ref[:, 0:LANES])
    aff = _sigmoid(logits.T[0:N_EXPERTS, :])
    biased = aff + rb_ref[...]
    grp = biased.reshape(N_GROUPS, GROUP_SIZE, tm)
    sub = lax.broadcasted_iota(jnp.int32, grp.shape, 1)
    m1 = jnp.max(grp, axis=1, keepdims=True)
    first = jnp.min(jnp.where(grp == m1, sub, GROUP_SIZE), axis=1, keepdims=True)
    m2 = jnp.max(jnp.where(sub == first, NEG_INF, grp), axis=1, keepdims=True)
    gscore = (m1 + m2).reshape(N_GROUPS, tm)
    g_i = lax.broadcasted_iota(jnp.int32, (N_GROUPS, tm), 0)
    g_rank = jnp.zeros((N_GROUPS, tm), F32)
    for g in range(N_GROUPS):
        other = gscore[g:g + 1, :]
        beats = (other > gscore) | ((other == gscore) & (g < g_i))
        g_rank = g_rank + jnp.where(beats, 1.0, 0.0)
    g_keep = jnp.where(g_rank < TOPK_GROUPS, 1.0, 0.0).reshape(N_GROUPS, 1, tm)
    e_keep = jnp.broadcast_to(g_keep, (N_GROUPS, GROUP_SIZE, tm)).reshape(N_EXPERTS, tm)
    masked = jnp.where(e_keep > 0.5, biased, NEG_INF)
    e_i = lax.broadcasted_iota(jnp.int32, (N_EXPERTS, tm), 0)
    chosen = jnp.zeros((N_EXPERTS, tm), F32)
    picks = []
    for _ in range(TOPK_EXPERTS):
        best = jnp.max(masked, axis=0, keepdims=True)
        first = jnp.min(jnp.where(masked == best, e_i, N_EXPERTS), axis=0, keepdims=True)
        hit = e_i == first
        chosen = jnp.where(hit, 1.0, chosen)
        masked = jnp.where(hit, NEG_INF, masked)
        picks.append(first)
    top_aff = jnp.where(chosen > 0.5, aff, 0.0)
    comb_t = top_aff / jnp.sum(top_aff, axis=0, keepdims=True) * ROUTED_SCALE

    @pl.when((pl.program_id(0) == 0) & (pl.program_id(1) == 0))
    def _():
        seen_scr[...] = jnp.zeros_like(seen_scr)

    r_i = lax.broadcasted_iota(jnp.int32, (tm, tm), 0)
    c_i = lax.broadcasted_iota(jnp.int32, (tm, tm), 1)
    earlier = jnp.where(r_i < c_i, 1.0, 0.0).astype(BF16)
    arrival = _dot(chosen.astype(BF16), earlier) + seen_scr[:, 0:1]
    seen_scr[...] = seen_scr[...] + jnp.sum(chosen, axis=1, keepdims=True)
    seen_ref[...] = seen_scr[...]
    ranks, gates = [], []
    for first in picks:
        hit = e_i == first
        ranks.append(jnp.sum(jnp.where(hit, arrival, 0.0), axis=0, keepdims=True))
        gates.append(jnp.sum(jnp.where(hit, comb_t, 0.0), axis=0, keepdims=True))
    expk_ref[...] = jnp.concatenate(picks, axis=0)
    rankk_ref[...] = jnp.concatenate(ranks, axis=0).astype(jnp.int32)
    gate_rows = jnp.concatenate(gates + [jnp.zeros((LANES - TOPK_EXPERTS, tm), F32)], axis=0)
    gk_ref[0] = gate_rows.T
    _store_token_rows(u2t_ref, _pack_pairs(u2), tm)


def _merge(oa, ob, ga, gb, x, mod, wba, wbb, wo, ln1, wr, rb, tm, alpha):
    bsz, s, d = x.shape
    row = lambda b, i: (b, i, 0)
    per_batch = s // tm
    flat = lambda b, i: (b * per_batch + i, 0)
    cols = lambda b, i: (0, b * per_batch + i)
    w2 = lambda shape: pl.BlockSpec(shape, lambda b, i: (0,) * len(shape))
    return pl.pallas_call(
        functools.partial(_merge_kernel, alpha=alpha),
        out_shape=[jax.ShapeDtypeStruct((bsz, s, d), F32),
                   jax.ShapeDtypeStruct((bsz, s, d), BF16),
                   jax.ShapeDtypeStruct((bsz, s, LANES), F32),
                   jax.ShapeDtypeStruct((bsz * s * TILE_ROWS, LANES), jnp.uint32),
                   jax.ShapeDtypeStruct((TOPK_EXPERTS, bsz * s), jnp.int32),
                   jax.ShapeDtypeStruct((TOPK_EXPERTS, bsz * s), jnp.int32),
                   jax.ShapeDtypeStruct((N_EXPERTS, LANES), F32)],
        grid=(bsz, s // tm),
        in_specs=[
            pl.BlockSpec((1, tm, WIDTH), row), pl.BlockSpec((1, tm, WIDTH), row),
            pl.BlockSpec((1, tm, d), row), pl.BlockSpec((1, tm, d), row),
            pl.BlockSpec((1, tm, d), row),
            pl.BlockSpec((1, N_MOD, d), lambda b, i: (b, 0, 0)),
            w2(wba.shape), w2(wbb.shape), w2(wo.shape), w2(ln1.shape), w2(wr.shape), w2(rb.shape),
        ],
        out_specs=[pl.BlockSpec((1, tm, d), row), pl.BlockSpec((1, tm, d), row),
                   pl.BlockSpec((1, tm, LANES), row),
                   pl.BlockSpec((tm * TILE_ROWS, LANES), flat),
                   pl.BlockSpec((TOPK_EXPERTS, tm), cols),
                   pl.BlockSpec((TOPK_EXPERTS, tm), cols),
                   pl.BlockSpec((N_EXPERTS, LANES), lambda b, i: (0, 0))],
        scratch_shapes=[pltpu.VMEM((N_EXPERTS, LANES), F32)],
        compiler_params=_cparams(("arbitrary", "arbitrary")),
        name="merge",
    )(oa, ob, ga, gb, x, mod, wba, wbb, wo, ln1, wr, rb)


def _silu(z):
    return z * _sigmoid(z)


def _sc_params():
    return pltpu.CompilerParams(use_tc_tiling_on_sc=False)


def _sc_mesh():
    return plsc.VectorSubcoreMesh(core_axis_name="c", subcore_axis_name="s",
                                  num_cores=SC_CORES, num_subcores=SC_SUBCORES)


def _sc_dispatch(x_tiles, dest, n_rows):
    n_tok = x_tiles.shape[0]
    n_k = dest.shape[0]
    mesh = _sc_mesh()

    @pl.kernel(out_type=jax.ShapeDtypeStruct((n_rows,) + x_tiles.shape[1:], x_tiles.dtype), mesh=mesh,
               scratch_types=[], compiler_params=_sc_params())
    def scatter_rows(x_hbm, *rest):
        idx_hbm, o_hbm = rest[:n_k], rest[n_k]

        def body(x_vmem, *idx_vmem):
            for k in range(n_k):
                pltpu.sync_copy(x_vmem, o_hbm.at[idx_vmem[k].at[0]])

        pltpu.emit_pipeline(
            body, grid=(n_tok // SC_WINDOW,),
            in_specs=[pl.BlockSpec((SC_WINDOW,) + x_tiles.shape[1:], lambda i: (i, 0, 0))]
            + [pl.BlockSpec((1, SC_WINDOW), lambda i: (0, i))] * n_k,
            out_specs=[], core_axis_name=("c", "s"), dimension_semantics=(pltpu.PARALLEL,),
        )(x_hbm, *idx_hbm)

    return scatter_rows(x_tiles, *[dest[k:k + 1] for k in range(n_k)])


def _sc_gather(rows, idx):
    n = idx.shape[1]
    mesh = _sc_mesh()

    @pl.kernel(out_type=jax.ShapeDtypeStruct((n,) + rows.shape[1:], rows.dtype), mesh=mesh,
               scratch_types=[], compiler_params=_sc_params())
    def gather_rows(r_hbm, i_hbm, o_hbm):
        def body(i_vmem, o_vmem):
            pltpu.sync_copy(r_hbm.at[i_vmem.at[0]], o_vmem)

        pltpu.emit_pipeline(
            body, grid=(n // SC_WINDOW,),
            in_specs=[pl.BlockSpec((1, SC_WINDOW), lambda i: (0, i))],
            out_specs=[pl.BlockSpec((SC_WINDOW,) + rows.shape[1:], lambda i: (i, 0, 0))],
            core_axis_name=("c", "s"), dimension_semantics=(pltpu.PARALLEL,),
        )(i_hbm, o_hbm)

    return gather_rows(rows, idx)


def _ffn_grouped_kernel(expert_ref, valid_ref, used_ref, x_ref, wg_ref, wu_ref, wd_ref, y_ref,
                        wg_bf, wu_bf, wd_bf):
    i = pl.program_id(0)
    n_rows = x_ref.shape[0] // TILE_ROWS
    in_use = i < used_ref[0]
    new_expert = (i == 0) | (expert_ref[i] != expert_ref[jnp.maximum(i - 1, 0)])

    @pl.when(in_use & new_expert)
    def _():
        wg_bf[...] = wg_ref[0].astype(BF16)
        wu_bf[...] = wu_ref[0].astype(BF16)
        wd_bf[...] = wd_ref[0].astype(BF16)

    @pl.when(in_use)
    def _():
        row = lax.broadcasted_iota(jnp.int32, (n_rows, 1), 0)
        words = jnp.where(row < valid_ref[i], _load_token_rows(x_ref, n_rows), jnp.uint32(0))
        x = _unpack_pairs(words).astype(BF16)
        h = _silu(_dot(x, wg_bf[...])) * _dot(x, wu_bf[...])
        y = _dot(h.astype(BF16), wd_bf[...])
        _store_token_rows(y_ref, _pack_pairs(y), n_rows)


def _ffn_grouped(tile_expert, tile_valid, tiles_used, xs, wg, wu, wd, n_tiles):
    _, d, d_exp = wg.shape
    blk = pl.BlockSpec((MOE_ROW_TILE * TILE_ROWS, LANES), lambda i, te, tv, nu: (i, 0))
    weight = lambda shape: pl.BlockSpec(shape, lambda i, te, tv, nu: (te[i], 0, 0))
    return pl.pallas_call(
        _ffn_grouped_kernel,
        out_shape=jax.ShapeDtypeStruct(xs.shape, xs.dtype),
        grid_spec=pltpu.PrefetchScalarGridSpec(
            num_scalar_prefetch=3, grid=(n_tiles,),
            in_specs=[blk, weight((1, d, d_exp)), weight((1, d, d_exp)), weight((1, d_exp, d))],
            out_specs=blk,
            scratch_shapes=[pltpu.VMEM((d, d_exp), BF16), pltpu.VMEM((d, d_exp), BF16),
                            pltpu.VMEM((d_exp, d), BF16)]),
        compiler_params=_cparams(("arbitrary",)),
        name="moe_ffn",
    )(tile_expert, tile_valid, tiles_used, xs, wg, wu, wd)


def _moe_final_kernel(yg_ref, gk_ref, u2_ref, x1_ref, mod_ref, wsg_ref, wsu_ref, wsd_ref, ln2_ref, o_ref,
                      *, alpha):
    tm = x1_ref.shape[0]
    t = u2_ref[...]
    ffn = _dot((_silu(_dot(t, wsg_ref[...])) * _dot(t, wsu_ref[...])).astype(BF16), wsd_ref[...])
    gk = gk_ref[...]
    for k in range(TOPK_EXPERTS):
        ffn = ffn + _unpack_pairs(_load_token_rows(yg_ref, tm, (k,))) * gk[:, k:k + 1]
    mod = mod_ref[0]
    g2 = mod[5:6, :]
    ln2 = ln2_ref[...]
    o_ref[...] = _ln(alpha * x1_ref[...] + g2 * ffn) * ln2[0:1, :] + ln2[1:2, :]


def _moe_final(yg, gk, u2, x1, mod, wsg, wsu, wsd, ln2, tm, tokens_per_batch, tok0, alpha):
    t, d = u2.shape
    n_tok = yg.shape[1] // TILE_ROWS
    blocks_per_batch = tokens_per_batch // tm
    blk0 = tok0 // tm
    row = lambda i: (blk0 + i, 0)
    w2 = lambda shape: pl.BlockSpec(shape, lambda i: (0,) * len(shape))
    return pl.pallas_call(
        functools.partial(_moe_final_kernel, alpha=alpha),
        out_shape=jax.ShapeDtypeStruct((t, d), F32),
        grid=(n_tok // tm,),
        in_specs=[
            pl.BlockSpec((TOPK_EXPERTS, tm * TILE_ROWS, LANES), lambda i: (0, i, 0)),
            pl.BlockSpec((tm, LANES), row),
            pl.BlockSpec((tm, d), row),
            pl.BlockSpec((tm, d), row),
            pl.BlockSpec((1, N_MOD, d), lambda i: ((blk0 + i) // blocks_per_batch, 0, 0)),
            w2(wsg.shape), w2(wsu.shape), w2(wsd.shape), w2(ln2.shape),
        ],
        out_specs=pl.BlockSpec((tm, d), row),
        input_output_aliases={3: 0},
        compiler_params=_cparams(("arbitrary",)),
        name="moe_final",
    )(yg, gk, u2, x1, mod, wsg, wsu, wsd, ln2)


def _sorted_layout(expk, rankk, counts, n_tiles):
    e_ids = jnp.arange(N_EXPERTS, dtype=jnp.int32)[:, None, None]
    is_e = expk[None] == e_ids
    tiles = (counts + MOE_ROW_TILE - 1) // MOE_ROW_TILE
    tile_end = jnp.cumsum(tiles)
    first_row = (tile_end - tiles) * MOE_ROW_TILE
    dest = jnp.sum(jnp.where(is_e, first_row[:, None, None], 0), axis=0, dtype=jnp.int32) + rankk
    tile_ids = jnp.arange(n_tiles, dtype=jnp.int32)[:, None]
    done = tile_ids >= tile_end[None, :]
    owner = ~done & (tile_ids >= (tile_end - tiles)[None, :])
    tile_expert = jnp.minimum(jnp.sum(done, axis=1, dtype=jnp.int32), N_EXPERTS - 1)
    rows_before = (tile_ids[:, 0] - jnp.sum(jnp.where(done, tiles[None, :], 0), axis=1)) * MOE_ROW_TILE
    owner_count = jnp.sum(jnp.where(owner, counts[None, :], 0), axis=1)
    tile_valid = jnp.clip(owner_count - rows_before, 0, MOE_ROW_TILE)
    return dest, tile_expert, tile_valid, tile_end[-1:]


def _rope_tables(positions):
    inv_freq = ROPE_THETA ** (-jnp.arange(0, ROPE_DIM, 2, dtype=F32) / ROPE_DIM)
    ang = positions.astype(F32)[..., None] * inv_freq
    cos, sin = jnp.cos(ang), jnp.sin(ang)
    ones = jnp.ones(cos.shape[:-1] + (HEAD_DIM - ROPE_DIM,), F32)
    zeros = jnp.zeros_like(ones)
    zh = jnp.zeros_like(sin)
    c = jnp.concatenate([cos, cos, ones], axis=-1)
    s1 = jnp.concatenate([-sin, zh, zeros], axis=-1)
    s2 = jnp.concatenate([zh, sin, zeros], axis=-1)
    reps = LANES // HEAD_DIM
    return jnp.concatenate([jnp.tile(c, reps), jnp.tile(s1, reps), jnp.tile(s2, reps)], axis=-1)


def _band_bias_table(rel_bias, tq):
    n_heads = rel_bias.shape[0]
    win = tq + BAND_PAD
    row_len = win + tq
    n_high = BAND_PAD - MAX_REL + 1
    n_ramp = 2 * MAX_REL - 1
    high = rel_bias[:, 2 * MAX_REL:]
    profile = jnp.concatenate([
        jnp.broadcast_to(high, (n_heads, n_high)),
        rel_bias[:, n_ramp:0:-1],
        jnp.broadcast_to(rel_bias[:, :1], (n_heads, win - n_high - n_ramp)),
        jnp.broadcast_to(high, (n_heads, tq)),
    ], axis=1)
    skew = jnp.broadcast_to(profile[:, None, :], (n_heads, tq, row_len)).reshape(n_heads, tq * row_len)
    skew = skew[:, :tq * (row_len - 1)].reshape(n_heads, tq, row_len - 1)[:, :, :win]
    qi = np.arange(tq)[:, None]
    kj = np.arange(win)[None, :]
    q_chunk = qi // CHUNK
    k_chunk = kj // CHUNK - (BAND_CHUNKS - 1)
    in_band = (k_chunk <= q_chunk) & (k_chunk >= q_chunk - (BAND_CHUNKS - 1))
    return jnp.where(jnp.asarray(in_band)[None], skew, NEG_INF)


def kernel(x, c, positions, w_ada, b_ada, w_in, rel_bias, idx_k_norm_g, idx_k_norm_b, w_branch_a,
           w_branch_b, w_out, ln1_g, ln1_b, w_router, router_bias, w_exp_gate, w_exp_up, w_exp_down,
           w_sh_gate, w_sh_up, w_sh_down, ln2_g, ln2_b):
    bsz, s, d = x.shape
    depth = w_ada.shape[0]
    alpha = (2.0 * depth) ** 0.25
    tm_proj = min(512, s)
    tq_band = min(256, s)
    tq_dsa = min(256, s)
    tm_merge = min(512, s)
    tm_moe = min(256, s)

    rope_tab = _rope_tables(positions)
    n7 = 7 * WIDTH
    for l in range(depth):
        mod = _ada(c, w_ada[l], b_ada[l]).reshape(bsz, N_MOD, d)
        w_l = w_in[l]
        w7 = w_l[:, :n7].astype(BF16)
        n_kw = IDX_DIM + N_IDX_HEADS
        wkw = jnp.pad(w_l[:, n7:n7 + n_kw], ((0, 0), (0, LANES - n_kw))).astype(BF16)
        wg = w_l[:, n7 + n_kw:].astype(BF16)
        lnk = jnp.pad(jnp.stack([idx_k_norm_g[l], idx_k_norm_b[l]]), ((0, 0), (0, LANES - IDX_DIM)))
        qa, ka, va, qb, kb, vb, qi, ki, wi, ga, gb = _proj(x, mod, rope_tab, w7, wkw, wg, lnk, tm_proj)

        oa = _band(qa, ka, va, _band_bias_table(rel_bias[l], tq_band), tq_band)
        ob = _dsa(qi, wi, qb, ki, kb, vb, tq_dsa)

        wr = jnp.pad(w_router[l], ((0, 0), (0, LANES - N_EXPERTS)))
        wr_hi = wr.astype(BF16)
        wr_lo = (wr - wr_hi.astype(F32)).astype(BF16)
        x1, u2, gk, u2_tiles, expk, rankk, seen = _merge(
            oa, ob, ga, gb, x, mod,
            w_branch_a[l].astype(BF16), w_branch_b[l].astype(BF16), w_out[l].astype(BF16),
            jnp.stack([ln1_g[l], ln1_b[l]]), jnp.concatenate([wr_hi, wr_lo], axis=1),
            router_bias[l].reshape(N_EXPERTS, 1), tm_merge, alpha)

        t = bsz * s
        n_tiles = t * TOPK_EXPERTS // MOE_ROW_TILE + N_EXPERTS
        n_rows = n_tiles * MOE_ROW_TILE
        dest, tile_expert, tile_valid, tiles_used = _sorted_layout(
            expk, rankk, seen[:, 0].astype(jnp.int32), n_tiles)
        xs = _sc_dispatch(u2_tiles.reshape(t, TILE_ROWS, LANES), dest, n_rows)
        ys = _ffn_grouped(tile_expert, tile_valid, tiles_used, xs.reshape(n_rows * TILE_ROWS, LANES),
                          w_exp_gate[l], w_exp_up[l], w_exp_down[l], n_tiles)
        ys = ys.reshape(n_rows, TILE_ROWS, LANES)
        shared = (w_sh_gate[l].astype(BF16), w_sh_up[l].astype(BF16), w_sh_down[l].astype(BF16))
        ln2 = jnp.stack([ln2_g[l], ln2_b[l]])
        chunk = t // MOE_COMBINE_CHUNKS
        out = x1.reshape(t, d)
        for c in range(MOE_COMBINE_CHUNKS):
            idx = dest[:, c * chunk:(c + 1) * chunk].reshape(1, TOPK_EXPERTS * chunk)
            yg = _sc_gather(ys, idx).reshape(TOPK_EXPERTS, chunk * TILE_ROWS, LANES)
            out = _moe_final(yg, gk.reshape(t, LANES), u2.reshape(t, d), out, mod,
                             *shared, ln2, tm_moe, s, c * chunk, alpha)
        x = out.reshape(bsz, s, d)
    return x
```

```python
import functools

import jax
import jax.numpy as jnp
import numpy as np
from jax import lax
from jax.experimental import pallas as pl
from jax.experimental.pallas import tpu as pltpu
from jax.experimental.pallas import tpu_sc as plsc

F32 = jnp.float32
BF16 = jnp.bfloat16
NEG_INF = float("-inf")

CHUNK = 64
HEAD_DIM = 64
N_HEADS = 8
WIDTH = N_HEADS * HEAD_DIM
BAND_CHUNKS = 9
BAND_PAD = (BAND_CHUNKS - 1) * CHUNK
MAX_REL = 128
ROPE_THETA = 500000.0
ROPE_DIM = HEAD_DIM // 4
ROPE_HALF = ROPE_DIM // 2
N_IDX_HEADS = 8
IDX_DIM = 64
TOPK_MAX = 256
N_EXPERTS = 64
N_GROUPS = 8
GROUP_SIZE = N_EXPERTS // N_GROUPS
TOPK_GROUPS = 4
TOPK_EXPERTS = 8
ROUTED_SCALE = 1.0
N_MOD = 6
LN_EPS = 1e-5
LANES = 128
KEY_TILE = 256

VMEM_LIMIT = 56 * 1024 * 1024

DSA_KEY_CLASSES = 8
TILE_ROWS = 4
MOE_ROW_TILE = 1024
SC_WINDOW = 64
MOE_COMBINE_CHUNKS = 4
SC_CORES = 2
SC_SUBCORES = 16


def _cparams(sem):
    return pltpu.CompilerParams(dimension_semantics=sem, vmem_limit_bytes=VMEM_LIMIT)


def _ln(z):
    mu = jnp.mean(z, axis=-1, keepdims=True)
    zc = z - mu
    var = jnp.mean(zc * zc, axis=-1, keepdims=True)
    return zc * lax.rsqrt(var + LN_EPS)


def _sigmoid(z):
    return 1.0 / (1.0 + jnp.exp(-z))


def _dot(a, b):
    return jnp.dot(a, b, preferred_element_type=F32)


def _dot_nt(a, b):
    return lax.dot_general(a, b, (((1,), (1,)), ((), ())), preferred_element_type=F32)


def _split_bf16(z):
    hi = z.astype(BF16)
    lo = (z - hi.astype(F32)).astype(BF16)
    return hi, lo


HIGH_HALF = 0xFFFF0000


def _pack_pairs(z):
    half = z.shape[1] // 2

    def bits(v):
        return lax.bitcast_convert_type(v.astype(BF16).astype(F32), jnp.uint32)

    return (bits(z[:, :half]) >> 16) | (bits(z[:, half:]) & jnp.uint32(HIGH_HALF))


def _unpack_pairs(p):
    low = lax.bitcast_convert_type(p << 16, F32)
    high = lax.bitcast_convert_type(p & jnp.uint32(HIGH_HALF), F32)
    return jnp.concatenate([low, high], axis=1)


def _store_token_rows(ref, words, n_tok):
    for q in range(TILE_ROWS):
        ref[pl.ds(q, n_tok, stride=TILE_ROWS), :] = words[:, q * LANES:(q + 1) * LANES]


def _load_token_rows(ref, n_tok, lead=()):
    return jnp.concatenate([ref[lead + (pl.ds(q, n_tok, stride=TILE_ROWS), slice(None))]
                            for q in range(TILE_ROWS)], axis=1)


def _ada_kernel(c_ref, w_ref, b_ref, o_ref):
    c = c_ref[...]
    ca = c * _sigmoid(c)
    ca_hi, ca_lo = _split_bf16(ca)
    w = w_ref[...]
    w_hi, w_lo = _split_bf16(w)
    acc = _dot(ca_hi, w_hi) + _dot(ca_lo, w_hi) + _dot(ca_hi, w_lo)
    o_ref[...] = acc + b_ref[...]


def _ada(c, w_ada, b_ada):
    bsz, d = c.shape
    n = w_ada.shape[1]
    tn = 1024
    return pl.pallas_call(
        _ada_kernel,
        out_shape=jax.ShapeDtypeStruct((bsz, n), F32),
        grid=(n // tn,),
        in_specs=[
            pl.BlockSpec((bsz, d), lambda j: (0, 0)),
            pl.BlockSpec((d, tn), lambda j: (0, j)),
            pl.BlockSpec((1, tn), lambda j: (0, j)),
        ],
        out_specs=pl.BlockSpec((bsz, tn), lambda j: (0, j)),
        compiler_params=_cparams(("arbitrary",)),
        name="ada",
    )(c, w_ada, b_ada.reshape(1, n))


def _rope(z, c_t, s1_t, s2_t):
    n = z.shape[-1]
    return z * c_t + pltpu.roll(z, n - ROPE_HALF, 1) * s1_t + pltpu.roll(z, ROPE_HALF, 1) * s2_t


def _proj_kernel(x_ref, mod_ref, rope_ref, w7_ref, wkw_ref, wg_ref, lnk_ref,
                 qa_ref, ka_ref, va_ref, qb_ref, kb_ref, vb_ref, qi_ref, ki_ref, wi_ref,
                 ga_ref, gb_ref):
    x = x_ref[0]
    mod = mod_ref[0]
    sh1 = mod[0:1, :]
    sc1 = mod[1:2, :]
    u = (_ln(x) * (1.0 + sc1) + sh1).astype(BF16)

    rope = rope_ref[0]
    c1 = rope[:, 0:LANES]
    s1 = rope[:, LANES:2 * LANES]
    s2 = rope[:, 2 * LANES:3 * LANES]
    reps = WIDTH // LANES
    c_t = jnp.concatenate([c1] * reps, axis=1)
    s1_t = jnp.concatenate([s1] * reps, axis=1)
    s2_t = jnp.concatenate([s2] * reps, axis=1)

    att_scale = HEAD_DIM ** -0.5
    idx_scale = IDX_DIM ** -0.5

    def seg(k):
        return _dot(u, w7_ref[:, k * WIDTH:(k + 1) * WIDTH])

    qa_ref[0] = (seg(0) * att_scale).astype(BF16)
    ka_ref[0] = seg(1).astype(BF16)
    va_ref[0] = seg(2).astype(BF16)
    qb_ref[0] = (_rope(seg(3), c_t, s1_t, s2_t) * att_scale).astype(BF16)
    kb_ref[0] = _rope(seg(4), c_t, s1_t, s2_t).astype(BF16)
    vb_ref[0] = seg(5).astype(BF16)
    qi_ref[0] = (_rope(seg(6), c_t, s1_t, s2_t) * idx_scale).astype(BF16)

    z = _dot(u, wkw_ref[...])
    lane = lax.broadcasted_iota(jnp.int32, z.shape, 1)
    is_k = lane < IDX_DIM
    mu = jnp.sum(jnp.where(is_k, z, 0.0), axis=-1, keepdims=True) * (1.0 / IDX_DIM)
    zc = jnp.where(is_k, z - mu, 0.0)
    var = jnp.sum(zc * zc, axis=-1, keepdims=True) * (1.0 / IDX_DIM)
    lnk = lnk_ref[...]
    y = zc * lax.rsqrt(var + LN_EPS) * lnk[0:1, :] + lnk[1:2, :]
    y = _rope(y, c1, jnp.where(is_k, s1, 0.0), jnp.where(is_k, s2, 0.0))
    ki_ref[0] = y[:, 0:IDX_DIM].astype(BF16)
    wi_ref[0] = z.T[IDX_DIM:IDX_DIM + N_IDX_HEADS, :] * (N_IDX_HEADS ** -0.5)

    d = ga_ref.shape[-1]
    ga_ref[0] = _dot(u, wg_ref[:, 0:d]).astype(BF16)
    gb_ref[0] = _dot(u, wg_ref[:, d:2 * d]).astype(BF16)


def _proj(x, mod, rope_tab, w7, wkw, wg, lnk, tm):
    bsz, s, d = x.shape
    const = dict(pipeline_mode=pl.Buffered(1))
    row = lambda b, i: (b, i, 0)
    wspec = lambda shape: pl.BlockSpec(shape, lambda b, i: (0, 0), **const)
    out_w = jax.ShapeDtypeStruct((bsz, s, WIDTH), BF16)
    out_d = jax.ShapeDtypeStruct((bsz, s, d), BF16)
    return pl.pallas_call(
        _proj_kernel,
        out_shape=[out_w] * 7 + [
            jax.ShapeDtypeStruct((bsz, s, IDX_DIM), BF16),
            jax.ShapeDtypeStruct((bsz, N_IDX_HEADS, s), F32),
            out_d, out_d],
        grid=(bsz, s // tm),
        in_specs=[
            pl.BlockSpec((1, tm, d), row),
            pl.BlockSpec((1, N_MOD, d), lambda b, i: (b, 0, 0)),
            pl.BlockSpec((1, tm, 3 * LANES), row),
            wspec(w7.shape), wspec(wkw.shape), wspec(wg.shape), wspec(lnk.shape),
        ],
        out_specs=[pl.BlockSpec((1, tm, WIDTH), row)] * 7 + [
            pl.BlockSpec((1, tm, IDX_DIM), row),
            pl.BlockSpec((1, N_IDX_HEADS, tm), lambda b, i: (b, 0, i)),
            pl.BlockSpec((1, tm, d), row), pl.BlockSpec((1, tm, d), row)],
        compiler_params=_cparams(("arbitrary", "arbitrary")),
        name="proj",
    )(x, mod, rope_tab, w7, wkw, wg, lnk)


def _band_kernel(q_ref, k_ref, v_ref, prof_ref, o_ref, kpad, vpad, a_scr, p_scr, bias_ref, *, tq):
    i = pl.program_id(1)
    s = k_ref.shape[1]
    win = tq + BAND_PAD

    @pl.when((pl.program_id(0) == 0) & (i == 0))
    def _():
        q_chunk = lax.broadcasted_iota(jnp.int32, (tq, win), 0) // CHUNK
        k_chunk = lax.broadcasted_iota(jnp.int32, (tq, win), 1) // CHUNK - (BAND_CHUNKS - 1)
        in_band = (k_chunk <= q_chunk) & (k_chunk >= q_chunk - (BAND_CHUNKS - 1))
        for h in range(N_HEADS):
            rows = jnp.broadcast_to(prof_ref[h:h + 1, :], (tq, prof_ref.shape[1]))
            skew = pltpu.roll(rows, 0, 1, stride=1, stride_axis=0)
            bias_ref[h] = jnp.where(in_band, skew[:, 0:win], NEG_INF)

    @pl.when(i == 0)
    def _():
        zeros = jnp.zeros((BAND_PAD, WIDTH), BF16)
        kpad[0:BAND_PAD, :] = zeros
        vpad[0:BAND_PAD, :] = zeros
        kpad[BAND_PAD:BAND_PAD + s, :] = k_ref[0]
        vpad[BAND_PAD:BAND_PAD + s, :] = v_ref[0]

    q0 = pl.multiple_of(i * tq, tq)
    n_lg = win // LANES
    col = lax.broadcasted_iota(jnp.int32, (tq, LANES), 1)
    first_frame = BAND_PAD - q0
    heads = [slice(h * HEAD_DIM, (h + 1) * HEAD_DIM) for h in range(N_HEADS)]

    def lanes(g):
        return slice(g * LANES, (g + 1) * LANES)

    def scores(mask_padding):
        row_max = []
        for h, sl in enumerate(heads):
            sc = _dot_nt(q_ref[0, :, sl], kpad[pl.ds(q0, win), sl])
            mx = jnp.full((tq, LANES), NEG_INF, F32)
            for g in range(n_lg):
                a = sc[:, lanes(g)] + bias_ref[h, :, lanes(g)]
                if mask_padding:
                    a = jnp.where(col >= first_frame - g * LANES, a, NEG_INF)
                a_scr[h, :, lanes(g)] = a
                mx = jnp.maximum(mx, a)
            row_max.append(jnp.max(mx, axis=-1, keepdims=True))
        return row_max

    row_max = lax.cond(first_frame > 0, lambda: scores(True), lambda: scores(False))
    row_sum = []
    for h in range(N_HEADS):
        m_b = jnp.broadcast_to(row_max[h], (tq, LANES))
        ls = jnp.zeros((tq, LANES), F32)
        for g in range(n_lg):
            p = jnp.exp(a_scr[h, :, lanes(g)] - m_b)
            ls = ls + p
            p_scr[h, :, lanes(g)] = p.astype(BF16)
        row_sum.append(jnp.sum(ls, axis=-1, keepdims=True))
    for h, sl in enumerate(heads):
        o = _dot(p_scr[h], vpad[pl.ds(q0, win), sl]) / row_sum[h]
        o_ref[0, :, sl] = o.astype(BF16)


def _band(qa, ka, va, profile, tq):
    bsz, s, _ = qa.shape
    win = tq + BAND_PAD
    return pl.pallas_call(
        functools.partial(_band_kernel, tq=tq),
        out_shape=jax.ShapeDtypeStruct((bsz, s, WIDTH), BF16),
        grid=(bsz, s // tq),
        in_specs=[
            pl.BlockSpec((1, tq, WIDTH), lambda b, i: (b, i, 0)),
            pl.BlockSpec((1, s, WIDTH), lambda b, i: (b, 0, 0)),
            pl.BlockSpec((1, s, WIDTH), lambda b, i: (b, 0, 0)),
            pl.BlockSpec(profile.shape, lambda b, i: (0, 0)),
        ],
        out_specs=pl.BlockSpec((1, tq, WIDTH), lambda b, i: (b, i, 0)),
        scratch_shapes=[pltpu.VMEM((BAND_PAD + s, WIDTH), BF16),
                        pltpu.VMEM((BAND_PAD + s, WIDTH), BF16),
                        pltpu.VMEM((N_HEADS, tq, win), F32),
                        pltpu.VMEM((N_HEADS, tq, win), BF16),
                        pltpu.VMEM((N_HEADS, tq, win), F32)],
        compiler_params=_cparams(("arbitrary", "arbitrary")),
        name="band",
    )(qa, ka, va, profile)


BISECT_STEPS_PER_CHECK = 4
BISECT_MAX_CHECKS = 400
COUNT_ROWS = 64


def _dsa_kernel(qi_ref, wit_ref, qb_ref, ki_ref, kb_ref, vb_ref, o_ref, sct_scr, msk_scr, a_scr, p_scr,
                *, tq, k_sel, q_base):
    i = pl.program_id(1)
    sk = ki_ref.shape[1]
    n_kt = sk // KEY_TILE
    n_lg = sk // LANES
    q0 = q_base + i * tq
    kf = float(k_sel)

    def lanes(g):
        return slice(g * LANES, (g + 1) * LANES)

    def slab_reduce(fn, combine, init):
        acc = jnp.full((COUNT_ROWS, tq), init, F32)
        for r in range(sk // COUNT_ROWS):
            acc = combine(acc, fn(sct_scr[r * COUNT_ROWS:(r + 1) * COUNT_ROWS, :]))
        return acc

    def count(pred):
        part = slab_reduce(lambda t: jnp.where(pred(t), 1.0, 0.0), jnp.add, 0.0)
        return jnp.sum(part, axis=0, keepdims=True)

    w_t = wit_ref[0]
    t_pos = q0 + lax.broadcasted_iota(jnp.int32, (1, tq), 1)
    limit = (t_pos // CHUNK + 1) * CHUNK
    key_in_tile = lax.broadcasted_iota(jnp.int32, (KEY_TILE, tq), 0)
    for kt in range(n_kt):
        ks = slice(kt * KEY_TILE, (kt + 1) * KEY_TILE)
        ki_t = ki_ref[0, ks, :]
        acc = jnp.zeros((KEY_TILE, tq), F32)
        for h in range(N_IDX_HEADS):
            lg = _dot_nt(ki_t, qi_ref[0, :, h * IDX_DIM:(h + 1) * IDX_DIM])
            acc = acc + jnp.maximum(lg, 0.0) * w_t[h:h + 1, :]
        sct_scr[ks, :] = jnp.where(key_in_tile < limit - kt * KEY_TILE, acc, NEG_INF)

    smax = jnp.max(slab_reduce(lambda t: t, jnp.maximum, NEG_INF), axis=0, keepdims=True)
    smin = jnp.min(slab_reduce(lambda t: jnp.where(t == NEG_INF, jnp.inf, t), jnp.minimum, jnp.inf),
                   axis=0, keepdims=True)
    n_adm = limit.astype(F32)
    c_max = count(lambda t: t >= smax)
    c_pos = count(lambda t: t > 0.0)
    c_nn = count(lambda t: t >= 0.0)
    zero = jnp.zeros_like(smax)
    at_zero = (c_pos < kf) & (c_nn >= kf)
    below_zero = c_nn < kf
    lo = jnp.where(at_zero | ~below_zero, zero, smin)
    clo = jnp.where(at_zero | ~below_zero, c_nn, n_adm)
    hi = jnp.where(at_zero | below_zero, zero, smax)
    few = n_adm <= kf
    lo = jnp.where(few, smin, lo)
    clo = jnp.where(few, n_adm, clo)
    top_tied = (c_max >= kf) & ~few
    lo = jnp.where(top_tied, smax, lo)
    clo = jnp.where(top_tied, c_max, clo)
    hi = jnp.where(top_tied, smax, hi)
    done0 = jnp.where(few | at_zero | top_tied, 1.0, 0.0)

    def bisect(carry):
        lo, hi, clo, done, it = carry
        for _ in range(BISECT_STEPS_PER_CHECK):
            mid = 0.5 * lo + 0.5 * hi
            stuck = (mid <= lo) | (mid >= hi)
            c = count(lambda t: t >= mid)
            ge = c >= kf
            lo = jnp.where(ge, mid, lo)
            clo = jnp.where(ge, c, clo)
            hi = jnp.where(ge, hi, mid)
            done = jnp.where(stuck | (clo <= kf), 1.0, done)
        return lo, hi, clo, done, it + 1

    def not_converged(carry):
        _, _, _, done, it = carry
        return (jnp.min(done) < 0.5) & (it < BISECT_MAX_CHECKS)

    lo, hi, clo, _, _ = lax.while_loop(not_converged, bisect, (lo, hi, clo, done0, jnp.int32(0)))

    thr = lo
    c_gt = count(lambda t: t > thr)
    c_eq = count(lambda t: t == thr)
    need = kf - c_gt
    tie_cut = jnp.max(c_gt + c_eq - kf) > 0.0

    def with_ties():
        r_i = lax.broadcasted_iota(jnp.int32, (KEY_TILE, KEY_TILE), 0)
        c_i = lax.broadcasted_iota(jnp.int32, (KEY_TILE, KEY_TILE), 1)
        earlier = jnp.where(c_i < r_i, 1.0, 0.0).astype(BF16)
        carry = jnp.zeros((1, tq), F32)
        for kt in range(n_kt):
            ks = slice(kt * KEY_TILE, (kt + 1) * KEY_TILE)
            sc = sct_scr[ks, :]
            eq = sc == thr
            eq_f = jnp.where(eq, 1.0, 0.0)
            rank = _dot(earlier, eq_f.astype(BF16)) + carry
            keep = (sc > thr) | (eq & (rank < need))
            msk_scr[:, ks] = jnp.where(keep, 0.0, NEG_INF).T
            carry = carry + jnp.sum(eq_f, axis=0, keepdims=True)

    def without_ties():
        for kt in range(n_kt):
            ks = slice(kt * KEY_TILE, (kt + 1) * KEY_TILE)
            msk_scr[:, ks] = jnp.where(sct_scr[ks, :] >= thr, 0.0, NEG_INF).T

    lax.cond(tie_cut, with_ties, without_ties)

    heads = [slice(h * HEAD_DIM, (h + 1) * HEAD_DIM) for h in range(N_HEADS)]
    row_max = []
    for h, sl in enumerate(heads):
        qh = qb_ref[0, :, sl]
        mx = jnp.full((tq, LANES), NEG_INF, F32)
        for kt in range(n_kt):
            ks = slice(kt * KEY_TILE, (kt + 1) * KEY_TILE)
            a = _dot_nt(qh, kb_ref[0, ks, sl]) + msk_scr[:, ks]
            a_scr[h, :, ks] = a
            for g in range(KEY_TILE // LANES):
                mx = jnp.maximum(mx, a[:, lanes(g)])
        row_max.append(jnp.max(mx, axis=-1, keepdims=True))
    row_sum = []
    for h in range(N_HEADS):
        m_b = jnp.broadcast_to(row_max[h], (tq, LANES))
        ls = jnp.zeros((tq, LANES), F32)
        for g in range(n_lg):
            p = jnp.exp(a_scr[h, :, lanes(g)] - m_b)
            ls = ls + p
            p_scr[h, :, lanes(g)] = p.astype(BF16)
        row_sum.append(jnp.sum(ls, axis=-1, keepdims=True))
    for h, sl in enumerate(heads):
        o = _dot(p_scr[h], vb_ref[0, :, sl]) / row_sum[h]
        o_ref[0, :, sl] = o.astype(BF16)


def _dsa_class(qi, wit, qb, ki, kb, vb, tq, k_sel, q_base, q_len, sk):
    bsz = qb.shape[0]
    blk0 = q_base // tq
    row = lambda b, i: (b, blk0 + i, 0)
    keys = lambda b, i: (b, 0, 0)
    return pl.pallas_call(
        functools.partial(_dsa_kernel, tq=tq, k_sel=k_sel, q_base=q_base),
        out_shape=jax.ShapeDtypeStruct((bsz, q_len, WIDTH), BF16),
        grid=(bsz, q_len // tq),
        in_specs=[
            pl.BlockSpec((1, tq, N_IDX_HEADS * IDX_DIM), row),
            pl.BlockSpec((1, N_IDX_HEADS, tq), lambda b, i: (b, 0, blk0 + i)),
            pl.BlockSpec((1, tq, WIDTH), row),
            pl.BlockSpec((1, sk, IDX_DIM), keys),
            pl.BlockSpec((1, sk, WIDTH), keys),
            pl.BlockSpec((1, sk, WIDTH), keys),
        ],
        out_specs=pl.BlockSpec((1, tq, WIDTH), lambda b, i: (b, i, 0)),
        scratch_shapes=[pltpu.VMEM((sk, tq), F32), pltpu.VMEM((tq, sk), F32),
                        pltpu.VMEM((N_HEADS, tq, sk), F32), pltpu.VMEM((N_HEADS, tq, sk), BF16)],
        compiler_params=_cparams(("arbitrary", "arbitrary")),
        name=f"dsa_k{sk}",
    )(qi, wit, qb, ki, kb, vb)


def _dsa(qi, wit, qb, ki, kb, vb, tq):
    s = qb.shape[1]
    k_sel = min(TOPK_MAX, s // 4)
    n_cls = max(1, min(DSA_KEY_CLASSES, s // KEY_TILE))
    q_len = s // n_cls
    outs = [_dsa_class(qi, wit, qb, ki, kb, vb, tq, k_sel, c * q_len, q_len, (c + 1) * q_len)
            for c in range(n_cls)]
    return jnp.concatenate(outs, axis=1)


def _merge_kernel(oa_ref, ob_ref, ga_ref, gb_ref, x_ref, mod_ref, wba_ref, wbb_ref, wo_ref,
                  ln1_ref, wr_ref, rb_ref, wsg_ref, wsu_ref, wsd_ref, z_ref, gk_ref, u2t_ref, expk_ref,
                  rankk_ref, seen_ref, seen_scr, *, alpha):
    tm = x_ref.shape[1]
    ya = _dot(oa_ref[0], wba_ref[...])
    yb = _dot(ob_ref[0], wbb_ref[...])
    merged = _sigmoid(ga_ref[0].astype(F32)) * ya + _sigmoid(gb_ref[0].astype(F32)) * yb
    mix = _dot(merged.astype(BF16), wo_ref[...])
    mod = mod_ref[0]
    g1 = mod[2:3, :]
    sh2 = mod[3:4, :]
    sc2 = mod[4:5, :]
    ln1 = ln1_ref[...]
    x1 = _ln(alpha * x_ref[0] + g1 * mix) * ln1[0:1, :] + ln1[1:2, :]
    u2 = _ln(x1) * (1.0 + sc2) + sh2
    t = u2.astype(BF16)
    shared = _dot((_silu(_dot(t, wsg_ref[...])) * _dot(t, wsu_ref[...])).astype(BF16), wsd_ref[...])
    z_ref[0] = alpha * x1 + mod[5:6, :] * shared

    u_hi, u_lo = _split_bf16(u2)
    both = _dot(u_hi, wr_ref[...])
    logits = both[:, 0:LANES] + both[:, LANES:] + _dot(u_lo, wr_ref[:, 0:LANES])
    aff = _sigmoid(logits.T[0:N_EXPERTS, :])
    biased = aff + rb_ref[...]
    grp = biased.reshape(N_GROUPS, GROUP_SIZE, tm)
    sub = lax.broadcasted_iota(jnp.int32, grp.shape, 1)
    m1 = jnp.max(grp, axis=1, keepdims=True)
    first = jnp.min(jnp.where(grp == m1, sub, GROUP_SIZE), axis=1, keepdims=True)
    m2 = jnp.max(jnp.where(sub == first, NEG_INF, grp), axis=1, keepdims=True)
    gscore = (m1 + m2).reshape(N_GROUPS, tm)
    g_i = lax.broadcasted_iota(jnp.int32, (N_GROUPS, tm), 0)
    g_rank = jnp.zeros((N_GROUPS, tm), F32)
    for g in range(N_GROUPS):
        other = gscore[g:g + 1, :]
        beats = (other > gscore) | ((other == gscore) & (g < g_i))
        g_rank = g_rank + jnp.where(beats, 1.0, 0.0)
    g_keep = jnp.where(g_rank < TOPK_GROUPS, 1.0, 0.0).reshape(N_GROUPS, 1, tm)
    e_keep = jnp.broadcast_to(g_keep, (N_GROUPS, GROUP_SIZE, tm)).reshape(N_EXPERTS, tm)
    masked = jnp.where(e_keep > 0.5, biased, NEG_INF)
    e_i = lax.broadcasted_iota(jnp.int32, (N_EXPERTS, tm), 0)
    chosen = jnp.zeros((N_EXPERTS, tm), F32)
    picks = []
    for _ in range(TOPK_EXPERTS):
        best = jnp.max(masked, axis=0, keepdims=True)
        first = jnp.min(jnp.where(masked == best, e_i, N_EXPERTS), axis=0, keepdims=True)
        hit = e_i == first
        chosen = jnp.where(hit, 1.0, chosen)
        masked = jnp.where(hit, NEG_INF, masked)
        picks.append(first)
    top_aff = jnp.where(chosen > 0.5, aff, 0.0)
    comb_t = top_aff / jnp.sum(top_aff, axis=0, keepdims=True) * ROUTED_SCALE

    @pl.when((pl.program_id(0) == 0) & (pl.program_id(1) == 0))
    def _():
        seen_scr[...] = jnp.zeros_like(seen_scr)

    r_i = lax.broadcasted_iota(jnp.int32, (tm, tm), 0)
    c_i = lax.broadcasted_iota(jnp.int32, (tm, tm), 1)
    earlier = jnp.where(r_i < c_i, 1.0, 0.0).astype(BF16)
    arrival = _dot(chosen.astype(BF16), earlier) + seen_scr[:, 0:1]
    seen_scr[...] = seen_scr[...] + jnp.sum(chosen, axis=1, keepdims=True)
    seen_ref[...] = seen_scr[...]
    ranks, gates = [], []
    for first in picks:
        hit = e_i == first
        ranks.append(jnp.sum(jnp.where(hit, arrival, 0.0), axis=0, keepdims=True))
        gates.append(jnp.sum(jnp.where(hit, comb_t, 0.0), axis=0, keepdims=True))
    expk_ref[...] = jnp.concatenate(picks, axis=0)
    rankk_ref[...] = jnp.concatenate(ranks, axis=0).astype(jnp.int32)
    gate_rows = jnp.concatenate(gates + [jnp.zeros((LANES - TOPK_EXPERTS, tm), F32)], axis=0)
    gk_ref[0] = gate_rows.T
    _store_token_rows(u2t_ref, _pack_pairs(u2), tm)


def _merge(oa, ob, ga, gb, x, mod, wba, wbb, wo, ln1, wr, rb, wsg, wsu, wsd, tm, alpha):
    bsz, s, d = x.shape
    row = lambda b, i: (b, i, 0)
    per_batch = s // tm
    flat = lambda b, i: (b * per_batch + i, 0)
    cols = lambda b, i: (0, b * per_batch + i)
    w2 = lambda shape: pl.BlockSpec(shape, lambda b, i: (0,) * len(shape))
    return pl.pallas_call(
        functools.partial(_merge_kernel, alpha=alpha),
        out_shape=[jax.ShapeDtypeStruct((bsz, s, d), F32),
                   jax.ShapeDtypeStruct((bsz, s, LANES), F32),
                   jax.ShapeDtypeStruct((bsz * s * TILE_ROWS, LANES), jnp.uint32),
                   jax.ShapeDtypeStruct((TOPK_EXPERTS, bsz * s), jnp.int32),
                   jax.ShapeDtypeStruct((TOPK_EXPERTS, bsz * s), jnp.int32),
                   jax.ShapeDtypeStruct((N_EXPERTS, LANES), F32)],
        grid=(bsz, s // tm),
        in_specs=[
            pl.BlockSpec((1, tm, WIDTH), row), pl.BlockSpec((1, tm, WIDTH), row),
            pl.BlockSpec((1, tm, d), row), pl.BlockSpec((1, tm, d), row),
            pl.BlockSpec((1, tm, d), row),
            pl.BlockSpec((1, N_MOD, d), lambda b, i: (b, 0, 0)),
            w2(wba.shape), w2(wbb.shape), w2(wo.shape), w2(ln1.shape), w2(wr.shape), w2(rb.shape),
            w2(wsg.shape), w2(wsu.shape), w2(wsd.shape),
        ],
        out_specs=[pl.BlockSpec((1, tm, d), row),
                   pl.BlockSpec((1, tm, LANES), row),
                   pl.BlockSpec((tm * TILE_ROWS, LANES), flat),
                   pl.BlockSpec((TOPK_EXPERTS, tm), cols),
                   pl.BlockSpec((TOPK_EXPERTS, tm), cols),
                   pl.BlockSpec((N_EXPERTS, LANES), lambda b, i: (0, 0))],
        scratch_shapes=[pltpu.VMEM((N_EXPERTS, LANES), F32)],
        compiler_params=_cparams(("arbitrary", "arbitrary")),
        name="merge",
    )(oa, ob, ga, gb, x, mod, wba, wbb, wo, ln1, wr, rb, wsg, wsu, wsd)


def _silu(z):
    return z * _sigmoid(z)


def _sc_params():
    return pltpu.CompilerParams(use_tc_tiling_on_sc=False)


def _sc_mesh():
    return plsc.VectorSubcoreMesh(core_axis_name="c", subcore_axis_name="s",
                                  num_cores=SC_CORES, num_subcores=SC_SUBCORES)


def _sc_dispatch(x_tiles, dest, n_rows):
    n_tok = x_tiles.shape[0]
    n_k = dest.shape[0]
    mesh = _sc_mesh()

    @pl.kernel(out_type=jax.ShapeDtypeStruct((n_rows,) + x_tiles.shape[1:], x_tiles.dtype), mesh=mesh,
               scratch_types=[], compiler_params=_sc_params())
    def scatter_rows(x_hbm, *rest):
        idx_hbm, o_hbm = rest[:n_k], rest[n_k]

        def body(x_vmem, *idx_vmem):
            for k in range(n_k):
                pltpu.sync_copy(x_vmem, o_hbm.at[idx_vmem[k].at[0]])

        pltpu.emit_pipeline(
            body, grid=(n_tok // SC_WINDOW,),
            in_specs=[pl.BlockSpec((SC_WINDOW,) + x_tiles.shape[1:], lambda i: (i, 0, 0))]
            + [pl.BlockSpec((1, SC_WINDOW), lambda i: (0, i))] * n_k,
            out_specs=[], core_axis_name=("c", "s"), dimension_semantics=(pltpu.PARALLEL,),
        )(x_hbm, *idx_hbm)

    return scatter_rows(x_tiles, *[dest[k:k + 1] for k in range(n_k)])


def _sc_gather(rows, idx):
    n = idx.shape[1]
    mesh = _sc_mesh()

    @pl.kernel(out_type=jax.ShapeDtypeStruct((n,) + rows.shape[1:], rows.dtype), mesh=mesh,
               scratch_types=[], compiler_params=_sc_params())
    def gather_rows(r_hbm, i_hbm, o_hbm):
        def body(i_vmem, o_vmem):
            pltpu.sync_copy(r_hbm.at[i_vmem.at[0]], o_vmem)

        pltpu.emit_pipeline(
            body, grid=(n // SC_WINDOW,),
            in_specs=[pl.BlockSpec((1, SC_WINDOW), lambda i: (0, i))],
            out_specs=[pl.BlockSpec((SC_WINDOW,) + rows.shape[1:], lambda i: (i, 0, 0))],
            core_axis_name=("c", "s"), dimension_semantics=(pltpu.PARALLEL,),
        )(i_hbm, o_hbm)

    return gather_rows(rows, idx)


def _ffn_grouped_kernel(expert_ref, valid_ref, used_ref, x_ref, wg_ref, wu_ref, wd_ref, y_ref,
                        wg_bf, wu_bf, wd_bf):
    i = pl.program_id(0)
    n_rows = x_ref.shape[0] // TILE_ROWS
    in_use = i < used_ref[0]
    new_expert = (i == 0) | (expert_ref[i] != expert_ref[jnp.maximum(i - 1, 0)])

    @pl.when(in_use & new_expert)
    def _():
        wg_bf[...] = wg_ref[0].astype(BF16)
        wu_bf[...] = wu_ref[0].astype(BF16)
        wd_bf[...] = wd_ref[0].astype(BF16)

    @pl.when(in_use)
    def _():
        row = lax.broadcasted_iota(jnp.int32, (n_rows, 1), 0)
        words = jnp.where(row < valid_ref[i], _load_token_rows(x_ref, n_rows), jnp.uint32(0))
        x = _unpack_pairs(words).astype(BF16)
        h = _silu(_dot(x, wg_bf[...])) * _dot(x, wu_bf[...])
        y = _dot(h.astype(BF16), wd_bf[...])
        _store_token_rows(y_ref, _pack_pairs(y), n_rows)


def _ffn_grouped(tile_expert, tile_valid, tiles_used, xs, wg, wu, wd, n_tiles):
    _, d, d_exp = wg.shape
    blk = pl.BlockSpec((MOE_ROW_TILE * TILE_ROWS, LANES), lambda i, te, tv, nu: (i, 0))
    weight = lambda shape: pl.BlockSpec(shape, lambda i, te, tv, nu: (te[i], 0, 0))
    return pl.pallas_call(
        _ffn_grouped_kernel,
        out_shape=jax.ShapeDtypeStruct(xs.shape, xs.dtype),
        grid_spec=pltpu.PrefetchScalarGridSpec(
            num_scalar_prefetch=3, grid=(n_tiles,),
            in_specs=[blk, weight((1, d, d_exp)), weight((1, d, d_exp)), weight((1, d_exp, d))],
            out_specs=blk,
            scratch_shapes=[pltpu.VMEM((d, d_exp), BF16), pltpu.VMEM((d, d_exp), BF16),
                            pltpu.VMEM((d_exp, d), BF16)]),
        compiler_params=_cparams(("arbitrary",)),
        name="moe_ffn",
    )(tile_expert, tile_valid, tiles_used, xs, wg, wu, wd)


def _moe_final_kernel(yg_ref, gk_ref, z_ref, mod_ref, ln2_ref, o_ref):
    tm = z_ref.shape[0]
    gk = gk_ref[...]
    routed = jnp.zeros(z_ref.shape, F32)
    for k in range(TOPK_EXPERTS):
        routed = routed + _unpack_pairs(_load_token_rows(yg_ref, tm, (k,))) * gk[:, k:k + 1]
    g2 = mod_ref[0][5:6, :]
    ln2 = ln2_ref[...]
    o_ref[...] = _ln(z_ref[...] + g2 * routed) * ln2[0:1, :] + ln2[1:2, :]


def _moe_final(yg, gk, z, mod, ln2, tm, tokens_per_batch, tok0):
    t, d = z.shape
    n_tok = yg.shape[1] // TILE_ROWS
    blocks_per_batch = tokens_per_batch // tm
    blk0 = tok0 // tm
    row = lambda i: (blk0 + i, 0)
    return pl.pallas_call(
        _moe_final_kernel,
        out_shape=jax.ShapeDtypeStruct((t, d), F32),
        grid=(n_tok // tm,),
        in_specs=[
            pl.BlockSpec((TOPK_EXPERTS, tm * TILE_ROWS, LANES), lambda i: (0, i, 0)),
            pl.BlockSpec((tm, LANES), row),
            pl.BlockSpec((tm, d), row),
            pl.BlockSpec((1, N_MOD, d), lambda i: ((blk0 + i) // blocks_per_batch, 0, 0)),
            pl.BlockSpec(ln2.shape, lambda i: (0, 0)),
        ],
        out_specs=pl.BlockSpec((tm, d), row),
        input_output_aliases={2: 0},
        compiler_params=_cparams(("arbitrary",)),
        name="moe_final",
    )(yg, gk, z, mod, ln2)


def _sorted_layout(expk, rankk, counts, n_tiles):
    e_ids = jnp.arange(N_EXPERTS, dtype=jnp.int32)[:, None, None]
    is_e = expk[None] == e_ids
    tiles = (counts + MOE_ROW_TILE - 1) // MOE_ROW_TILE
    tile_end = jnp.cumsum(tiles)
    first_row = (tile_end - tiles) * MOE_ROW_TILE
    dest = jnp.sum(jnp.where(is_e, first_row[:, None, None], 0), axis=0, dtype=jnp.int32) + rankk
    tile_ids = jnp.arange(n_tiles, dtype=jnp.int32)[:, None]
    done = tile_ids >= tile_end[None, :]
    owner = ~done & (tile_ids >= (tile_end - tiles)[None, :])
    tile_expert = jnp.minimum(jnp.sum(done, axis=1, dtype=jnp.int32), N_EXPERTS - 1)
    rows_before = (tile_ids[:, 0] - jnp.sum(jnp.where(done, tiles[None, :], 0), axis=1)) * MOE_ROW_TILE
    owner_count = jnp.sum(jnp.where(owner, counts[None, :], 0), axis=1)
    tile_valid = jnp.clip(owner_count - rows_before, 0, MOE_ROW_TILE)
    return dest, tile_expert, tile_valid, tile_end[-1:]


def _rope_tables(positions):
    inv_freq = ROPE_THETA ** (-jnp.arange(0, ROPE_DIM, 2, dtype=F32) / ROPE_DIM)
    ang = positions.astype(F32)[..., None] * inv_freq
    cos, sin = jnp.cos(ang), jnp.sin(ang)
    ones = jnp.ones(cos.shape[:-1] + (HEAD_DIM - ROPE_DIM,), F32)
    zeros = jnp.zeros_like(ones)
    zh = jnp.zeros_like(sin)
    c = jnp.concatenate([cos, cos, ones], axis=-1)
    s1 = jnp.concatenate([-sin, zh, zeros], axis=-1)
    s2 = jnp.concatenate([zh, sin, zeros], axis=-1)
    reps = LANES // HEAD_DIM
    return jnp.concatenate([jnp.tile(c, reps), jnp.tile(s1, reps), jnp.tile(s2, reps)], axis=-1)


def _band_bias_profile(rel_bias, tq):
    n_heads = rel_bias.shape[0]
    win = tq + BAND_PAD
    n_high = BAND_PAD - MAX_REL + 1
    n_ramp = 2 * MAX_REL - 1
    high = rel_bias[:, 2 * MAX_REL:]
    return jnp.concatenate([
        jnp.broadcast_to(high, (n_heads, n_high)),
        rel_bias[:, n_ramp:0:-1],
        jnp.broadcast_to(rel_bias[:, :1], (n_heads, win - n_high - n_ramp)),
        jnp.broadcast_to(high, (n_heads, tq)),
    ], axis=1)


def kernel(x, c, positions, w_ada, b_ada, w_in, rel_bias, idx_k_norm_g, idx_k_norm_b, w_branch_a,
           w_branch_b, w_out, ln1_g, ln1_b, w_router, router_bias, w_exp_gate, w_exp_up, w_exp_down,
           w_sh_gate, w_sh_up, w_sh_down, ln2_g, ln2_b):
    bsz, s, d = x.shape
    depth = w_ada.shape[0]
    alpha = (2.0 * depth) ** 0.25
    tm_proj = min(512, s)
    tq_band = min(256, s)
    tq_dsa = min(256, s)
    tm_merge = min(512, s)
    tm_moe = min(256, s)

    rope_tab = _rope_tables(positions)
    n7 = 7 * WIDTH
    for l in range(depth):
        mod = _ada(c, w_ada[l], b_ada[l]).reshape(bsz, N_MOD, d)
        w_l = w_in[l]
        w7 = w_l[:, :n7].astype(BF16)
        n_kw = IDX_DIM + N_IDX_HEADS
        wkw = jnp.pad(w_l[:, n7:n7 + n_kw], ((0, 0), (0, LANES - n_kw))).astype(BF16)
        wg = w_l[:, n7 + n_kw:].astype(BF16)
        lnk = jnp.pad(jnp.stack([idx_k_norm_g[l], idx_k_norm_b[l]]), ((0, 0), (0, LANES - IDX_DIM)))
        qa, ka, va, qb, kb, vb, qi, ki, wi, ga, gb = _proj(x, mod, rope_tab, w7, wkw, wg, lnk, tm_proj)

        oa = _band(qa, ka, va, _band_bias_profile(rel_bias[l], tq_band), tq_band)
        ob = _dsa(qi, wi, qb, ki, kb, vb, tq_dsa)

        wr = jnp.pad(w_router[l], ((0, 0), (0, LANES - N_EXPERTS)))
        wr_hi = wr.astype(BF16)
        wr_lo = (wr - wr_hi.astype(F32)).astype(BF16)
        z, gk, u2_tiles, expk, rankk, seen = _merge(
            oa, ob, ga, gb, x, mod,
            w_branch_a[l].astype(BF16), w_branch_b[l].astype(BF16), w_out[l].astype(BF16),
            jnp.stack([ln1_g[l], ln1_b[l]]), jnp.concatenate([wr_hi, wr_lo], axis=1),
            router_bias[l].reshape(N_EXPERTS, 1),
            w_sh_gate[l].astype(BF16), w_sh_up[l].astype(BF16), w_sh_down[l].astype(BF16), tm_merge, alpha)

        t = bsz * s
        n_tiles = t * TOPK_EXPERTS // MOE_ROW_TILE + N_EXPERTS
        n_rows = n_tiles * MOE_ROW_TILE
        dest, tile_expert, tile_valid, tiles_used = _sorted_layout(
            expk, rankk, seen[:, 0].astype(jnp.int32), n_tiles)
        xs = _sc_dispatch(u2_tiles.reshape(t, TILE_ROWS, LANES), dest, n_rows)
        ys = _ffn_grouped(tile_expert, tile_valid, tiles_used, xs.reshape(n_rows * TILE_ROWS, LANES),
                          w_exp_gate[l], w_exp_up[l], w_exp_down[l], n_tiles)
        ys = ys.reshape(n_rows, TILE_ROWS, LANES)
        ln2 = jnp.stack([ln2_g[l], ln2_b[l]])
        chunk = t // MOE_COMBINE_CHUNKS
        out = z.reshape(t, d)
        for c in range(MOE_COMBINE_CHUNKS):
            idx = dest[:, c * chunk:(c + 1) * chunk].reshape(1, TOPK_EXPERTS * chunk)
            yg = _sc_gather(ys, idx).reshape(TOPK_EXPERTS, chunk * TILE_ROWS, LANES)
            out = _moe_final(yg, gk.reshape(t, LANES), out, mod, ln2, tm_moe, s, c * chunk)
        x = out.reshape(bsz, s, d)
    return x
```

```python
import functools

import jax
import jax.numpy as jnp
import numpy as np
from jax import lax
from jax.experimental import pallas as pl
from jax.experimental.pallas import tpu as pltpu
from jax.experimental.pallas import tpu_sc as plsc

F32 = jnp.float32
BF16 = jnp.bfloat16
NEG_INF = float("-inf")

CHUNK = 64
HEAD_DIM = 64
N_HEADS = 8
WIDTH = N_HEADS * HEAD_DIM
BAND_CHUNKS = 9
BAND_PAD = (BAND_CHUNKS - 1) * CHUNK
MAX_REL = 128
ROPE_THETA = 500000.0
ROPE_DIM = HEAD_DIM // 4
ROPE_HALF = ROPE_DIM // 2
N_IDX_HEADS = 8
IDX_DIM = 64
TOPK_MAX = 256
N_EXPERTS = 64
N_GROUPS = 8
GROUP_SIZE = N_EXPERTS // N_GROUPS
TOPK_GROUPS = 4
TOPK_EXPERTS = 8
ROUTED_SCALE = 1.0
N_MOD = 6
LN_EPS = 1e-5
LANES = 128
KEY_TILE = 256

VMEM_LIMIT = 56 * 1024 * 1024

DSA_KEY_CLASSES = 8
TILE_ROWS = 4
MOE_ROW_TILE = 1024
SC_WINDOW = 64
MOE_COMBINE_CHUNKS = 4
SC_CORES = 2
SC_SUBCORES = 16


def _cparams(sem):
    return pltpu.CompilerParams(dimension_semantics=sem, vmem_limit_bytes=VMEM_LIMIT)


def _ln(z):
    mu = jnp.mean(z, axis=-1, keepdims=True)
    zc = z - mu
    var = jnp.mean(zc * zc, axis=-1, keepdims=True)
    return zc * lax.rsqrt(var + LN_EPS)


def _sigmoid(z):
    return 1.0 / (1.0 + jnp.exp(-z))


def _dot(a, b):
    return jnp.dot(a, b, preferred_element_type=F32)


def _dot_nt(a, b):
    return lax.dot_general(a, b, (((1,), (1,)), ((), ())), preferred_element_type=F32)


def _split_bf16(z):
    hi = z.astype(BF16)
    lo = (z - hi.astype(F32)).astype(BF16)
    return hi, lo


HIGH_HALF = 0xFFFF0000


def _pack_pairs(z):
    half = z.shape[1] // 2

    def bits(v):
        return lax.bitcast_convert_type(v.astype(BF16).astype(F32), jnp.uint32)

    return (bits(z[:, :half]) >> 16) | (bits(z[:, half:]) & jnp.uint32(HIGH_HALF))


def _unpack_pairs(p):
    low = lax.bitcast_convert_type(p << 16, F32)
    high = lax.bitcast_convert_type(p & jnp.uint32(HIGH_HALF), F32)
    return jnp.concatenate([low, high], axis=1)


def _store_token_rows(ref, words, n_tok):
    for q in range(TILE_ROWS):
        ref[pl.ds(q, n_tok, stride=TILE_ROWS), :] = words[:, q * LANES:(q + 1) * LANES]


def _load_token_rows(ref, n_tok, lead=()):
    return jnp.concatenate([ref[lead + (pl.ds(q, n_tok, stride=TILE_ROWS), slice(None))]
                            for q in range(TILE_ROWS)], axis=1)


def _ada_kernel(c_ref, w_ref, b_ref, o_ref):
    c = c_ref[...]
    ca = c * _sigmoid(c)
    ca_hi, ca_lo = _split_bf16(ca)
    w = w_ref[...]
    w_hi, w_lo = _split_bf16(w)
    acc = _dot(ca_hi, w_hi) + _dot(ca_lo, w_hi) + _dot(ca_hi, w_lo)
    o_ref[...] = acc + b_ref[...]


def _ada(c, w_ada, b_ada):
    bsz, d = c.shape
    n = w_ada.shape[1]
    tn = 1024
    return pl.pallas_call(
        _ada_kernel,
        out_shape=jax.ShapeDtypeStruct((bsz, n), F32),
        grid=(n // tn,),
        in_specs=[
            pl.BlockSpec((bsz, d), lambda j: (0, 0)),
            pl.BlockSpec((d, tn), lambda j: (0, j)),
            pl.BlockSpec((1, tn), lambda j: (0, j)),
        ],
        out_specs=pl.BlockSpec((bsz, tn), lambda j: (0, j)),
        compiler_params=_cparams(("arbitrary",)),
        name="ada",
    )(c, w_ada, b_ada.reshape(1, n))


def _rope(z, c_t, s1_t, s2_t):
    n = z.shape[-1]
    return z * c_t + pltpu.roll(z, n - ROPE_HALF, 1) * s1_t + pltpu.roll(z, ROPE_HALF, 1) * s2_t


def _proj_kernel(x_ref, mod_ref, rope_ref, w7_ref, wkw_ref, wg_ref, lnk_ref,
                 qa_ref, ka_ref, va_ref, qb_ref, kb_ref, vb_ref, qi_ref, ki_ref, wi_ref,
                 ga_ref, gb_ref):
    x = x_ref[0]
    mod = mod_ref[0]
    sh1 = mod[0:1, :]
    sc1 = mod[1:2, :]
    u = (_ln(x) * (1.0 + sc1) + sh1).astype(BF16)

    rope = rope_ref[0]
    c1 = rope[:, 0:LANES]
    s1 = rope[:, LANES:2 * LANES]
    s2 = rope[:, 2 * LANES:3 * LANES]
    reps = WIDTH // LANES
    c_t = jnp.concatenate([c1] * reps, axis=1)
    s1_t = jnp.concatenate([s1] * reps, axis=1)
    s2_t = jnp.concatenate([s2] * reps, axis=1)

    att_scale = HEAD_DIM ** -0.5
    idx_scale = IDX_DIM ** -0.5

    def seg(k):
        return _dot(u, w7_ref[:, k * WIDTH:(k + 1) * WIDTH])

    qa_ref[0] = (seg(0) * att_scale).astype(BF16)
    ka_ref[0] = seg(1).astype(BF16)
    va_ref[0] = seg(2).astype(BF16)
    qb_ref[0] = (_rope(seg(3), c_t, s1_t, s2_t) * att_scale).astype(BF16)
    kb_ref[0] = _rope(seg(4), c_t, s1_t, s2_t).astype(BF16)
    vb_ref[0] = seg(5).astype(BF16)
    qi_ref[0] = (_rope(seg(6), c_t, s1_t, s2_t) * idx_scale).astype(BF16)

    z = _dot(u, wkw_ref[...])
    lane = lax.broadcasted_iota(jnp.int32, z.shape, 1)
    is_k = lane < IDX_DIM
    mu = jnp.sum(jnp.where(is_k, z, 0.0), axis=-1, keepdims=True) * (1.0 / IDX_DIM)
    zc = jnp.where(is_k, z - mu, 0.0)
    var = jnp.sum(zc * zc, axis=-1, keepdims=True) * (1.0 / IDX_DIM)
    lnk = lnk_ref[...]
    y = zc * lax.rsqrt(var + LN_EPS) * lnk[0:1, :] + lnk[1:2, :]
    y = _rope(y, c1, jnp.where(is_k, s1, 0.0), jnp.where(is_k, s2, 0.0))
    ki_ref[0] = y[:, 0:IDX_DIM].astype(BF16)
    wi_ref[0] = z.T[IDX_DIM:IDX_DIM + N_IDX_HEADS, :] * (N_IDX_HEADS ** -0.5)

    d = ga_ref.shape[-1]
    ga_ref[0] = _dot(u, wg_ref[:, 0:d]).astype(BF16)
    gb_ref[0] = _dot(u, wg_ref[:, d:2 * d]).astype(BF16)


def _proj(x, mod, rope_tab, w7, wkw, wg, lnk, tm):
    bsz, s, d = x.shape
    const = dict(pipeline_mode=pl.Buffered(1))
    row = lambda b, i: (b, i, 0)
    wspec = lambda shape: pl.BlockSpec(shape, lambda b, i: (0, 0), **const)
    out_w = jax.ShapeDtypeStruct((bsz, s, WIDTH), BF16)
    out_d = jax.ShapeDtypeStruct((bsz, s, d), BF16)
    return pl.pallas_call(
        _proj_kernel,
        out_shape=[out_w] * 7 + [
            jax.ShapeDtypeStruct((bsz, s, IDX_DIM), BF16),
            jax.ShapeDtypeStruct((bsz, N_IDX_HEADS, s), F32),
            out_d, out_d],
        grid=(bsz, s // tm),
        in_specs=[
            pl.BlockSpec((1, tm, d), row),
            pl.BlockSpec((1, N_MOD, d), lambda b, i: (b, 0, 0)),
            pl.BlockSpec((1, tm, 3 * LANES), row),
            wspec(w7.shape), wspec(wkw.shape), wspec(wg.shape), wspec(lnk.shape),
        ],
        out_specs=[pl.BlockSpec((1, tm, WIDTH), row)] * 7 + [
            pl.BlockSpec((1, tm, IDX_DIM), row),
            pl.BlockSpec((1, N_IDX_HEADS, tm), lambda b, i: (b, 0, i)),
            pl.BlockSpec((1, tm, d), row), pl.BlockSpec((1, tm, d), row)],
        compiler_params=_cparams(("arbitrary", "arbitrary")),
        name="proj",
    )(x, mod, rope_tab, w7, wkw, wg, lnk)


def _band_kernel(q_ref, k_ref, v_ref, prof_ref, o_ref, kpad, vpad, a_scr, p_scr, bias_ref, *, tq):
    i = pl.program_id(1)
    s = k_ref.shape[1]
    win = tq + BAND_PAD

    @pl.when((pl.program_id(0) == 0) & (i == 0))
    def _():
        q_chunk = lax.broadcasted_iota(jnp.int32, (tq, win), 0) // CHUNK
        k_chunk = lax.broadcasted_iota(jnp.int32, (tq, win), 1) // CHUNK - (BAND_CHUNKS - 1)
        in_band = (k_chunk <= q_chunk) & (k_chunk >= q_chunk - (BAND_CHUNKS - 1))
        for h in range(N_HEADS):
            rows = jnp.broadcast_to(prof_ref[h:h + 1, :], (tq, prof_ref.shape[1]))
            skew = pltpu.roll(rows, 0, 1, stride=1, stride_axis=0)
            bias_ref[h] = jnp.where(in_band, skew[:, 0:win], NEG_INF)

    @pl.when(i == 0)
    def _():
        zeros = jnp.zeros((BAND_PAD, WIDTH), BF16)
        kpad[0:BAND_PAD, :] = zeros
        vpad[0:BAND_PAD, :] = zeros
        kpad[BAND_PAD:BAND_PAD + s, :] = k_ref[0]
        vpad[BAND_PAD:BAND_PAD + s, :] = v_ref[0]

    q0 = pl.multiple_of(i * tq, tq)
    n_lg = win // LANES
    col = lax.broadcasted_iota(jnp.int32, (tq, LANES), 1)
    first_frame = BAND_PAD - q0
    heads = [slice(h * HEAD_DIM, (h + 1) * HEAD_DIM) for h in range(N_HEADS)]

    def lanes(g):
        return slice(g * LANES, (g + 1) * LANES)

    def scores(mask_padding):
        row_max = []
        for h, sl in enumerate(heads):
            sc = _dot_nt(q_ref[0, :, sl], kpad[pl.ds(q0, win), sl])
            mx = jnp.full((tq, LANES), NEG_INF, F32)
            for g in range(n_lg):
                a = sc[:, lanes(g)] + bias_ref[h, :, lanes(g)]
                if mask_padding:
                    a = jnp.where(col >= first_frame - g * LANES, a, NEG_INF)
                a_scr[h, :, lanes(g)] = a
                mx = jnp.maximum(mx, a)
            row_max.append(jnp.max(mx, axis=-1, keepdims=True))
        return row_max

    row_max = lax.cond(first_frame > 0, lambda: scores(True), lambda: scores(False))
    row_sum = []
    for h in range(N_HEADS):
        m_b = jnp.broadcast_to(row_max[h], (tq, LANES))
        ls = jnp.zeros((tq, LANES), F32)
        for g in range(n_lg):
            p = jnp.exp(a_scr[h, :, lanes(g)] - m_b)
            ls = ls + p
            p_scr[h, :, lanes(g)] = p.astype(BF16)
        row_sum.append(jnp.sum(ls, axis=-1, keepdims=True))
    for h, sl in enumerate(heads):
        o = _dot(p_scr[h], vpad[pl.ds(q0, win), sl]) / row_sum[h]
        o_ref[0, :, sl] = o.astype(BF16)


def _band(qa, ka, va, profile, tq):
    bsz, s, _ = qa.shape
    win = tq + BAND_PAD
    return pl.pallas_call(
        functools.partial(_band_kernel, tq=tq),
        out_shape=jax.ShapeDtypeStruct((bsz, s, WIDTH), BF16),
        grid=(bsz, s // tq),
        in_specs=[
            pl.BlockSpec((1, tq, WIDTH), lambda b, i: (b, i, 0)),
            pl.BlockSpec((1, s, WIDTH), lambda b, i: (b, 0, 0)),
            pl.BlockSpec((1, s, WIDTH), lambda b, i: (b, 0, 0)),
            pl.BlockSpec(profile.shape, lambda b, i: (0, 0)),
        ],
        out_specs=pl.BlockSpec((1, tq, WIDTH), lambda b, i: (b, i, 0)),
        scratch_shapes=[pltpu.VMEM((BAND_PAD + s, WIDTH), BF16),
                        pltpu.VMEM((BAND_PAD + s, WIDTH), BF16),
                        pltpu.VMEM((N_HEADS, tq, win), F32),
                        pltpu.VMEM((N_HEADS, tq, win), BF16),
                        pltpu.VMEM((N_HEADS, tq, win), F32)],
        compiler_params=_cparams(("arbitrary", "arbitrary")),
        name="band",
    )(qa, ka, va, profile)


BISECT_STEPS_PER_CHECK = 4
BISECT_MAX_CHECKS = 400
COUNT_ROWS = 64


def _dsa_kernel(qi_ref, wit_ref, qb_ref, ki_ref, kb_ref, vb_ref, o_ref, sct_scr, msk_scr, a_scr, p_scr,
                *, tq, k_sel, q_base):
    i = pl.program_id(1)
    sk = ki_ref.shape[1]
    n_kt = sk // KEY_TILE
    n_lg = sk // LANES
    q0 = q_base + i * tq
    kf = float(k_sel)

    def lanes(g):
        return slice(g * LANES, (g + 1) * LANES)

    def slab_reduce(fn, combine, init):
        acc = jnp.full((COUNT_ROWS, tq), init, F32)
        for r in range(sk // COUNT_ROWS):
            acc = combine(acc, fn(sct_scr[r * COUNT_ROWS:(r + 1) * COUNT_ROWS, :]))
        return acc

    def count(pred):
        part = slab_reduce(lambda t: jnp.where(pred(t), 1.0, 0.0), jnp.add, 0.0)
        return jnp.sum(part, axis=0, keepdims=True)

    w_t = wit_ref[0]
    t_pos = q0 + lax.broadcasted_iota(jnp.int32, (1, tq), 1)
    limit = (t_pos // CHUNK + 1) * CHUNK
    key_in_tile = lax.broadcasted_iota(jnp.int32, (KEY_TILE, tq), 0)
    for kt in range(n_kt):
        ks = slice(kt * KEY_TILE, (kt + 1) * KEY_TILE)
        ki_t = ki_ref[0, ks, :]
        acc = jnp.zeros((KEY_TILE, tq), F32)
        for h in range(N_IDX_HEADS):
            lg = _dot_nt(ki_t, qi_ref[0, :, h * IDX_DIM:(h + 1) * IDX_DIM])
            acc = acc + jnp.maximum(lg, 0.0) * w_t[h:h + 1, :]
        sct_scr[ks, :] = jnp.where(key_in_tile < limit - kt * KEY_TILE, acc, NEG_INF)

    smax = jnp.max(slab_reduce(lambda t: t, jnp.maximum, NEG_INF), axis=0, keepdims=True)
    smin = jnp.min(slab_reduce(lambda t: jnp.where(t == NEG_INF, jnp.inf, t), jnp.minimum, jnp.inf),
                   axis=0, keepdims=True)
    n_adm = limit.astype(F32)
    c_max = count(lambda t: t >= smax)
    c_pos = count(lambda t: t > 0.0)
    c_nn = count(lambda t: t >= 0.0)
    zero = jnp.zeros_like(smax)
    at_zero = (c_pos < kf) & (c_nn >= kf)
    below_zero = c_nn < kf
    lo = jnp.where(at_zero | ~below_zero, zero, smin)
    clo = jnp.where(at_zero | ~below_zero, c_nn, n_adm)
    hi = jnp.where(at_zero | below_zero, zero, smax)
    few = n_adm <= kf
    lo = jnp.where(few, smin, lo)
    clo = jnp.where(few, n_adm, clo)
    top_tied = (c_max >= kf) & ~few
    lo = jnp.where(top_tied, smax, lo)
    clo = jnp.where(top_tied, c_max, clo)
    hi = jnp.where(top_tied, smax, hi)
    done0 = jnp.where(few | at_zero | top_tied, 1.0, 0.0)

    def bisect(carry):
        lo, hi, clo, done, it = carry
        for _ in range(BISECT_STEPS_PER_CHECK):
            mid = 0.5 * lo + 0.5 * hi
            stuck = (mid <= lo) | (mid >= hi)
            c = count(lambda t: t >= mid)
            ge = c >= kf
            lo = jnp.where(ge, mid, lo)
            clo = jnp.where(ge, c, clo)
            hi = jnp.where(ge, hi, mid)
            done = jnp.where(stuck | (clo <= kf), 1.0, done)
        return lo, hi, clo, done, it + 1

    def not_converged(carry):
        _, _, _, done, it = carry
        return (jnp.min(done) < 0.5) & (it < BISECT_MAX_CHECKS)

    lo, hi, clo, _, _ = lax.while_loop(not_converged, bisect, (lo, hi, clo, done0, jnp.int32(0)))

    thr = lo
    c_gt = count(lambda t: t > thr)
    c_eq = count(lambda t: t == thr)
    need = kf - c_gt
    tie_cut = jnp.max(c_gt + c_eq - kf) > 0.0

    def with_ties():
        r_i = lax.broadcasted_iota(jnp.int32, (KEY_TILE, KEY_TILE), 0)
        c_i = lax.broadcasted_iota(jnp.int32, (KEY_TILE, KEY_TILE), 1)
        earlier = jnp.where(c_i < r_i, 1.0, 0.0).astype(BF16)
        carry = jnp.zeros((1, tq), F32)
        for kt in range(n_kt):
            ks = slice(kt * KEY_TILE, (kt + 1) * KEY_TILE)
            sc = sct_scr[ks, :]
            eq = sc == thr
            eq_f = jnp.where(eq, 1.0, 0.0)
            rank = _dot(earlier, eq_f.astype(BF16)) + carry
            keep = (sc > thr) | (eq & (rank < need))
            msk_scr[:, ks] = jnp.where(keep, 0.0, NEG_INF).T
            carry = carry + jnp.sum(eq_f, axis=0, keepdims=True)

    def without_ties():
        for kt in range(n_kt):
            ks = slice(kt * KEY_TILE, (kt + 1) * KEY_TILE)
            msk_scr[:, ks] = jnp.where(sct_scr[ks, :] >= thr, 0.0, NEG_INF).T

    lax.cond(tie_cut, with_ties, without_ties)

    heads = [slice(h * HEAD_DIM, (h + 1) * HEAD_DIM) for h in range(N_HEADS)]
    row_max = []
    for h, sl in enumerate(heads):
        qh = qb_ref[0, :, sl]
        mx = jnp.full((tq, LANES), NEG_INF, F32)
        for kt in range(n_kt):
            ks = slice(kt * KEY_TILE, (kt + 1) * KEY_TILE)
            a = _dot_nt(qh, kb_ref[0, ks, sl]) + msk_scr[:, ks]
            a_scr[h, :, ks] = a
            for g in range(KEY_TILE // LANES):
                mx = jnp.maximum(mx, a[:, lanes(g)])
        row_max.append(jnp.max(mx, axis=-1, keepdims=True))
    row_sum = []
    for h in range(N_HEADS):
        m_b = jnp.broadcast_to(row_max[h], (tq, LANES))
        ls = jnp.zeros((tq, LANES), F32)
        for g in range(n_lg):
            p = jnp.exp(a_scr[h, :, lanes(g)] - m_b)
            ls = ls + p
            p_scr[h, :, lanes(g)] = p.astype(BF16)
        row_sum.append(jnp.sum(ls, axis=-1, keepdims=True))
    for h, sl in enumerate(heads):
        o = _dot(p_scr[h], vb_ref[0, :, sl]) / row_sum[h]
        o_ref[0, :, sl] = o.astype(BF16)


def _dsa_class(qi, wit, qb, ki, kb, vb, tq, k_sel, q_base, q_len, sk):
    bsz = qb.shape[0]
    blk0 = q_base // tq
    row = lambda b, i: (b, blk0 + i, 0)
    keys = lambda b, i: (b, 0, 0)
    return pl.pallas_call(
        functools.partial(_dsa_kernel, tq=tq, k_sel=k_sel, q_base=q_base),
        out_shape=jax.ShapeDtypeStruct((bsz, q_len, WIDTH), BF16),
        grid=(bsz, q_len // tq),
        in_specs=[
            pl.BlockSpec((1, tq, N_IDX_HEADS * IDX_DIM), row),
            pl.BlockSpec((1, N_IDX_HEADS, tq), lambda b, i: (b, 0, blk0 + i)),
            pl.BlockSpec((1, tq, WIDTH), row),
            pl.BlockSpec((1, sk, IDX_DIM), keys),
            pl.BlockSpec((1, sk, WIDTH), keys),
            pl.BlockSpec((1, sk, WIDTH), keys),
        ],
        out_specs=pl.BlockSpec((1, tq, WIDTH), lambda b, i: (b, i, 0)),
        scratch_shapes=[pltpu.VMEM((sk, tq), F32), pltpu.VMEM((tq, sk), F32),
                        pltpu.VMEM((N_HEADS, tq, sk), F32), pltpu.VMEM((N_HEADS, tq, sk), BF16)],
        compiler_params=_cparams(("arbitrary", "arbitrary")),
        name=f"dsa_k{sk}",
    )(qi, wit, qb, ki, kb, vb)


def _dsa(qi, wit, qb, ki, kb, vb, tq):
    s = qb.shape[1]
    k_sel = min(TOPK_MAX, s // 4)
    n_cls = max(1, min(DSA_KEY_CLASSES, s // KEY_TILE))
    q_len = s // n_cls
    outs = [_dsa_class(qi, wit, qb, ki, kb, vb, tq, k_sel, c * q_len, q_len, (c + 1) * q_len)
            for c in range(n_cls)]
    return jnp.concatenate(outs, axis=1)


def _merge_kernel(oa_ref, ob_ref, ga_ref, gb_ref, x_ref, mod_ref, wba_ref, wbb_ref, wo_ref,
                  ln1_ref, wr_ref, rb_ref, wsg_ref, wsu_ref, wsd_ref, z_ref, gk_ref, u2t_ref, expk_ref,
                  rankk_ref, seen_ref, seen_scr, *, alpha):
    tm = x_ref.shape[1]
    ya = _dot(oa_ref[0], wba_ref[...])
    yb = _dot(ob_ref[0], wbb_ref[...])
    merged = _sigmoid(ga_ref[0].astype(F32)) * ya + _sigmoid(gb_ref[0].astype(F32)) * yb
    mix = _dot(merged.astype(BF16), wo_ref[...])
    mod = mod_ref[0]
    g1 = mod[2:3, :]
    sh2 = mod[3:4, :]
    sc2 = mod[4:5, :]
    ln1 = ln1_ref[...]
    x1 = _ln(alpha * x_ref[0] + g1 * mix) * ln1[0:1, :] + ln1[1:2, :]
    u2 = _ln(x1) * (1.0 + sc2) + sh2
    t = u2.astype(BF16)
    shared = _dot((_silu(_dot(t, wsg_ref[...])) * _dot(t, wsu_ref[...])).astype(BF16), wsd_ref[...])
    z_ref[0] = alpha * x1 + mod[5:6, :] * shared

    u_hi, u_lo = _split_bf16(u2)
    both = _dot(u_hi, wr_ref[...])
    logits = both[:, 0:LANES] + both[:, LANES:] + _dot(u_lo, wr_ref[:, 0:LANES])
    aff = _sigmoid(logits.T[0:N_EXPERTS, :])
    biased = aff + rb_ref[...]
    grp = biased.reshape(N_GROUPS, GROUP_SIZE, tm)
    sub = lax.broadcasted_iota(jnp.int32, grp.shape, 1)
    m1 = jnp.max(grp, axis=1, keepdims=True)
    first = jnp.min(jnp.where(grp == m1, sub, GROUP_SIZE), axis=1, keepdims=True)
    m2 = jnp.max(jnp.where(sub == first, NEG_INF, grp), axis=1, keepdims=True)
    gscore = (m1 + m2).reshape(N_GROUPS, tm)
    g_i = lax.broadcasted_iota(jnp.int32, (N_GROUPS, tm), 0)
    g_rank = jnp.zeros((N_GROUPS, tm), F32)
    for g in range(N_GROUPS):
        other = gscore[g:g + 1, :]
        beats = (other > gscore) | ((other == gscore) & (g < g_i))
        g_rank = g_rank + jnp.where(beats, 1.0, 0.0)
    g_keep = jnp.where(g_rank < TOPK_GROUPS, 1.0, 0.0).reshape(N_GROUPS, 1, tm)
    e_keep = jnp.broadcast_to(g_keep, (N_GROUPS, GROUP_SIZE, tm)).reshape(N_EXPERTS, tm)
    masked = jnp.where(e_keep > 0.5, biased, NEG_INF)
    e_i = lax.broadcasted_iota(jnp.int32, (N_EXPERTS, tm), 0)
    chosen = jnp.zeros((N_EXPERTS, tm), F32)
    picks = []
    for _ in range(TOPK_EXPERTS):
        best = jnp.max(masked, axis=0, keepdims=True)
        first = jnp.min(jnp.where(masked == best, e_i, N_EXPERTS), axis=0, keepdims=True)
        hit = e_i == first
        chosen = jnp.where(hit, 1.0, chosen)
        masked = jnp.where(hit, NEG_INF, masked)
        picks.append(first)
    top_aff = jnp.where(chosen > 0.5, aff, 0.0)
    comb_t = top_aff / jnp.sum(top_aff, axis=0, keepdims=True) * ROUTED_SCALE

    @pl.when((pl.program_id(0) == 0) & (pl.program_id(1) == 0))
    def _():
        seen_scr[...] = jnp.zeros_like(seen_scr)

    r_i = lax.broadcasted_iota(jnp.int32, (tm, tm), 0)
    c_i = lax.broadcasted_iota(jnp.int32, (tm, tm), 1)
    earlier = jnp.where(r_i < c_i, 1.0, 0.0).astype(BF16)
    arrival = _dot(chosen.astype(BF16), earlier) + seen_scr[:, 0:1]
    seen_scr[...] = seen_scr[...] + jnp.sum(chosen, axis=1, keepdims=True)
    seen_ref[...] = seen_scr[...]
    ranks, gates = [], []
    for first in picks:
        hit = e_i == first
        ranks.append(jnp.sum(jnp.where(hit, arrival, 0.0), axis=0, keepdims=True))
        gates.append(jnp.sum(jnp.where(hit, comb_t, 0.0), axis=0, keepdims=True))
    expk_ref[...] = jnp.concatenate(picks, axis=0)
    rankk_ref[...] = jnp.concatenate(ranks, axis=0).astype(jnp.int32)
    gate_rows = jnp.concatenate(gates + [jnp.zeros((LANES - TOPK_EXPERTS, tm), F32)], axis=0)
    gk_ref[0] = gate_rows.T
    _store_token_rows(u2t_ref, _pack_pairs(u2), tm)


def _merge(oa, ob, ga, gb, x, mod, wba, wbb, wo, ln1, wr, rb, wsg, wsu, wsd, tm, alpha):
    bsz, s, d = x.shape
    row = lambda b, i: (b, i, 0)
    per_batch = s // tm
    flat = lambda b, i: (b * per_batch + i, 0)
    cols = lambda b, i: (0, b * per_batch + i)
    w2 = lambda shape: pl.BlockSpec(shape, lambda b, i: (0,) * len(shape))
    return pl.pallas_call(
        functools.partial(_merge_kernel, alpha=alpha),
        out_shape=[jax.ShapeDtypeStruct((bsz, s, d), F32),
                   jax.ShapeDtypeStruct((bsz, s, LANES), F32),
                   jax.ShapeDtypeStruct((bsz * s * TILE_ROWS, LANES), jnp.uint32),
                   jax.ShapeDtypeStruct((TOPK_EXPERTS, bsz * s), jnp.int32),
                   jax.ShapeDtypeStruct((TOPK_EXPERTS, bsz * s), jnp.int32),
                   jax.ShapeDtypeStruct((N_EXPERTS, LANES), F32)],
        grid=(bsz, s // tm),
        in_specs=[
            pl.BlockSpec((1, tm, WIDTH), row), pl.BlockSpec((1, tm, WIDTH), row),
            pl.BlockSpec((1, tm, d), row), pl.BlockSpec((1, tm, d), row),
            pl.BlockSpec((1, tm, d), row),
            pl.BlockSpec((1, N_MOD, d), lambda b, i: (b, 0, 0)),
            w2(wba.shape), w2(wbb.shape), w2(wo.shape), w2(ln1.shape), w2(wr.shape), w2(rb.shape),
            w2(wsg.shape), w2(wsu.shape), w2(wsd.shape),
        ],
        out_specs=[pl.BlockSpec((1, tm, d), row),
                   pl.BlockSpec((1, tm, LANES), row),
                   pl.BlockSpec((tm * TILE_ROWS, LANES), flat),
                   pl.BlockSpec((TOPK_EXPERTS, tm), cols),
                   pl.BlockSpec((TOPK_EXPERTS, tm), cols),
                   pl.BlockSpec((N_EXPERTS, LANES), lambda b, i: (0, 0))],
        scratch_shapes=[pltpu.VMEM((N_EXPERTS, LANES), F32)],
        compiler_params=_cparams(("arbitrary", "arbitrary")),
        name="merge",
    )(oa, ob, ga, gb, x, mod, wba, wbb, wo, ln1, wr, rb, wsg, wsu, wsd)


def _silu(z):
    return z * _sigmoid(z)


def _sc_params():
    return pltpu.CompilerParams(use_tc_tiling_on_sc=False)


def _sc_mesh():
    return plsc.VectorSubcoreMesh(core_axis_name="c", subcore_axis_name="s",
                                  num_cores=SC_CORES, num_subcores=SC_SUBCORES)


def _sc_dispatch(x_tiles, dest, n_rows):
    n_tok = x_tiles.shape[0]
    n_k = dest.shape[0]
    mesh = _sc_mesh()

    @pl.kernel(out_type=jax.ShapeDtypeStruct((n_rows,) + x_tiles.shape[1:], x_tiles.dtype), mesh=mesh,
               scratch_types=[], compiler_params=_sc_params())
    def scatter_rows(x_hbm, *rest):
        idx_hbm, o_hbm = rest[:n_k], rest[n_k]

        def body(x_vmem, *idx_vmem):
            for k in range(n_k):
                pltpu.sync_copy(x_vmem, o_hbm.at[idx_vmem[k].at[0]])

        pltpu.emit_pipeline(
            body, grid=(n_tok // SC_WINDOW,),
            in_specs=[pl.BlockSpec((SC_WINDOW,) + x_tiles.shape[1:], lambda i: (i, 0, 0))]
            + [pl.BlockSpec((1, SC_WINDOW), lambda i: (0, i))] * n_k,
            out_specs=[], core_axis_name=("c", "s"), dimension_semantics=(pltpu.PARALLEL,),
        )(x_hbm, *idx_hbm)

    return scatter_rows(x_tiles, *[dest[k:k + 1] for k in range(n_k)])


def _sc_gather(rows, idx):
    n = idx.shape[1]
    mesh = _sc_mesh()

    @pl.kernel(out_type=jax.ShapeDtypeStruct((n,) + rows.shape[1:], rows.dtype), mesh=mesh,
               scratch_types=[], compiler_params=_sc_params())
    def gather_rows(r_hbm, i_hbm, o_hbm):
        def body(i_vmem, o_vmem):
            pltpu.sync_copy(r_hbm.at[i_vmem.at[0]], o_vmem)

        pltpu.emit_pipeline(
            body, grid=(n // SC_WINDOW,),
            in_specs=[pl.BlockSpec((1, SC_WINDOW), lambda i: (0, i))],
            out_specs=[pl.BlockSpec((SC_WINDOW,) + rows.shape[1:], lambda i: (i, 0, 0))],
            core_axis_name=("c", "s"), dimension_semantics=(pltpu.PARALLEL,),
        )(i_hbm, o_hbm)

    return gather_rows(rows, idx)


def _ffn_grouped_kernel(expert_ref, valid_ref, used_ref, x_ref, wg_ref, wu_ref, wd_ref, y_ref,
                        wg_bf, wu_bf, wd_bf):
    i = pl.program_id(0)
    n_rows = x_ref.shape[0] // TILE_ROWS
    in_use = i < used_ref[0]
    new_expert = (i == 0) | (expert_ref[i] != expert_ref[jnp.maximum(i - 1, 0)])

    @pl.when(in_use & new_expert)
    def _():
        wg_bf[...] = wg_ref[0].astype(BF16)
        wu_bf[...] = wu_ref[0].astype(BF16)
        wd_bf[...] = wd_ref[0].astype(BF16)

    @pl.when(in_use)
    def _():
        row = lax.broadcasted_iota(jnp.int32, (n_rows, 1), 0)
        words = jnp.where(row < valid_ref[i], _load_token_rows(x_ref, n_rows), jnp.uint32(0))
        x = _unpack_pairs(words).astype(BF16)
        h = _silu(_dot(x, wg_bf[...])) * _dot(x, wu_bf[...])
        y = _dot(h.astype(BF16), wd_bf[...])
        _store_token_rows(y_ref, _pack_pairs(y), n_rows)


def _ffn_grouped(tile_expert, tile_valid, tiles_used, xs, wg, wu, wd, n_tiles):
    _, d, d_exp = wg.shape
    blk = pl.BlockSpec((MOE_ROW_TILE * TILE_ROWS, LANES), lambda i, te, tv, nu: (i, 0))
    weight = lambda shape: pl.BlockSpec(shape, lambda i, te, tv, nu: (te[i], 0, 0))
    return pl.pallas_call(
        _ffn_grouped_kernel,
        out_shape=jax.ShapeDtypeStruct(xs.shape, xs.dtype),
        grid_spec=pltpu.PrefetchScalarGridSpec(
            num_scalar_prefetch=3, grid=(n_tiles,),
            in_specs=[blk, weight((1, d, d_exp)), weight((1, d, d_exp)), weight((1, d_exp, d))],
            out_specs=blk,
            scratch_shapes=[pltpu.VMEM((d, d_exp), BF16), pltpu.VMEM((d, d_exp), BF16),
                            pltpu.VMEM((d_exp, d), BF16)]),
        compiler_params=_cparams(("arbitrary",)),
        name="moe_ffn",
    )(tile_expert, tile_valid, tiles_used, xs, wg, wu, wd)


def _moe_final_kernel(yg_ref, gk_ref, z_ref, mod_ref, ln2_ref, o_ref):
    tm = z_ref.shape[0]
    gk = gk_ref[...]
    routed = jnp.zeros(z_ref.shape, F32)
    for k in range(TOPK_EXPERTS):
        routed = routed + _unpack_pairs(_load_token_rows(yg_ref, tm, (k,))) * gk[:, k:k + 1]
    g2 = mod_ref[0][5:6, :]
    ln2 = ln2_ref[...]
    o_ref[...] = _ln(z_ref[...] + g2 * routed) * ln2[0:1, :] + ln2[1:2, :]


def _moe_final(yg, gk, z, mod, ln2, tm, tokens_per_batch, tok0):
    t, d = z.shape
    n_tok = yg.shape[1] // TILE_ROWS
    blocks_per_batch = tokens_per_batch // tm
    blk0 = tok0 // tm
    row = lambda i: (blk0 + i, 0)
    return pl.pallas_call(
        _moe_final_kernel,
        out_shape=jax.ShapeDtypeStruct((t, d), F32),
        grid=(n_tok // tm,),
        in_specs=[
            pl.BlockSpec((TOPK_EXPERTS, tm * TILE_ROWS, LANES), lambda i: (0, i, 0)),
            pl.BlockSpec((tm, LANES), row),
            pl.BlockSpec((tm, d), row),
            pl.BlockSpec((1, N_MOD, d), lambda i: ((blk0 + i) // blocks_per_batch, 0, 0)),
            pl.BlockSpec(ln2.shape, lambda i: (0, 0)),
        ],
        out_specs=pl.BlockSpec((tm, d), row),
        input_output_aliases={2: 0},
        compiler_params=_cparams(("arbitrary",)),
        name="moe_final",
    )(yg, gk, z, mod, ln2)


def _sorted_layout(expk, rankk, counts, n_tiles):
    e_ids = jnp.arange(N_EXPERTS, dtype=jnp.int32)[:, None, None]
    is_e = expk[None] == e_ids
    tiles = (counts + MOE_ROW_TILE - 1) // MOE_ROW_TILE
    tile_end = jnp.cumsum(tiles)
    first_row = (tile_end - tiles) * MOE_ROW_TILE
    dest = jnp.sum(jnp.where(is_e, first_row[:, None, None], 0), axis=0, dtype=jnp.int32) + rankk
    tile_ids = jnp.arange(n_tiles, dtype=jnp.int32)[:, None]
    done = tile_ids >= tile_end[None, :]
    owner = ~done & (tile_ids >= (tile_end - tiles)[None, :])
    tile_expert = jnp.minimum(jnp.sum(done, axis=1, dtype=jnp.int32), N_EXPERTS - 1)
    rows_before = (tile_ids[:, 0] - jnp.sum(jnp.where(done, tiles[None, :], 0), axis=1)) * MOE_ROW_TILE
    owner_count = jnp.sum(jnp.where(owner, counts[None, :], 0), axis=1)
    tile_valid = jnp.clip(owner_count - rows_before, 0, MOE_ROW_TILE)
    return dest, tile_expert, tile_valid, tile_end[-1:]


def _rope_tables(positions):
    inv_freq = ROPE_THETA ** (-jnp.arange(0, ROPE_DIM, 2, dtype=F32) / ROPE_DIM)
    ang = positions.astype(F32)[..., None] * inv_freq
    cos, sin = jnp.cos(ang), jnp.sin(ang)
    ones = jnp.ones(cos.shape[:-1] + (HEAD_DIM - ROPE_DIM,), F32)
    zeros = jnp.zeros_like(ones)
    zh = jnp.zeros_like(sin)
    c = jnp.concatenate([cos, cos, ones], axis=-1)
    s1 = jnp.concatenate([-sin, zh, zeros], axis=-1)
    s2 = jnp.concatenate([zh, sin, zeros], axis=-1)
    reps = LANES // HEAD_DIM
    return jnp.concatenate([jnp.tile(c, reps), jnp.tile(s1, reps), jnp.tile(s2, reps)], axis=-1)


def _band_bias_profile(rel_bias, tq):
    n_heads = rel_bias.shape[0]
    win = tq + BAND_PAD
    n_high = BAND_PAD - MAX_REL + 1
    n_ramp = 2 * MAX_REL - 1
    high = rel_bias[:, 2 * MAX_REL:]
    return jnp.concatenate([
        jnp.broadcast_to(high, (n_heads, n_high)),
        rel_bias[:, n_ramp:0:-1],
        jnp.broadcast_to(rel_bias[:, :1], (n_heads, win - n_high - n_ramp)),
        jnp.broadcast_to(high, (n_heads, tq)),
    ], axis=1)


def kernel(x, c, positions, w_ada, b_ada, w_in, rel_bias, idx_k_norm_g, idx_k_norm_b, w_branch_a,
           w_branch_b, w_out, ln1_g, ln1_b, w_router, router_bias, w_exp_gate, w_exp_up, w_exp_down,
           w_sh_gate, w_sh_up, w_sh_down, ln2_g, ln2_b):
    bsz, s, d = x.shape
    depth = w_ada.shape[0]
    alpha = (2.0 * depth) ** 0.25
    tm_proj = min(1024, s)
    tq_band = min(256, s)
    tq_dsa = min(256, s)
    tm_merge = min(512, s)
    tm_moe = min(512, s)

    rope_tab = _rope_tables(positions)
    n7 = 7 * WIDTH
    for l in range(depth):
        mod = _ada(c, w_ada[l], b_ada[l]).reshape(bsz, N_MOD, d)
        w_l = w_in[l]
        w7 = w_l[:, :n7].astype(BF16)
        n_kw = IDX_DIM + N_IDX_HEADS
        wkw = jnp.pad(w_l[:, n7:n7 + n_kw], ((0, 0), (0, LANES - n_kw))).astype(BF16)
        wg = w_l[:, n7 + n_kw:].astype(BF16)
        lnk = jnp.pad(jnp.stack([idx_k_norm_g[l], idx_k_norm_b[l]]), ((0, 0), (0, LANES - IDX_DIM)))
        qa, ka, va, qb, kb, vb, qi, ki, wi, ga, gb = _proj(x, mod, rope_tab, w7, wkw, wg, lnk, tm_proj)

        oa = _band(qa, ka, va, _band_bias_profile(rel_bias[l], tq_band), tq_band)
        ob = _dsa(qi, wi, qb, ki, kb, vb, tq_dsa)

        wr = jnp.pad(w_router[l], ((0, 0), (0, LANES - N_EXPERTS)))
        wr_hi = wr.astype(BF16)
        wr_lo = (wr - wr_hi.astype(F32)).astype(BF16)
        z, gk, u2_tiles, expk, rankk, seen = _merge(
            oa, ob, ga, gb, x, mod,
            w_branch_a[l].astype(BF16), w_branch_b[l].astype(BF16), w_out[l].astype(BF16),
            jnp.stack([ln1_g[l], ln1_b[l]]), jnp.concatenate([wr_hi, wr_lo], axis=1),
            router_bias[l].reshape(N_EXPERTS, 1),
            w_sh_gate[l].astype(BF16), w_sh_up[l].astype(BF16), w_sh_down[l].astype(BF16), tm_merge, alpha)

        t = bsz * s
        n_tiles = t * TOPK_EXPERTS // MOE_ROW_TILE + N_EXPERTS
        n_rows = n_tiles * MOE_ROW_TILE
        dest, tile_expert, tile_valid, tiles_used = _sorted_layout(
            expk, rankk, seen[:, 0].astype(jnp.int32), n_tiles)
        xs = _sc_dispatch(u2_tiles.reshape(t, TILE_ROWS, LANES), dest, n_rows)
        ys = _ffn_grouped(tile_expert, tile_valid, tiles_used, xs.reshape(n_rows * TILE_ROWS, LANES),
                          w_exp_gate[l], w_exp_up[l], w_exp_down[l], n_tiles)
        ys = ys.reshape(n_rows, TILE_ROWS, LANES)
        ln2 = jnp.stack([ln2_g[l], ln2_b[l]])
        chunk = t // MOE_COMBINE_CHUNKS
        out = z.reshape(t, d)
        for c in range(MOE_COMBINE_CHUNKS):
            idx = dest[:, c * chunk:(c + 1) * chunk].reshape(1, TOPK_EXPERTS * chunk)
            yg = _sc_gather(ys, idx).reshape(TOPK_EXPERTS, chunk * TILE_ROWS, LANES)
            out = _moe_final(yg, gk.reshape(t, LANES), out, mod, ln2, tm_moe, s, c * chunk)
        x = out.reshape(bsz, s, d)
    return x
```

```python
import functools

import jax
import jax.numpy as jnp
import numpy as np
from jax import lax
from jax.experimental import pallas as pl
from jax.experimental.pallas import tpu as pltpu
from jax.experimental.pallas import tpu_sc as plsc

F32 = jnp.float32
BF16 = jnp.bfloat16
NEG_INF = float("-inf")

CHUNK = 64
HEAD_DIM = 64
N_HEADS = 8
WIDTH = N_HEADS * HEAD_DIM
BAND_CHUNKS = 9
BAND_PAD = (BAND_CHUNKS - 1) * CHUNK
MAX_REL = 128
ROPE_THETA = 500000.0
ROPE_DIM = HEAD_DIM // 4
ROPE_HALF = ROPE_DIM // 2
N_IDX_HEADS = 8
IDX_DIM = 64
TOPK_MAX = 256
N_EXPERTS = 64
N_GROUPS = 8
GROUP_SIZE = N_EXPERTS // N_GROUPS
TOPK_GROUPS = 4
TOPK_EXPERTS = 8
ROUTED_SCALE = 1.0
N_MOD = 6
LN_EPS = 1e-5
LANES = 128
KEY_TILE = 256

VMEM_LIMIT = 56 * 1024 * 1024

DSA_KEY_CLASSES = 8
TILE_ROWS = 4
MOE_ROW_TILE = 1024
SC_WINDOW = 64
SC_CORES = 2
SC_SUBCORES = 16


def _cparams(sem):
    return pltpu.CompilerParams(dimension_semantics=sem, vmem_limit_bytes=VMEM_LIMIT)


def _ln(z):
    mu = jnp.mean(z, axis=-1, keepdims=True)
    zc = z - mu
    var = jnp.mean(zc * zc, axis=-1, keepdims=True)
    return zc * lax.rsqrt(var + LN_EPS)


def _sigmoid(z):
    return 1.0 / (1.0 + jnp.exp(-z))


def _dot(a, b):
    return jnp.dot(a, b, preferred_element_type=F32)


def _dot_nt(a, b):
    return lax.dot_general(a, b, (((1,), (1,)), ((), ())), preferred_element_type=F32)


def _split_bf16(z):
    hi = z.astype(BF16)
    lo = (z - hi.astype(F32)).astype(BF16)
    return hi, lo


HIGH_HALF = 0xFFFF0000


def _pack_pairs(z):
    half = z.shape[1] // 2

    def bits(v):
        return lax.bitcast_convert_type(v.astype(BF16).astype(F32), jnp.uint32)

    return (bits(z[:, :half]) >> 16) | (bits(z[:, half:]) & jnp.uint32(HIGH_HALF))


def _unpack_pairs(p):
    low = lax.bitcast_convert_type(p << 16, F32)
    high = lax.bitcast_convert_type(p & jnp.uint32(HIGH_HALF), F32)
    return jnp.concatenate([low, high], axis=1)


def _store_token_rows(ref, words, n_tok):
    for q in range(TILE_ROWS):
        ref[pl.ds(q, n_tok, stride=TILE_ROWS), :] = words[:, q * LANES:(q + 1) * LANES]


def _load_token_rows(ref, n_tok, lead=()):
    return jnp.concatenate([ref[lead + (pl.ds(q, n_tok, stride=TILE_ROWS), slice(None))]
                            for q in range(TILE_ROWS)], axis=1)


def _ada_kernel(c_ref, w_ref, b_ref, o_ref):
    c = c_ref[...]
    ca = c * _sigmoid(c)
    ca_hi, ca_lo = _split_bf16(ca)
    w = w_ref[...]
    w_hi, w_lo = _split_bf16(w)
    acc = _dot(ca_hi, w_hi) + _dot(ca_lo, w_hi) + _dot(ca_hi, w_lo)
    o_ref[...] = acc + b_ref[...]


def _ada(c, w_ada, b_ada):
    bsz, d = c.shape
    n = w_ada.shape[1]
    tn = 1024
    return pl.pallas_call(
        _ada_kernel,
        out_shape=jax.ShapeDtypeStruct((bsz, n), F32),
        grid=(n // tn,),
        in_specs=[
            pl.BlockSpec((bsz, d), lambda j: (0, 0)),
            pl.BlockSpec((d, tn), lambda j: (0, j)),
            pl.BlockSpec((1, tn), lambda j: (0, j)),
        ],
        out_specs=pl.BlockSpec((bsz, tn), lambda j: (0, j)),
        compiler_params=_cparams(("arbitrary",)),
        name="ada",
    )(c, w_ada, b_ada.reshape(1, n))


def _rope(z, c_t, s1_t, s2_t):
    n = z.shape[-1]
    return z * c_t + pltpu.roll(z, n - ROPE_HALF, 1) * s1_t + pltpu.roll(z, ROPE_HALF, 1) * s2_t


def _proj_kernel(x_ref, mod_ref, rope_ref, w7_ref, wkw_ref, wg_ref, lnk_ref,
                 qa_ref, ka_ref, va_ref, qb_ref, kb_ref, vb_ref, qi_ref, ki_ref, wi_ref,
                 ga_ref, gb_ref):
    x = x_ref[0]
    mod = mod_ref[0]
    sh1 = mod[0:1, :]
    sc1 = mod[1:2, :]
    u = (_ln(x) * (1.0 + sc1) + sh1).astype(BF16)

    rope = rope_ref[0]
    c1 = rope[:, 0:LANES]
    s1 = rope[:, LANES:2 * LANES]
    s2 = rope[:, 2 * LANES:3 * LANES]
    reps = WIDTH // LANES
    c_t = jnp.concatenate([c1] * reps, axis=1)
    s1_t = jnp.concatenate([s1] * reps, axis=1)
    s2_t = jnp.concatenate([s2] * reps, axis=1)

    att_scale = HEAD_DIM ** -0.5
    idx_scale = IDX_DIM ** -0.5

    def seg(k):
        return _dot(u, w7_ref[:, k * WIDTH:(k + 1) * WIDTH])

    qa_ref[0] = (seg(0) * att_scale).astype(BF16)
    ka_ref[0] = seg(1).astype(BF16)
    va_ref[0] = seg(2).astype(BF16)
    qb_ref[0] = (_rope(seg(3), c_t, s1_t, s2_t) * att_scale).astype(BF16)
    kb_ref[0] = _rope(seg(4), c_t, s1_t, s2_t).astype(BF16)
    vb_ref[0] = seg(5).astype(BF16)
    qi_ref[0] = (_rope(seg(6), c_t, s1_t, s2_t) * idx_scale).astype(BF16)

    z = _dot(u, wkw_ref[...])
    lane = lax.broadcasted_iota(jnp.int32, z.shape, 1)
    is_k = lane < IDX_DIM
    mu = jnp.sum(jnp.where(is_k, z, 0.0), axis=-1, keepdims=True) * (1.0 / IDX_DIM)
    zc = jnp.where(is_k, z - mu, 0.0)
    var = jnp.sum(zc * zc, axis=-1, keepdims=True) * (1.0 / IDX_DIM)
    lnk = lnk_ref[...]
    y = zc * lax.rsqrt(var + LN_EPS) * lnk[0:1, :] + lnk[1:2, :]
    y = _rope(y, c1, jnp.where(is_k, s1, 0.0), jnp.where(is_k, s2, 0.0))
    ki_ref[0] = y[:, 0:IDX_DIM].astype(BF16)
    wi_ref[0] = z.T[IDX_DIM:IDX_DIM + N_IDX_HEADS, :] * (N_IDX_HEADS ** -0.5)

    d = ga_ref.shape[-1]
    ga_ref[0] = _dot(u, wg_ref[:, 0:d]).astype(BF16)
    gb_ref[0] = _dot(u, wg_ref[:, d:2 * d]).astype(BF16)


def _proj(x, mod, rope_tab, w7, wkw, wg, lnk, tm):
    bsz, s, d = x.shape
    const = dict(pipeline_mode=pl.Buffered(1))
    row = lambda b, i: (b, i, 0)
    wspec = lambda shape: pl.BlockSpec(shape, lambda b, i: (0, 0), **const)
    out_w = jax.ShapeDtypeStruct((bsz, s, WIDTH), BF16)
    out_d = jax.ShapeDtypeStruct((bsz, s, d), BF16)
    return pl.pallas_call(
        _proj_kernel,
        out_shape=[out_w] * 7 + [
            jax.ShapeDtypeStruct((bsz, s, IDX_DIM), BF16),
            jax.ShapeDtypeStruct((bsz, N_IDX_HEADS, s), F32),
            out_d, out_d],
        grid=(bsz, s // tm),
        in_specs=[
            pl.BlockSpec((1, tm, d), row),
            pl.BlockSpec((1, N_MOD, d), lambda b, i: (b, 0, 0)),
            pl.BlockSpec((1, tm, 3 * LANES), row),
            wspec(w7.shape), wspec(wkw.shape), wspec(wg.shape), wspec(lnk.shape),
        ],
        out_specs=[pl.BlockSpec((1, tm, WIDTH), row)] * 7 + [
            pl.BlockSpec((1, tm, IDX_DIM), row),
            pl.BlockSpec((1, N_IDX_HEADS, tm), lambda b, i: (b, 0, i)),
            pl.BlockSpec((1, tm, d), row), pl.BlockSpec((1, tm, d), row)],
        compiler_params=_cparams(("arbitrary", "arbitrary")),
        name="proj",
    )(x, mod, rope_tab, w7, wkw, wg, lnk)


def _band_kernel(q_ref, k_ref, v_ref, prof_ref, o_ref, kpad, vpad, a_scr, p_scr, bias_ref, *, tq):
    i = pl.program_id(1)
    s = k_ref.shape[1]
    win = tq + BAND_PAD

    @pl.when((pl.program_id(0) == 0) & (i == 0))
    def _():
        q_chunk = lax.broadcasted_iota(jnp.int32, (tq, win), 0) // CHUNK
        k_chunk = lax.broadcasted_iota(jnp.int32, (tq, win), 1) // CHUNK - (BAND_CHUNKS - 1)
        in_band = (k_chunk <= q_chunk) & (k_chunk >= q_chunk - (BAND_CHUNKS - 1))
        for h in range(N_HEADS):
            rows = jnp.broadcast_to(prof_ref[h:h + 1, :], (tq, prof_ref.shape[1]))
            skew = pltpu.roll(rows, 0, 1, stride=1, stride_axis=0)
            bias_ref[h] = jnp.where(in_band, skew[:, 0:win], NEG_INF)

    @pl.when(i == 0)
    def _():
        zeros = jnp.zeros((BAND_PAD, WIDTH), BF16)
        kpad[0:BAND_PAD, :] = zeros
        vpad[0:BAND_PAD, :] = zeros
        kpad[BAND_PAD:BAND_PAD + s, :] = k_ref[0]
        vpad[BAND_PAD:BAND_PAD + s, :] = v_ref[0]

    q0 = pl.multiple_of(i * tq, tq)
    n_lg = win // LANES
    col = lax.broadcasted_iota(jnp.int32, (tq, LANES), 1)
    first_frame = BAND_PAD - q0
    heads = [slice(h * HEAD_DIM, (h + 1) * HEAD_DIM) for h in range(N_HEADS)]

    def lanes(g):
        return slice(g * LANES, (g + 1) * LANES)

    def scores(mask_padding):
        row_max = []
        for h, sl in enumerate(heads):
            sc = _dot_nt(q_ref[0, :, sl], kpad[pl.ds(q0, win), sl])
            mx = jnp.full((tq, LANES), NEG_INF, F32)
            for g in range(n_lg):
                a = sc[:, lanes(g)] + bias_ref[h, :, lanes(g)]
                if mask_padding:
                    a = jnp.where(col >= first_frame - g * LANES, a, NEG_INF)
                a_scr[h, :, lanes(g)] = a
                mx = jnp.maximum(mx, a)
            row_max.append(jnp.max(mx, axis=-1, keepdims=True))
        return row_max

    row_max = lax.cond(first_frame > 0, lambda: scores(True), lambda: scores(False))
    row_sum = []
    for h in range(N_HEADS):
        m_b = jnp.broadcast_to(row_max[h], (tq, LANES))
        ls = jnp.zeros((tq, LANES), F32)
        for g in range(n_lg):
            p = jnp.exp(a_scr[h, :, lanes(g)] - m_b)
            ls = ls + p
            p_scr[h, :, lanes(g)] = p.astype(BF16)
        row_sum.append(jnp.sum(ls, axis=-1, keepdims=True))
    for h, sl in enumerate(heads):
        o = _dot(p_scr[h], vpad[pl.ds(q0, win), sl]) / row_sum[h]
        o_ref[0, :, sl] = o.astype(BF16)


def _band(qa, ka, va, profile, tq):
    bsz, s, _ = qa.shape
    win = tq + BAND_PAD
    return pl.pallas_call(
        functools.partial(_band_kernel, tq=tq),
        out_shape=jax.ShapeDtypeStruct((bsz, s, WIDTH), BF16),
        grid=(bsz, s // tq),
        in_specs=[
            pl.BlockSpec((1, tq, WIDTH), lambda b, i: (b, i, 0)),
            pl.BlockSpec((1, s, WIDTH), lambda b, i: (b, 0, 0)),
            pl.BlockSpec((1, s, WIDTH), lambda b, i: (b, 0, 0)),
            pl.BlockSpec(profile.shape, lambda b, i: (0, 0)),
        ],
        out_specs=pl.BlockSpec((1, tq, WIDTH), lambda b, i: (b, i, 0)),
        scratch_shapes=[pltpu.VMEM((BAND_PAD + s, WIDTH), BF16),
                        pltpu.VMEM((BAND_PAD + s, WIDTH), BF16),
                        pltpu.VMEM((N_HEADS, tq, win), F32),
                        pltpu.VMEM((N_HEADS, tq, win), BF16),
                        pltpu.VMEM((N_HEADS, tq, win), F32)],
        compiler_params=_cparams(("arbitrary", "arbitrary")),
        name="band",
    )(qa, ka, va, profile)


BISECT_STEPS_PER_CHECK = 4
BISECT_MAX_CHECKS = 400
COUNT_ROWS = 64


def _dsa_kernel(qi_ref, wit_ref, qb_ref, ki_ref, kb_ref, vb_ref, o_ref, sct_scr, msk_scr, a_scr, p_scr,
                *, tq, k_sel, q_base):
    i = pl.program_id(1)
    sk = ki_ref.shape[1]
    n_kt = sk // KEY_TILE
    n_lg = sk // LANES
    q0 = q_base + i * tq
    kf = float(k_sel)

    def lanes(g):
        return slice(g * LANES, (g + 1) * LANES)

    def slab_reduce(fn, combine, init):
        acc = jnp.full((COUNT_ROWS, tq), init, F32)
        for r in range(sk // COUNT_ROWS):
            acc = combine(acc, fn(sct_scr[r * COUNT_ROWS:(r + 1) * COUNT_ROWS, :]))
        return acc

    def count(pred):
        part = slab_reduce(lambda t: jnp.where(pred(t), 1.0, 0.0), jnp.add, 0.0)
        return jnp.sum(part, axis=0, keepdims=True)

    w_t = wit_ref[0]
    t_pos = q0 + lax.broadcasted_iota(jnp.int32, (1, tq), 1)
    limit = (t_pos // CHUNK + 1) * CHUNK
    key_in_tile = lax.broadcasted_iota(jnp.int32, (KEY_TILE, tq), 0)
    for kt in range(n_kt):
        ks = slice(kt * KEY_TILE, (kt + 1) * KEY_TILE)
        ki_t = ki_ref[0, ks, :]
        acc = jnp.zeros((KEY_TILE, tq), F32)
        for h in range(N_IDX_HEADS):
            lg = _dot_nt(ki_t, qi_ref[0, :, h * IDX_DIM:(h + 1) * IDX_DIM])
            acc = acc + jnp.maximum(lg, 0.0) * w_t[h:h + 1, :]
        sct_scr[ks, :] = jnp.where(key_in_tile < limit - kt * KEY_TILE, acc, NEG_INF)

    smax = jnp.max(slab_reduce(lambda t: t, jnp.maximum, NEG_INF), axis=0, keepdims=True)
    smin = jnp.min(slab_reduce(lambda t: jnp.where(t == NEG_INF, jnp.inf, t), jnp.minimum, jnp.inf),
                   axis=0, keepdims=True)
    n_adm = limit.astype(F32)
    c_max = count(lambda t: t >= smax)
    c_pos = count(lambda t: t > 0.0)
    c_nn = count(lambda t: t >= 0.0)
    zero = jnp.zeros_like(smax)
    at_zero = (c_pos < kf) & (c_nn >= kf)
    below_zero = c_nn < kf
    lo = jnp.where(at_zero | ~below_zero, zero, smin)
    clo = jnp.where(at_zero | ~below_zero, c_nn, n_adm)
    hi = jnp.where(at_zero | below_zero, zero, smax)
    few = n_adm <= kf
    lo = jnp.where(few, smin, lo)
    clo = jnp.where(few, n_adm, clo)
    top_tied = (c_max >= kf) & ~few
    lo = jnp.where(top_tied, smax, lo)
    clo = jnp.where(top_tied, c_max, clo)
    hi = jnp.where(top_tied, smax, hi)
    done0 = jnp.where(few | at_zero | top_tied, 1.0, 0.0)

    def bisect(carry):
        lo, hi, clo, done, it = carry
        for _ in range(BISECT_STEPS_PER_CHECK):
            mid = 0.5 * lo + 0.5 * hi
            stuck = (mid <= lo) | (mid >= hi)
            c = count(lambda t: t >= mid)
            ge = c >= kf
            lo = jnp.where(ge, mid, lo)
            clo = jnp.where(ge, c, clo)
            hi = jnp.where(ge, hi, mid)
            done = jnp.where(stuck | (clo <= kf), 1.0, done)
        return lo, hi, clo, done, it + 1

    def not_converged(carry):
        _, _, _, done, it = carry
        return (jnp.min(done) < 0.5) & (it < BISECT_MAX_CHECKS)

    lo, hi, clo, _, _ = lax.while_loop(not_converged, bisect, (lo, hi, clo, done0, jnp.int32(0)))

    thr = lo
    c_gt = count(lambda t: t > thr)
    c_eq = count(lambda t: t == thr)
    need = kf - c_gt
    tie_cut = jnp.max(c_gt + c_eq - kf) > 0.0

    def with_ties():
        r_i = lax.broadcasted_iota(jnp.int32, (KEY_TILE, KEY_TILE), 0)
        c_i = lax.broadcasted_iota(jnp.int32, (KEY_TILE, KEY_TILE), 1)
        earlier = jnp.where(c_i < r_i, 1.0, 0.0).astype(BF16)
        carry = jnp.zeros((1, tq), F32)
        for kt in range(n_kt):
            ks = slice(kt * KEY_TILE, (kt + 1) * KEY_TILE)
            sc = sct_scr[ks, :]
            eq = sc == thr
            eq_f = jnp.where(eq, 1.0, 0.0)
            rank = _dot(earlier, eq_f.astype(BF16)) + carry
            keep = (sc > thr) | (eq & (rank < need))
            msk_scr[:, ks] = jnp.where(keep, 0.0, NEG_INF).T
            carry = carry + jnp.sum(eq_f, axis=0, keepdims=True)

    def without_ties():
        for kt in range(n_kt):
            ks = slice(kt * KEY_TILE, (kt + 1) * KEY_TILE)
            msk_scr[:, ks] = jnp.where(sct_scr[ks, :] >= thr, 0.0, NEG_INF).T

    lax.cond(tie_cut, with_ties, without_ties)

    heads = [slice(h * HEAD_DIM, (h + 1) * HEAD_DIM) for h in range(N_HEADS)]
    row_max = []
    for h, sl in enumerate(heads):
        qh = qb_ref[0, :, sl]
        mx = jnp.full((tq, LANES), NEG_INF, F32)
        for kt in range(n_kt):
            ks = slice(kt * KEY_TILE, (kt + 1) * KEY_TILE)
            a = _dot_nt(qh, kb_ref[0, ks, sl]) + msk_scr[:, ks]
            a_scr[h, :, ks] = a
            for g in range(KEY_TILE // LANES):
                mx = jnp.maximum(mx, a[:, lanes(g)])
        row_max.append(jnp.max(mx, axis=-1, keepdims=True))
    row_sum = []
    for h in range(N_HEADS):
        m_b = jnp.broadcast_to(row_max[h], (tq, LANES))
        ls = jnp.zeros((tq, LANES), F32)
        for g in range(n_lg):
            p = jnp.exp(a_scr[h, :, lanes(g)] - m_b)
            ls = ls + p
            p_scr[h, :, lanes(g)] = p.astype(BF16)
        row_sum.append(jnp.sum(ls, axis=-1, keepdims=True))
    for h, sl in enumerate(heads):
        o = _dot(p_scr[h], vb_ref[0, :, sl]) / row_sum[h]
        o_ref[0, :, sl] = o.astype(BF16)


def _dsa_class(qi, wit, qb, ki, kb, vb, tq, k_sel, q_base, q_len, sk):
    bsz = qb.shape[0]
    blk0 = q_base // tq
    row = lambda b, i: (b, blk0 + i, 0)
    keys = lambda b, i: (b, 0, 0)
    return pl.pallas_call(
        functools.partial(_dsa_kernel, tq=tq, k_sel=k_sel, q_base=q_base),
        out_shape=jax.ShapeDtypeStruct((bsz, q_len, WIDTH), BF16),
        grid=(bsz, q_len // tq),
        in_specs=[
            pl.BlockSpec((1, tq, N_IDX_HEADS * IDX_DIM), row),
            pl.BlockSpec((1, N_IDX_HEADS, tq), lambda b, i: (b, 0, blk0 + i)),
            pl.BlockSpec((1, tq, WIDTH), row),
            pl.BlockSpec((1, sk, IDX_DIM), keys),
            pl.BlockSpec((1, sk, WIDTH), keys),
            pl.BlockSpec((1, sk, WIDTH), keys),
        ],
        out_specs=pl.BlockSpec((1, tq, WIDTH), lambda b, i: (b, i, 0)),
        scratch_shapes=[pltpu.VMEM((sk, tq), F32), pltpu.VMEM((tq, sk), F32),
                        pltpu.VMEM((N_HEADS, tq, sk), F32), pltpu.VMEM((N_HEADS, tq, sk), BF16)],
        compiler_params=_cparams(("arbitrary", "arbitrary")),
        name=f"dsa_k{sk}",
    )(qi, wit, qb, ki, kb, vb)


def _dsa(qi, wit, qb, ki, kb, vb, tq):
    s = qb.shape[1]
    k_sel = min(TOPK_MAX, s // 4)
    n_cls = max(1, min(DSA_KEY_CLASSES, s // KEY_TILE))
    q_len = s // n_cls
    outs = [_dsa_class(qi, wit, qb, ki, kb, vb, tq, k_sel, c * q_len, q_len, (c + 1) * q_len)
            for c in range(n_cls)]
    return jnp.concatenate(outs, axis=1)


def _merge_kernel(oa_ref, ob_ref, ga_ref, gb_ref, x_ref, mod_ref, wba_ref, wbb_ref, wo_ref,
                  ln1_ref, wr_ref, rb_ref, wsg_ref, wsu_ref, wsd_ref, z_ref, gk_ref, u2t_ref, expk_ref,
                  rankk_ref, seen_ref, seen_scr, *, alpha):
    tm = x_ref.shape[1]
    ya = _dot(oa_ref[0], wba_ref[...])
    yb = _dot(ob_ref[0], wbb_ref[...])
    merged = _sigmoid(ga_ref[0].astype(F32)) * ya + _sigmoid(gb_ref[0].astype(F32)) * yb
    mix = _dot(merged.astype(BF16), wo_ref[...])
    mod = mod_ref[0]
    g1 = mod[2:3, :]
    sh2 = mod[3:4, :]
    sc2 = mod[4:5, :]
    ln1 = ln1_ref[...]
    x1 = _ln(alpha * x_ref[0] + g1 * mix) * ln1[0:1, :] + ln1[1:2, :]
    u2 = _ln(x1) * (1.0 + sc2) + sh2
    t = u2.astype(BF16)
    shared = _dot((_silu(_dot(t, wsg_ref[...])) * _dot(t, wsu_ref[...])).astype(BF16), wsd_ref[...])
    z_ref[0] = alpha * x1 + mod[5:6, :] * shared

    u_hi, u_lo = _split_bf16(u2)
    both = _dot(u_hi, wr_ref[...])
    logits = both[:, 0:LANES] + both[:, LANES:] + _dot(u_lo, wr_ref[:, 0:LANES])
    aff = _sigmoid(logits.T[0:N_EXPERTS, :])
    biased = aff + rb_ref[...]
    grp = biased.reshape(N_GROUPS, GROUP_SIZE, tm)
    sub = lax.broadcasted_iota(jnp.int32, grp.shape, 1)
    m1 = jnp.max(grp, axis=1, keepdims=True)
    first = jnp.min(jnp.where(grp == m1, sub, GROUP_SIZE), axis=1, keepdims=True)
    m2 = jnp.max(jnp.where(sub == first, NEG_INF, grp), axis=1, keepdims=True)
    gscore = (m1 + m2).reshape(N_GROUPS, tm)
    g_i = lax.broadcasted_iota(jnp.int32, (N_GROUPS, tm), 0)
    g_rank = jnp.zeros((N_GROUPS, tm), F32)
    for g in range(N_GROUPS):
        other = gscore[g:g + 1, :]
        beats = (other > gscore) | ((other == gscore) & (g < g_i))
        g_rank = g_rank + jnp.where(beats, 1.0, 0.0)
    g_keep = jnp.where(g_rank < TOPK_GROUPS, 1.0, 0.0).reshape(N_GROUPS, 1, tm)
    e_keep = jnp.broadcast_to(g_keep, (N_GROUPS, GROUP_SIZE, tm)).reshape(N_EXPERTS, tm)
    masked = jnp.where(e_keep > 0.5, biased, NEG_INF)
    e_i = lax.broadcasted_iota(jnp.int32, (N_EXPERTS, tm), 0)
    chosen = jnp.zeros((N_EXPERTS, tm), F32)
    picks = []
    for _ in range(TOPK_EXPERTS):
        best = jnp.max(masked, axis=0, keepdims=True)
        first = jnp.min(jnp.where(masked == best, e_i, N_EXPERTS), axis=0, keepdims=True)
        hit = e_i == first
        chosen = jnp.where(hit, 1.0, chosen)
        masked = jnp.where(hit, NEG_INF, masked)
        picks.append(first)
    top_aff = jnp.where(chosen > 0.5, aff, 0.0)
    comb_t = top_aff / jnp.sum(top_aff, axis=0, keepdims=True) * ROUTED_SCALE

    @pl.when((pl.program_id(0) == 0) & (pl.program_id(1) == 0))
    def _():
        seen_scr[...] = jnp.zeros_like(seen_scr)

    r_i = lax.broadcasted_iota(jnp.int32, (tm, tm), 0)
    c_i = lax.broadcasted_iota(jnp.int32, (tm, tm), 1)
    earlier = jnp.where(r_i < c_i, 1.0, 0.0).astype(BF16)
    arrival = _dot(chosen.astype(BF16), earlier) + seen_scr[:, 0:1]
    seen_scr[...] = seen_scr[...] + jnp.sum(chosen, axis=1, keepdims=True)
    seen_ref[...] = seen_scr[...]
    ranks, gates = [], []
    for first in picks:
        hit = e_i == first
        ranks.append(jnp.sum(jnp.where(hit, arrival, 0.0), axis=0, keepdims=True))
        gates.append(jnp.sum(jnp.where(hit, comb_t, 0.0), axis=0, keepdims=True))
    expk_ref[...] = jnp.concatenate(picks, axis=0)
    rankk_ref[...] = jnp.concatenate(ranks, axis=0).astype(jnp.int32)
    gate_rows = jnp.concatenate(gates + [jnp.zeros((LANES - TOPK_EXPERTS, tm), F32)], axis=0)
    gk_ref[0] = gate_rows.T
    _store_token_rows(u2t_ref, _pack_pairs(u2), tm)


def _merge(oa, ob, ga, gb, x, mod, wba, wbb, wo, ln1, wr, rb, wsg, wsu, wsd, tm, alpha):
    bsz, s, d = x.shape
    row = lambda b, i: (b, i, 0)
    per_batch = s // tm
    flat = lambda b, i: (b * per_batch + i, 0)
    cols = lambda b, i: (0, b * per_batch + i)
    w2 = lambda shape: pl.BlockSpec(shape, lambda b, i: (0,) * len(shape))
    return pl.pallas_call(
        functools.partial(_merge_kernel, alpha=alpha),
        out_shape=[jax.ShapeDtypeStruct((bsz, s, d), F32),
                   jax.ShapeDtypeStruct((bsz, s, LANES), F32),
                   jax.ShapeDtypeStruct((bsz * s * TILE_ROWS, LANES), jnp.uint32),
                   jax.ShapeDtypeStruct((TOPK_EXPERTS, bsz * s), jnp.int32),
                   jax.ShapeDtypeStruct((TOPK_EXPERTS, bsz * s), jnp.int32),
                   jax.ShapeDtypeStruct((N_EXPERTS, LANES), F32)],
        grid=(bsz, s // tm),
        in_specs=[
            pl.BlockSpec((1, tm, WIDTH), row), pl.BlockSpec((1, tm, WIDTH), row),
            pl.BlockSpec((1, tm, d), row), pl.BlockSpec((1, tm, d), row),
            pl.BlockSpec((1, tm, d), row),
            pl.BlockSpec((1, N_MOD, d), lambda b, i: (b, 0, 0)),
            w2(wba.shape), w2(wbb.shape), w2(wo.shape), w2(ln1.shape), w2(wr.shape), w2(rb.shape),
            w2(wsg.shape), w2(wsu.shape), w2(wsd.shape),
        ],
        out_specs=[pl.BlockSpec((1, tm, d), row),
                   pl.BlockSpec((1, tm, LANES), row),
                   pl.BlockSpec((tm * TILE_ROWS, LANES), flat),
                   pl.BlockSpec((TOPK_EXPERTS, tm), cols),
                   pl.BlockSpec((TOPK_EXPERTS, tm), cols),
                   pl.BlockSpec((N_EXPERTS, LANES), lambda b, i: (0, 0))],
        scratch_shapes=[pltpu.VMEM((N_EXPERTS, LANES), F32)],
        compiler_params=_cparams(("arbitrary", "arbitrary")),
        name="merge",
    )(oa, ob, ga, gb, x, mod, wba, wbb, wo, ln1, wr, rb, wsg, wsu, wsd)


def _silu(z):
    return z * _sigmoid(z)


def _sc_params():
    return pltpu.CompilerParams(use_tc_tiling_on_sc=False)


def _sc_mesh():
    return plsc.VectorSubcoreMesh(core_axis_name="c", subcore_axis_name="s",
                                  num_cores=SC_CORES, num_subcores=SC_SUBCORES)


def _sc_dispatch(x_tiles, dest, n_rows):
    n_tok = x_tiles.shape[0]
    n_k = dest.shape[0]
    mesh = _sc_mesh()

    @pl.kernel(out_type=jax.ShapeDtypeStruct((n_rows,) + x_tiles.shape[1:], x_tiles.dtype), mesh=mesh,
               scratch_types=[], compiler_params=_sc_params())
    def scatter_rows(x_hbm, *rest):
        idx_hbm, o_hbm = rest[:n_k], rest[n_k]

        def body(x_vmem, *idx_vmem):
            for k in range(n_k):
                pltpu.sync_copy(x_vmem, o_hbm.at[idx_vmem[k].at[0]])

        pltpu.emit_pipeline(
            body, grid=(n_tok // SC_WINDOW,),
            in_specs=[pl.BlockSpec((SC_WINDOW,) + x_tiles.shape[1:], lambda i: (i, 0, 0))]
            + [pl.BlockSpec((1, SC_WINDOW), lambda i: (0, i))] * n_k,
            out_specs=[], core_axis_name=("c", "s"), dimension_semantics=(pltpu.PARALLEL,),
        )(x_hbm, *idx_hbm)

    return scatter_rows(x_tiles, *[dest[k:k + 1] for k in range(n_k)])


def _sc_gather(rows, idx):
    n = idx.shape[1]
    mesh = _sc_mesh()

    @pl.kernel(out_type=jax.ShapeDtypeStruct((n,) + rows.shape[1:], rows.dtype), mesh=mesh,
               scratch_types=[], compiler_params=_sc_params())
    def gather_rows(r_hbm, i_hbm, o_hbm):
        def body(i_vmem, o_vmem):
            pltpu.sync_copy(r_hbm.at[i_vmem.at[0]], o_vmem)

        pltpu.emit_pipeline(
            body, grid=(n // SC_WINDOW,),
            in_specs=[pl.BlockSpec((1, SC_WINDOW), lambda i: (0, i))],
            out_specs=[pl.BlockSpec((SC_WINDOW,) + rows.shape[1:], lambda i: (i, 0, 0))],
            core_axis_name=("c", "s"), dimension_semantics=(pltpu.PARALLEL,),
        )(i_hbm, o_hbm)

    return gather_rows(rows, idx)


def _ffn_grouped_kernel(expert_ref, valid_ref, used_ref, x_ref, wg_ref, wu_ref, wd_ref, y_ref,
                        wg_bf, wu_bf, wd_bf):
    i = pl.program_id(0)
    n_rows = x_ref.shape[0] // TILE_ROWS
    in_use = i < used_ref[0]
    new_expert = (i == 0) | (expert_ref[i] != expert_ref[jnp.maximum(i - 1, 0)])

    @pl.when(in_use & new_expert)
    def _():
        wg_bf[...] = wg_ref[0].astype(BF16)
        wu_bf[...] = wu_ref[0].astype(BF16)
        wd_bf[...] = wd_ref[0].astype(BF16)

    @pl.when(in_use)
    def _():
        row = lax.broadcasted_iota(jnp.int32, (n_rows, 1), 0)
        words = jnp.where(row < valid_ref[i], _load_token_rows(x_ref, n_rows), jnp.uint32(0))
        x = _unpack_pairs(words).astype(BF16)
        h = _silu(_dot(x, wg_bf[...])) * _dot(x, wu_bf[...])
        y = _dot(h.astype(BF16), wd_bf[...])
        _store_token_rows(y_ref, _pack_pairs(y), n_rows)


def _ffn_grouped(tile_expert, tile_valid, tiles_used, xs, wg, wu, wd, n_tiles):
    _, d, d_exp = wg.shape
    blk = pl.BlockSpec((MOE_ROW_TILE * TILE_ROWS, LANES), lambda i, te, tv, nu: (i, 0))
    weight = lambda shape: pl.BlockSpec(shape, lambda i, te, tv, nu: (te[i], 0, 0))
    return pl.pallas_call(
        _ffn_grouped_kernel,
        out_shape=jax.ShapeDtypeStruct(xs.shape, xs.dtype),
        grid_spec=pltpu.PrefetchScalarGridSpec(
            num_scalar_prefetch=3, grid=(n_tiles,),
            in_specs=[blk, weight((1, d, d_exp)), weight((1, d, d_exp)), weight((1, d_exp, d))],
            out_specs=blk,
            scratch_shapes=[pltpu.VMEM((d, d_exp), BF16), pltpu.VMEM((d, d_exp), BF16),
                            pltpu.VMEM((d_exp, d), BF16)]),
        compiler_params=_cparams(("arbitrary",)),
        name="moe_ffn",
    )(tile_expert, tile_valid, tiles_used, xs, wg, wu, wd)


def _moe_final_kernel(yg_ref, gk_ref, z_ref, mod_ref, ln2_ref, o_ref):
    tm = z_ref.shape[0]
    gk = gk_ref[...]
    routed = jnp.zeros(z_ref.shape, F32)
    for k in range(TOPK_EXPERTS):
        routed = routed + _unpack_pairs(_load_token_rows(yg_ref, tm, (k,))) * gk[:, k:k + 1]
    g2 = mod_ref[0][5:6, :]
    ln2 = ln2_ref[...]
    o_ref[...] = _ln(z_ref[...] + g2 * routed) * ln2[0:1, :] + ln2[1:2, :]


def _moe_final(yg, gk, z, mod, ln2, tm, tokens_per_batch):
    t, d = z.shape
    blocks_per_batch = tokens_per_batch // tm
    row = lambda i: (i, 0)
    return pl.pallas_call(
        _moe_final_kernel,
        out_shape=jax.ShapeDtypeStruct((t, d), F32),
        grid=(t // tm,),
        in_specs=[
            pl.BlockSpec((TOPK_EXPERTS, tm * TILE_ROWS, LANES), lambda i: (0, i, 0)),
            pl.BlockSpec((tm, LANES), row),
            pl.BlockSpec((tm, d), row),
            pl.BlockSpec((1, N_MOD, d), lambda i: (i // blocks_per_batch, 0, 0)),
            pl.BlockSpec(ln2.shape, lambda i: (0, 0)),
        ],
        out_specs=pl.BlockSpec((tm, d), row),
        input_output_aliases={2: 0},
        compiler_params=_cparams(("arbitrary",)),
        name="moe_final",
    )(yg, gk, z, mod, ln2)


def _sorted_layout(expk, rankk, counts, n_tiles):
    e_ids = jnp.arange(N_EXPERTS, dtype=jnp.int32)[:, None, None]
    is_e = expk[None] == e_ids
    tiles = (counts + MOE_ROW_TILE - 1) // MOE_ROW_TILE
    tile_end = jnp.cumsum(tiles)
    first_row = (tile_end - tiles) * MOE_ROW_TILE
    dest = jnp.sum(jnp.where(is_e, first_row[:, None, None], 0), axis=0, dtype=jnp.int32) + rankk
    tile_ids = jnp.arange(n_tiles, dtype=jnp.int32)[:, None]
    done = tile_ids >= tile_end[None, :]
    owner = ~done & (tile_ids >= (tile_end - tiles)[None, :])
    tile_expert = jnp.minimum(jnp.sum(done, axis=1, dtype=jnp.int32), N_EXPERTS - 1)
    rows_before = (tile_ids[:, 0] - jnp.sum(jnp.where(done, tiles[None, :], 0), axis=1)) * MOE_ROW_TILE
    owner_count = jnp.sum(jnp.where(owner, counts[None, :], 0), axis=1)
    tile_valid = jnp.clip(owner_count - rows_before, 0, MOE_ROW_TILE)
    return dest, tile_expert, tile_valid, tile_end[-1:]


def _rope_tables(positions):
    inv_freq = ROPE_THETA ** (-jnp.arange(0, ROPE_DIM, 2, dtype=F32) / ROPE_DIM)
    ang = positions.astype(F32)[..., None] * inv_freq
    cos, sin = jnp.cos(ang), jnp.sin(ang)
    ones = jnp.ones(cos.shape[:-1] + (HEAD_DIM - ROPE_DIM,), F32)
    zeros = jnp.zeros_like(ones)
    zh = jnp.zeros_like(sin)
    c = jnp.concatenate([cos, cos, ones], axis=-1)
    s1 = jnp.concatenate([-sin, zh, zeros], axis=-1)
    s2 = jnp.concatenate([zh, sin, zeros], axis=-1)
    reps = LANES // HEAD_DIM
    return jnp.concatenate([jnp.tile(c, reps), jnp.tile(s1, reps), jnp.tile(s2, reps)], axis=-1)


def _band_bias_profile(rel_bias, tq):
    n_heads = rel_bias.shape[0]
    win = tq + BAND_PAD
    n_high = BAND_PAD - MAX_REL + 1
    n_ramp = 2 * MAX_REL - 1
    high = rel_bias[:, 2 * MAX_REL:]
    return jnp.concatenate([
        jnp.broadcast_to(high, (n_heads, n_high)),
        rel_bias[:, n_ramp:0:-1],
        jnp.broadcast_to(rel_bias[:, :1], (n_heads, win - n_high - n_ramp)),
        jnp.broadcast_to(high, (n_heads, tq)),
    ], axis=1)


def kernel(x, c, positions, w_ada, b_ada, w_in, rel_bias, idx_k_norm_g, idx_k_norm_b, w_branch_a,
           w_branch_b, w_out, ln1_g, ln1_b, w_router, router_bias, w_exp_gate, w_exp_up, w_exp_down,
           w_sh_gate, w_sh_up, w_sh_down, ln2_g, ln2_b):
    bsz, s, d = x.shape
    depth = w_ada.shape[0]
    alpha = (2.0 * depth) ** 0.25
    tm_proj = min(1024, s)
    tq_band = min(256, s)
    tq_dsa = min(256, s)
    tm_merge = min(512, s)
    tm_moe = min(512, s)

    rope_tab = _rope_tables(positions)
    n7 = 7 * WIDTH
    for l in range(depth):
        mod = _ada(c, w_ada[l], b_ada[l]).reshape(bsz, N_MOD, d)
        w_l = w_in[l]
        w7 = w_l[:, :n7].astype(BF16)
        n_kw = IDX_DIM + N_IDX_HEADS
        wkw = jnp.pad(w_l[:, n7:n7 + n_kw], ((0, 0), (0, LANES - n_kw))).astype(BF16)
        wg = w_l[:, n7 + n_kw:].astype(BF16)
        lnk = jnp.pad(jnp.stack([idx_k_norm_g[l], idx_k_norm_b[l]]), ((0, 0), (0, LANES - IDX_DIM)))
        qa, ka, va, qb, kb, vb, qi, ki, wi, ga, gb = _proj(x, mod, rope_tab, w7, wkw, wg, lnk, tm_proj)

        oa = _band(qa, ka, va, _band_bias_profile(rel_bias[l], tq_band), tq_band)
        ob = _dsa(qi, wi, qb, ki, kb, vb, tq_dsa)

        wr = jnp.pad(w_router[l], ((0, 0), (0, LANES - N_EXPERTS)))
        wr_hi = wr.astype(BF16)
        wr_lo = (wr - wr_hi.astype(F32)).astype(BF16)
        z, gk, u2_tiles, expk, rankk, seen = _merge(
            oa, ob, ga, gb, x, mod,
            w_branch_a[l].astype(BF16), w_branch_b[l].astype(BF16), w_out[l].astype(BF16),
            jnp.stack([ln1_g[l], ln1_b[l]]), jnp.concatenate([wr_hi, wr_lo], axis=1),
            router_bias[l].reshape(N_EXPERTS, 1),
            w_sh_gate[l].astype(BF16), w_sh_up[l].astype(BF16), w_sh_down[l].astype(BF16), tm_merge, alpha)

        t = bsz * s
        n_tiles = t * TOPK_EXPERTS // MOE_ROW_TILE + N_EXPERTS
        n_rows = n_tiles * MOE_ROW_TILE
        dest, tile_expert, tile_valid, tiles_used = _sorted_layout(
            expk, rankk, seen[:, 0].astype(jnp.int32), n_tiles)
        xs = _sc_dispatch(u2_tiles.reshape(t, TILE_ROWS, LANES), dest, n_rows)
        ys = _ffn_grouped(tile_expert, tile_valid, tiles_used, xs.reshape(n_rows * TILE_ROWS, LANES),
                          w_exp_gate[l], w_exp_up[l], w_exp_down[l], n_tiles)
        yg = _sc_gather(ys.reshape(n_rows, TILE_ROWS, LANES), dest.reshape(1, TOPK_EXPERTS * t))
        out = _moe_final(yg.reshape(TOPK_EXPERTS, t * TILE_ROWS, LANES), gk.reshape(t, LANES), z.reshape(t, d),
                         mod, jnp.stack([ln2_g[l], ln2_b[l]]), tm_moe, s)
        x = out.reshape(bsz, s, d)
    return x
```

```python
import functools

import jax
import jax.numpy as jnp
import numpy as np
from jax import lax
from jax.experimental import pallas as pl
from jax.experimental.pallas import tpu as pltpu
from jax.experimental.pallas import tpu_sc as plsc

F32 = jnp.float32
BF16 = jnp.bfloat16
NEG_INF = float("-inf")

CHUNK = 64
HEAD_DIM = 64
N_HEADS = 8
WIDTH = N_HEADS * HEAD_DIM
BAND_CHUNKS = 9
BAND_PAD = (BAND_CHUNKS - 1) * CHUNK
MAX_REL = 128
ROPE_THETA = 500000.0
ROPE_DIM = HEAD_DIM // 4
ROPE_HALF = ROPE_DIM // 2
N_IDX_HEADS = 8
IDX_DIM = 64
TOPK_MAX = 256
N_EXPERTS = 64
N_GROUPS = 8
GROUP_SIZE = N_EXPERTS // N_GROUPS
TOPK_GROUPS = 4
TOPK_EXPERTS = 8
ROUTED_SCALE = 1.0
N_MOD = 6
LN_EPS = 1e-5
LANES = 128
KEY_TILE = 256

VMEM_LIMIT = 56 * 1024 * 1024

DSA_KEY_CLASSES = 8
TILE_ROWS = 4
MOE_ROW_TILE = 1024
SC_WINDOW = 64
MOE_COMBINE_CHUNKS = 4
SC_CORES = 2
SC_SUBCORES = 16


def _cparams(sem):
    return pltpu.CompilerParams(dimension_semantics=sem, vmem_limit_bytes=VMEM_LIMIT)


def _ln(z):
    mu = jnp.mean(z, axis=-1, keepdims=True)
    zc = z - mu
    var = jnp.mean(zc * zc, axis=-1, keepdims=True)
    return zc * lax.rsqrt(var + LN_EPS)


def _sigmoid(z):
    return 1.0 / (1.0 + jnp.exp(-z))


def _dot(a, b):
    return jnp.dot(a, b, preferred_element_type=F32)


def _dot_nt(a, b):
    return lax.dot_general(a, b, (((1,), (1,)), ((), ())), preferred_element_type=F32)


def _split_bf16(z):
    hi = z.astype(BF16)
    lo = (z - hi.astype(F32)).astype(BF16)
    return hi, lo


HIGH_HALF = 0xFFFF0000


def _pack_pairs(z):
    half = z.shape[1] // 2

    def bits(v):
        return lax.bitcast_convert_type(v.astype(BF16).astype(F32), jnp.uint32)

    return (bits(z[:, :half]) >> 16) | (bits(z[:, half:]) & jnp.uint32(HIGH_HALF))


def _unpack_pairs(p):
    low = lax.bitcast_convert_type(p << 16, F32)
    high = lax.bitcast_convert_type(p & jnp.uint32(HIGH_HALF), F32)
    return jnp.concatenate([low, high], axis=1)


def _store_token_rows(ref, words, n_tok):
    for q in range(TILE_ROWS):
        ref[pl.ds(q, n_tok, stride=TILE_ROWS), :] = words[:, q * LANES:(q + 1) * LANES]


def _load_token_rows(ref, n_tok, lead=()):
    return jnp.concatenate([ref[lead + (pl.ds(q, n_tok, stride=TILE_ROWS), slice(None))]
                            for q in range(TILE_ROWS)], axis=1)


def _ada_kernel(c_ref, w_ref, b_ref, o_ref):
    c = c_ref[...]
    ca = c * _sigmoid(c)
    ca_hi, ca_lo = _split_bf16(ca)
    w = w_ref[...]
    w_hi, w_lo = _split_bf16(w)
    acc = _dot(ca_hi, w_hi) + _dot(ca_lo, w_hi) + _dot(ca_hi, w_lo)
    o_ref[...] = acc + b_ref[...]


def _ada(c, w_ada, b_ada):
    bsz, d = c.shape
    n = w_ada.shape[1]
    tn = 1024
    return pl.pallas_call(
        _ada_kernel,
        out_shape=jax.ShapeDtypeStruct((bsz, n), F32),
        grid=(n // tn,),
        in_specs=[
            pl.BlockSpec((bsz, d), lambda j: (0, 0)),
            pl.BlockSpec((d, tn), lambda j: (0, j)),
            pl.BlockSpec((1, tn), lambda j: (0, j)),
        ],
        out_specs=pl.BlockSpec((bsz, tn), lambda j: (0, j)),
        compiler_params=_cparams(("arbitrary",)),
        name="ada",
    )(c, w_ada, b_ada.reshape(1, n))


def _rope(z, c_t, s1_t, s2_t):
    n = z.shape[-1]
    return z * c_t + pltpu.roll(z, n - ROPE_HALF, 1) * s1_t + pltpu.roll(z, ROPE_HALF, 1) * s2_t


def _proj_kernel(x_ref, mod_ref, rope_ref, w7_ref, wkw_ref, wg_ref, lnk_ref,
                 qa_ref, ka_ref, va_ref, qb_ref, kb_ref, vb_ref, qi_ref, ki_ref, wi_ref,
                 ga_ref, gb_ref):
    x = x_ref[0]
    mod = mod_ref[0]
    sh1 = mod[0:1, :]
    sc1 = mod[1:2, :]
    u = (_ln(x) * (1.0 + sc1) + sh1).astype(BF16)

    rope = rope_ref[0]
    c1 = rope[:, 0:LANES]
    s1 = rope[:, LANES:2 * LANES]
    s2 = rope[:, 2 * LANES:3 * LANES]
    reps = WIDTH // LANES
    c_t = jnp.concatenate([c1] * reps, axis=1)
    s1_t = jnp.concatenate([s1] * reps, axis=1)
    s2_t = jnp.concatenate([s2] * reps, axis=1)

    att_scale = HEAD_DIM ** -0.5
    idx_scale = IDX_DIM ** -0.5

    def seg(k):
        return _dot(u, w7_ref[:, k * WIDTH:(k + 1) * WIDTH])

    qa_ref[0] = (seg(0) * att_scale).astype(BF16)
    ka_ref[0] = seg(1).astype(BF16)
    va_ref[0] = seg(2).astype(BF16)
    qb_ref[0] = (_rope(seg(3), c_t, s1_t, s2_t) * att_scale).astype(BF16)
    kb_ref[0] = _rope(seg(4), c_t, s1_t, s2_t).astype(BF16)
    vb_ref[0] = seg(5).astype(BF16)
    qi_ref[0] = (_rope(seg(6), c_t, s1_t, s2_t) * idx_scale).astype(BF16)

    z = _dot(u, wkw_ref[...])
    lane = lax.broadcasted_iota(jnp.int32, z.shape, 1)
    is_k = lane < IDX_DIM
    mu = jnp.sum(jnp.where(is_k, z, 0.0), axis=-1, keepdims=True) * (1.0 / IDX_DIM)
    zc = jnp.where(is_k, z - mu, 0.0)
    var = jnp.sum(zc * zc, axis=-1, keepdims=True) * (1.0 / IDX_DIM)
    lnk = lnk_ref[...]
    y = zc * lax.rsqrt(var + LN_EPS) * lnk[0:1, :] + lnk[1:2, :]
    y = _rope(y, c1, jnp.where(is_k, s1, 0.0), jnp.where(is_k, s2, 0.0))
    ki_ref[0] = y[:, 0:IDX_DIM].astype(BF16)
    wi_ref[0] = z.T[IDX_DIM:IDX_DIM + N_IDX_HEADS, :] * (N_IDX_HEADS ** -0.5)

    d = ga_ref.shape[-1]
    ga_ref[0] = _dot(u, wg_ref[:, 0:d]).astype(BF16)
    gb_ref[0] = _dot(u, wg_ref[:, d:2 * d]).astype(BF16)


def _proj(x, mod, rope_tab, w7, wkw, wg, lnk, tm):
    bsz, s, d = x.shape
    const = dict(pipeline_mode=pl.Buffered(1))
    row = lambda b, i: (b, i, 0)
    wspec = lambda shape: pl.BlockSpec(shape, lambda b, i: (0, 0), **const)
    out_w = jax.ShapeDtypeStruct((bsz, s, WIDTH), BF16)
    out_d = jax.ShapeDtypeStruct((bsz, s, d), BF16)
    return pl.pallas_call(
        _proj_kernel,
        out_shape=[out_w] * 7 + [
            jax.ShapeDtypeStruct((bsz, s, IDX_DIM), BF16),
            jax.ShapeDtypeStruct((bsz, N_IDX_HEADS, s), F32),
            out_d, out_d],
        grid=(bsz, s // tm),
        in_specs=[
            pl.BlockSpec((1, tm, d), row),
            pl.BlockSpec((1, N_MOD, d), lambda b, i: (b, 0, 0)),
            pl.BlockSpec((1, tm, 3 * LANES), row),
            wspec(w7.shape), wspec(wkw.shape), wspec(wg.shape), wspec(lnk.shape),
        ],
        out_specs=[pl.BlockSpec((1, tm, WIDTH), row)] * 7 + [
            pl.BlockSpec((1, tm, IDX_DIM), row),
            pl.BlockSpec((1, N_IDX_HEADS, tm), lambda b, i: (b, 0, i)),
            pl.BlockSpec((1, tm, d), row), pl.BlockSpec((1, tm, d), row)],
        compiler_params=_cparams(("arbitrary", "arbitrary")),
        name="proj",
    )(x, mod, rope_tab, w7, wkw, wg, lnk)


def _band_kernel(q_ref, k_ref, v_ref, prof_ref, o_ref, kpad, vpad, a_scr, p_scr, bias_ref, *, tq):
    i = pl.program_id(1)
    s = k_ref.shape[1]
    win = tq + BAND_PAD

    @pl.when((pl.program_id(0) == 0) & (i == 0))
    def _():
        q_chunk = lax.broadcasted_iota(jnp.int32, (tq, win), 0) // CHUNK
        k_chunk = lax.broadcasted_iota(jnp.int32, (tq, win), 1) // CHUNK - (BAND_CHUNKS - 1)
        in_band = (k_chunk <= q_chunk) & (k_chunk >= q_chunk - (BAND_CHUNKS - 1))
        for h in range(N_HEADS):
            rows = jnp.broadcast_to(prof_ref[h:h + 1, :], (tq, prof_ref.shape[1]))
            skew = pltpu.roll(rows, 0, 1, stride=1, stride_axis=0)
            bias_ref[h] = jnp.where(in_band, skew[:, 0:win], NEG_INF)

    @pl.when(i == 0)
    def _():
        zeros = jnp.zeros((BAND_PAD, WIDTH), BF16)
        kpad[0:BAND_PAD, :] = zeros
        vpad[0:BAND_PAD, :] = zeros
        kpad[BAND_PAD:BAND_PAD + s, :] = k_ref[0]
        vpad[BAND_PAD:BAND_PAD + s, :] = v_ref[0]

    q0 = pl.multiple_of(i * tq, tq)
    n_lg = win // LANES
    col = lax.broadcasted_iota(jnp.int32, (tq, LANES), 1)
    first_frame = BAND_PAD - q0
    heads = [slice(h * HEAD_DIM, (h + 1) * HEAD_DIM) for h in range(N_HEADS)]

    def lanes(g):
        return slice(g * LANES, (g + 1) * LANES)

    def scores(mask_padding):
        row_max = []
        for h, sl in enumerate(heads):
            sc = _dot_nt(q_ref[0, :, sl], kpad[pl.ds(q0, win), sl])
            mx = jnp.full((tq, LANES), NEG_INF, F32)
            for g in range(n_lg):
                a = sc[:, lanes(g)] + bias_ref[h, :, lanes(g)]
                if mask_padding:
                    a = jnp.where(col >= first_frame - g * LANES, a, NEG_INF)
                a_scr[h, :, lanes(g)] = a
                mx = jnp.maximum(mx, a)
            row_max.append(jnp.max(mx, axis=-1, keepdims=True))
        return row_max

    row_max = lax.cond(first_frame > 0, lambda: scores(True), lambda: scores(False))
    row_sum = []
    for h in range(N_HEADS):
        m_b = jnp.broadcast_to(row_max[h], (tq, LANES))
        ls = jnp.zeros((tq, LANES), F32)
        for g in range(n_lg):
            p = jnp.exp(a_scr[h, :, lanes(g)] - m_b)
            ls = ls + p
            p_scr[h, :, lanes(g)] = p.astype(BF16)
        row_sum.append(jnp.sum(ls, axis=-1, keepdims=True))
    for h, sl in enumerate(heads):
        o = _dot(p_scr[h], vpad[pl.ds(q0, win), sl]) / row_sum[h]
        o_ref[0, :, sl] = o.astype(BF16)


def _band(qa, ka, va, profile, tq):
    bsz, s, _ = qa.shape
    win = tq + BAND_PAD
    return pl.pallas_call(
        functools.partial(_band_kernel, tq=tq),
        out_shape=jax.ShapeDtypeStruct((bsz, s, WIDTH), BF16),
        grid=(bsz, s // tq),
        in_specs=[
            pl.BlockSpec((1, tq, WIDTH), lambda b, i: (b, i, 0)),
            pl.BlockSpec((1, s, WIDTH), lambda b, i: (b, 0, 0)),
            pl.BlockSpec((1, s, WIDTH), lambda b, i: (b, 0, 0)),
            pl.BlockSpec(profile.shape, lambda b, i: (0, 0)),
        ],
        out_specs=pl.BlockSpec((1, tq, WIDTH), lambda b, i: (b, i, 0)),
        scratch_shapes=[pltpu.VMEM((BAND_PAD + s, WIDTH), BF16),
                        pltpu.VMEM((BAND_PAD + s, WIDTH), BF16),
                        pltpu.VMEM((N_HEADS, tq, win), F32),
                        pltpu.VMEM((N_HEADS, tq, win), BF16),
                        pltpu.VMEM((N_HEADS, tq, win), F32)],
        compiler_params=_cparams(("arbitrary", "arbitrary")),
        name="band",
    )(qa, ka, va, profile)


BISECT_STEPS_PER_CHECK = 4
BISECT_MAX_CHECKS = 400
COUNT_ROWS = 64


def _dsa_kernel(qi_ref, wit_ref, qb_ref, ki_ref, kb_ref, vb_ref, _, o_ref, sct_scr, msk_scr, a_scr, p_scr,
                *, tq, k_sel, q_base):
    i = pl.program_id(1)
    sk = ki_ref.shape[1]
    n_kt = sk // KEY_TILE
    n_lg = sk // LANES
    q0 = q_base + i * tq
    kf = float(k_sel)

    def lanes(g):
        return slice(g * LANES, (g + 1) * LANES)

    def slab_reduce(fn, combine, init):
        acc = jnp.full((COUNT_ROWS, tq), init, F32)
        for r in range(sk // COUNT_ROWS):
            acc = combine(acc, fn(sct_scr[r * COUNT_ROWS:(r + 1) * COUNT_ROWS, :]))
        return acc

    def count(pred):
        part = slab_reduce(lambda t: jnp.where(pred(t), 1.0, 0.0), jnp.add, 0.0)
        return jnp.sum(part, axis=0, keepdims=True)

    w_t = wit_ref[0]
    t_pos = q0 + lax.broadcasted_iota(jnp.int32, (1, tq), 1)
    limit = (t_pos // CHUNK + 1) * CHUNK
    key_in_tile = lax.broadcasted_iota(jnp.int32, (KEY_TILE, tq), 0)
    for kt in range(n_kt):
        ks = slice(kt * KEY_TILE, (kt + 1) * KEY_TILE)
        ki_t = ki_ref[0, ks, :]
        acc = jnp.zeros((KEY_TILE, tq), F32)
        for h in range(N_IDX_HEADS):
            lg = _dot_nt(ki_t, qi_ref[0, :, h * IDX_DIM:(h + 1) * IDX_DIM])
            acc = acc + jnp.maximum(lg, 0.0) * w_t[h:h + 1, :]
        sct_scr[ks, :] = jnp.where(key_in_tile < limit - kt * KEY_TILE, acc, NEG_INF)

    smax = jnp.max(slab_reduce(lambda t: t, jnp.maximum, NEG_INF), axis=0, keepdims=True)
    smin = jnp.min(slab_reduce(lambda t: jnp.where(t == NEG_INF, jnp.inf, t), jnp.minimum, jnp.inf),
                   axis=0, keepdims=True)
    n_adm = limit.astype(F32)
    c_max = count(lambda t: t >= smax)
    c_pos = count(lambda t: t > 0.0)
    c_nn = count(lambda t: t >= 0.0)
    zero = jnp.zeros_like(smax)
    at_zero = (c_pos < kf) & (c_nn >= kf)
    below_zero = c_nn < kf
    lo = jnp.where(at_zero | ~below_zero, zero, smin)
    clo = jnp.where(at_zero | ~below_zero, c_nn, n_adm)
    hi = jnp.where(at_zero | below_zero, zero, smax)
    few = n_adm <= kf
    lo = jnp.where(few, smin, lo)
    clo = jnp.where(few, n_adm, clo)
    top_tied = (c_max >= kf) & ~few
    lo = jnp.where(top_tied, smax, lo)
    clo = jnp.where(top_tied, c_max, clo)
    hi = jnp.where(top_tied, smax, hi)
    done0 = jnp.where(few | at_zero | top_tied, 1.0, 0.0)

    def bisect(carry):
        lo, hi, clo, done, it = carry
        for _ in range(BISECT_STEPS_PER_CHECK):
            mid = 0.5 * lo + 0.5 * hi
            stuck = (mid <= lo) | (mid >= hi)
            c = count(lambda t: t >= mid)
            ge = c >= kf
            lo = jnp.where(ge, mid, lo)
            clo = jnp.where(ge, c, clo)
            hi = jnp.where(ge, hi, mid)
            done = jnp.where(stuck | (clo <= kf), 1.0, done)
        return lo, hi, clo, done, it + 1

    def not_converged(carry):
        _, _, _, done, it = carry
        return (jnp.min(done) < 0.5) & (it < BISECT_MAX_CHECKS)

    lo, hi, clo, _, _ = lax.while_loop(not_converged, bisect, (lo, hi, clo, done0, jnp.int32(0)))

    thr = lo
    c_gt = count(lambda t: t > thr)
    c_eq = count(lambda t: t == thr)
    need = kf - c_gt
    tie_cut = jnp.max(c_gt + c_eq - kf) > 0.0

    def with_ties():
        r_i = lax.broadcasted_iota(jnp.int32, (KEY_TILE, KEY_TILE), 0)
        c_i = lax.broadcasted_iota(jnp.int32, (KEY_TILE, KEY_TILE), 1)
        earlier = jnp.where(c_i < r_i, 1.0, 0.0).astype(BF16)
        carry = jnp.zeros((1, tq), F32)
        for kt in range(n_kt):
            ks = slice(kt * KEY_TILE, (kt + 1) * KEY_TILE)
            sc = sct_scr[ks, :]
            eq = sc == thr
            eq_f = jnp.where(eq, 1.0, 0.0)
            rank = _dot(earlier, eq_f.astype(BF16)) + carry
            keep = (sc > thr) | (eq & (rank < need))
            msk_scr[:, ks] = jnp.where(keep, 0.0, NEG_INF).T
            carry = carry + jnp.sum(eq_f, axis=0, keepdims=True)

    def without_ties():
        for kt in range(n_kt):
            ks = slice(kt * KEY_TILE, (kt + 1) * KEY_TILE)
            msk_scr[:, ks] = jnp.where(sct_scr[ks, :] >= thr, 0.0, NEG_INF).T

    lax.cond(tie_cut, with_ties, without_ties)

    heads = [slice(h * HEAD_DIM, (h + 1) * HEAD_DIM) for h in range(N_HEADS)]
    row_max = []
    for h, sl in enumerate(heads):
        qh = qb_ref[0, :, sl]
        mx = jnp.full((tq, LANES), NEG_INF, F32)
        for kt in range(n_kt):
            ks = slice(kt * KEY_TILE, (kt + 1) * KEY_TILE)
            a = _dot_nt(qh, kb_ref[0, ks, sl]) + msk_scr[:, ks]
            a_scr[h, :, ks] = a
            for g in range(KEY_TILE // LANES):
                mx = jnp.maximum(mx, a[:, lanes(g)])
        row_max.append(jnp.max(mx, axis=-1, keepdims=True))
    row_sum = []
    for h in range(N_HEADS):
        m_b = jnp.broadcast_to(row_max[h], (tq, LANES))
        ls = jnp.zeros((tq, LANES), F32)
        for g in range(n_lg):
            p = jnp.exp(a_scr[h, :, lanes(g)] - m_b)
            ls = ls + p
            p_scr[h, :, lanes(g)] = p.astype(BF16)
        row_sum.append(jnp.sum(ls, axis=-1, keepdims=True))
    for h, sl in enumerate(heads):
        o = _dot(p_scr[h], vb_ref[0, :, sl]) / row_sum[h]
        o_ref[0, :, sl] = o.astype(BF16)


def _dsa_class(qi, wit, qb, ki, kb, vb, out, tq, k_sel, q_base, q_len, sk):
    bsz = qb.shape[0]
    blk0 = q_base // tq
    row = lambda b, i: (b, blk0 + i, 0)
    keys = lambda b, i: (b, 0, 0)
    return pl.pallas_call(
        functools.partial(_dsa_kernel, tq=tq, k_sel=k_sel, q_base=q_base),
        out_shape=jax.ShapeDtypeStruct(out.shape, out.dtype),
        grid=(bsz, q_len // tq),
        in_specs=[
            pl.BlockSpec((1, tq, N_IDX_HEADS * IDX_DIM), row),
            pl.BlockSpec((1, N_IDX_HEADS, tq), lambda b, i: (b, 0, blk0 + i)),
            pl.BlockSpec((1, tq, WIDTH), row),
            pl.BlockSpec((1, sk, IDX_DIM), keys),
            pl.BlockSpec((1, sk, WIDTH), keys),
            pl.BlockSpec((1, sk, WIDTH), keys),
            pl.BlockSpec(memory_space=pl.ANY),
        ],
        out_specs=pl.BlockSpec((1, tq, WIDTH), row),
        input_output_aliases={6: 0},
        scratch_shapes=[pltpu.VMEM((sk, tq), F32), pltpu.VMEM((tq, sk), F32),
                        pltpu.VMEM((N_HEADS, tq, sk), F32), pltpu.VMEM((N_HEADS, tq, sk), BF16)],
        compiler_params=_cparams(("arbitrary", "arbitrary")),
        name=f"dsa_k{sk}",
    )(qi, wit, qb, ki, kb, vb, out)


def _dsa(qi, wit, qb, ki, kb, vb, tq):
    s = qb.shape[1]
    k_sel = min(TOPK_MAX, s // 4)
    n_cls = max(1, min(DSA_KEY_CLASSES, s // KEY_TILE))
    q_len = s // n_cls
    out = jnp.zeros(qb.shape, BF16)
    for c in range(n_cls):
        out = _dsa_class(qi, wit, qb, ki, kb, vb, out, tq, k_sel, c * q_len, q_len, (c + 1) * q_len)
    return out


def _merge_kernel(oa_ref, ob_ref, ga_ref, gb_ref, x_ref, mod_ref, wba_ref, wbb_ref, wo_ref,
                  ln1_ref, wr_ref, rb_ref, wsg_ref, wsu_ref, wsd_ref, z_ref, gk_ref, u2t_ref, expk_ref,
                  rankk_ref, seen_ref, seen_scr, *, alpha):
    tm = x_ref.shape[1]
    ya = _dot(oa_ref[0], wba_ref[...])
    yb = _dot(ob_ref[0], wbb_ref[...])
    merged = _sigmoid(ga_ref[0].astype(F32)) * ya + _sigmoid(gb_ref[0].astype(F32)) * yb
    mix = _dot(merged.astype(BF16), wo_ref[...])
    mod = mod_ref[0]
    g1 = mod[2:3, :]
    sh2 = mod[3:4, :]
    sc2 = mod[4:5, :]
    ln1 = ln1_ref[...]
    x1 = _ln(alpha * x_ref[0] + g1 * mix) * ln1[0:1, :] + ln1[1:2, :]
    u2 = _ln(x1) * (1.0 + sc2) + sh2
    t = u2.astype(BF16)
    shared = _dot((_silu(_dot(t, wsg_ref[...])) * _dot(t, wsu_ref[...])).astype(BF16), wsd_ref[...])
    z_ref[0] = alpha * x1 + mod[5:6, :] * shared

    u_hi, u_lo = _split_bf16(u2)
    both = _dot(u_hi, wr_ref[...])
    logits = both[:, 0:LANES] + both[:, LANES:] + _dot(u_lo, wr_ref[:, 0:LANES])
    aff = _sigmoid(logits.T[0:N_EXPERTS, :])
    biased = aff + rb_ref[...]
    grp = biased.reshape(N_GROUPS, GROUP_SIZE, tm)
    sub = lax.broadcasted_iota(jnp.int32, grp.shape, 1)
    m1 = jnp.max(grp, axis=1, keepdims=True)
    first = jnp.min(jnp.where(grp == m1, sub, GROUP_SIZE), axis=1, keepdims=True)
    m2 = jnp.max(jnp.where(sub == first, NEG_INF, grp), axis=1, keepdims=True)
    gscore = (m1 + m2).reshape(N_GROUPS, tm)
    g_i = lax.broadcasted_iota(jnp.int32, (N_GROUPS, tm), 0)
    g_rank = jnp.zeros((N_GROUPS, tm), F32)
    for g in range(N_GROUPS):
        other = gscore[g:g + 1, :]
        beats = (other > gscore) | ((other == gscore) & (g < g_i))
        g_rank = g_rank + jnp.where(beats, 1.0, 0.0)
    g_keep = jnp.where(g_rank < TOPK_GROUPS, 1.0, 0.0).reshape(N_GROUPS, 1, tm)
    e_keep = jnp.broadcast_to(g_keep, (N_GROUPS, GROUP_SIZE, tm)).reshape(N_EXPERTS, tm)
    masked = jnp.where(e_keep > 0.5, biased, NEG_INF)
    e_i = lax.broadcasted_iota(jnp.int32, (N_EXPERTS, tm), 0)
    chosen = jnp.zeros((N_EXPERTS, tm), F32)
    picks = []
    for _ in range(TOPK_EXPERTS):
        best = jnp.max(masked, axis=0, keepdims=True)
        first = jnp.min(jnp.where(masked == best, e_i, N_EXPERTS), axis=0, keepdims=True)
        hit = e_i == first
        chosen = jnp.where(hit, 1.0, chosen)
        masked = jnp.where(hit, NEG_INF, masked)
        picks.append(first)
    top_aff = jnp.where(chosen > 0.5, aff, 0.0)
    comb_t = top_aff / jnp.sum(top_aff, axis=0, keepdims=True) * ROUTED_SCALE

    @pl.when((pl.program_id(0) == 0) & (pl.program_id(1) == 0))
    def _():
        seen_scr[...] = jnp.zeros_like(seen_scr)

    r_i = lax.broadcasted_iota(jnp.int32, (tm, tm), 0)
    c_i = lax.broadcasted_iota(jnp.int32, (tm, tm), 1)
    earlier = jnp.where(r_i < c_i, 1.0, 0.0).astype(BF16)
    arrival = _dot(chosen.astype(BF16), earlier) + seen_scr[:, 0:1]
    seen_scr[...] = seen_scr[...] + jnp.sum(chosen, axis=1, keepdims=True)
    seen_ref[...] = seen_scr[...]
    ranks, gates = [], []
    for first in picks:
        hit = e_i == first
        ranks.append(jnp.sum(jnp.where(hit, arrival, 0.0), axis=0, keepdims=True))
        gates.append(jnp.sum(jnp.where(hit, comb_t, 0.0), axis=0, keepdims=True))
    expk_ref[...] = jnp.concatenate(picks, axis=0)
    rankk_ref[...] = jnp.concatenate(ranks, axis=0).astype(jnp.int32)
    gate_rows = jnp.concatenate(gates + [jnp.zeros((LANES - TOPK_EXPERTS, tm), F32)], axis=0)
    gk_ref[0] = gate_rows.T
    _store_token_rows(u2t_ref, _pack_pairs(u2), tm)


def _merge(oa, ob, ga, gb, x, mod, wba, wbb, wo, ln1, wr, rb, wsg, wsu, wsd, tm, alpha):
    bsz, s, d = x.shape
    row = lambda b, i: (b, i, 0)
    per_batch = s // tm
    flat = lambda b, i: (b * per_batch + i, 0)
    cols = lambda b, i: (0, b * per_batch + i)
    w2 = lambda shape: pl.BlockSpec(shape, lambda b, i: (0,) * len(shape))
    return pl.pallas_call(
        functools.partial(_merge_kernel, alpha=alpha),
        out_shape=[jax.ShapeDtypeStruct((bsz, s, d), F32),
                   jax.ShapeDtypeStruct((bsz, s, LANES), F32),
                   jax.ShapeDtypeStruct((bsz * s * TILE_ROWS, LANES), jnp.uint32),
                   jax.ShapeDtypeStruct((TOPK_EXPERTS, bsz * s), jnp.int32),
                   jax.ShapeDtypeStruct((TOPK_EXPERTS, bsz * s), jnp.int32),
                   jax.ShapeDtypeStruct((N_EXPERTS, LANES), F32)],
        grid=(bsz, s // tm),
        in_specs=[
            pl.BlockSpec((1, tm, WIDTH), row), pl.BlockSpec((1, tm, WIDTH), row),
            pl.BlockSpec((1, tm, d), row), pl.BlockSpec((1, tm, d), row),
            pl.BlockSpec((1, tm, d), row),
            pl.BlockSpec((1, N_MOD, d), lambda b, i: (b, 0, 0)),
            w2(wba.shape), w2(wbb.shape), w2(wo.shape), w2(ln1.shape), w2(wr.shape), w2(rb.shape),
            w2(wsg.shape), w2(wsu.shape), w2(wsd.shape),
        ],
        out_specs=[pl.BlockSpec((1, tm, d), row),
                   pl.BlockSpec((1, tm, LANES), row),
                   pl.BlockSpec((tm * TILE_ROWS, LANES), flat),
                   pl.BlockSpec((TOPK_EXPERTS, tm), cols),
                   pl.BlockSpec((TOPK_EXPERTS, tm), cols),
                   pl.BlockSpec((N_EXPERTS, LANES), lambda b, i: (0, 0))],
        scratch_shapes=[pltpu.VMEM((N_EXPERTS, LANES), F32)],
        compiler_params=_cparams(("arbitrary", "arbitrary")),
        name="merge",
    )(oa, ob, ga, gb, x, mod, wba, wbb, wo, ln1, wr, rb, wsg, wsu, wsd)


def _silu(z):
    return z * _sigmoid(z)


def _sc_params():
    return pltpu.CompilerParams(use_tc_tiling_on_sc=False)


def _sc_mesh():
    return plsc.VectorSubcoreMesh(core_axis_name="c", subcore_axis_name="s",
                                  num_cores=SC_CORES, num_subcores=SC_SUBCORES)


def _sc_dispatch(x_tiles, dest, n_rows):
    n_tok = x_tiles.shape[0]
    n_k = dest.shape[0]
    mesh = _sc_mesh()

    @pl.kernel(out_type=jax.ShapeDtypeStruct((n_rows,) + x_tiles.shape[1:], x_tiles.dtype), mesh=mesh,
               scratch_types=[], compiler_params=_sc_params())
    def scatter_rows(x_hbm, *rest):
        idx_hbm, o_hbm = rest[:n_k], rest[n_k]

        def body(x_vmem, *idx_vmem):
            for k in range(n_k):
                pltpu.sync_copy(x_vmem, o_hbm.at[idx_vmem[k].at[0]])

        pltpu.emit_pipeline(
            body, grid=(n_tok // SC_WINDOW,),
            in_specs=[pl.BlockSpec((SC_WINDOW,) + x_tiles.shape[1:], lambda i: (i, 0, 0))]
            + [pl.BlockSpec((1, SC_WINDOW), lambda i: (0, i))] * n_k,
            out_specs=[], core_axis_name=("c", "s"), dimension_semantics=(pltpu.PARALLEL,),
        )(x_hbm, *idx_hbm)

    return scatter_rows(x_tiles, *[dest[k:k + 1] for k in range(n_k)])


def _sc_gather(rows, idx):
    n = idx.shape[1]
    mesh = _sc_mesh()

    @pl.kernel(out_type=jax.ShapeDtypeStruct((n,) + rows.shape[1:], rows.dtype), mesh=mesh,
               scratch_types=[], compiler_params=_sc_params())
    def gather_rows(r_hbm, i_hbm, o_hbm):
        def body(i_vmem, o_vmem):
            pltpu.sync_copy(r_hbm.at[i_vmem.at[0]], o_vmem)

        pltpu.emit_pipeline(
            body, grid=(n // SC_WINDOW,),
            in_specs=[pl.BlockSpec((1, SC_WINDOW), lambda i: (0, i))],
            out_specs=[pl.BlockSpec((SC_WINDOW,) + rows.shape[1:], lambda i: (i, 0, 0))],
            core_axis_name=("c", "s"), dimension_semantics=(pltpu.PARALLEL,),
        )(i_hbm, o_hbm)

    return gather_rows(rows, idx)


def _ffn_grouped_kernel(expert_ref, valid_ref, used_ref, x_ref, wg_ref, wu_ref, wd_ref, y_ref,
                        wg_bf, wu_bf, wd_bf):
    i = pl.program_id(0)
    n_rows = x_ref.shape[0] // TILE_ROWS
    in_use = i < used_ref[0]
    new_expert = (i == 0) | (expert_ref[i] != expert_ref[jnp.maximum(i - 1, 0)])

    @pl.when(in_use & new_expert)
    def _():
        wg_bf[...] = wg_ref[0].astype(BF16)
        wu_bf[...] = wu_ref[0].astype(BF16)
        wd_bf[...] = wd_ref[0].astype(BF16)

    @pl.when(in_use)
    def _():
        row = lax.broadcasted_iota(jnp.int32, (n_rows, 1), 0)
        words = jnp.where(row < valid_ref[i], _load_token_rows(x_ref, n_rows), jnp.uint32(0))
        x = _unpack_pairs(words).astype(BF16)
        h = _silu(_dot(x, wg_bf[...])) * _dot(x, wu_bf[...])
        y = _dot(h.astype(BF16), wd_bf[...])
        _store_token_rows(y_ref, _pack_pairs(y), n_rows)


def _ffn_grouped(tile_expert, tile_valid, tiles_used, xs, wg, wu, wd, n_tiles):
    _, d, d_exp = wg.shape
    blk = pl.BlockSpec((MOE_ROW_TILE * TILE_ROWS, LANES), lambda i, te, tv, nu: (i, 0))
    weight = lambda shape: pl.BlockSpec(shape, lambda i, te, tv, nu: (te[i], 0, 0))
    return pl.pallas_call(
        _ffn_grouped_kernel,
        out_shape=jax.ShapeDtypeStruct(xs.shape, xs.dtype),
        grid_spec=pltpu.PrefetchScalarGridSpec(
            num_scalar_prefetch=3, grid=(n_tiles,),
            in_specs=[blk, weight((1, d, d_exp)), weight((1, d, d_exp)), weight((1, d_exp, d))],
            out_specs=blk,
            scratch_shapes=[pltpu.VMEM((d, d_exp), BF16), pltpu.VMEM((d, d_exp), BF16),
                            pltpu.VMEM((d_exp, d), BF16)]),
        compiler_params=_cparams(("arbitrary",)),
        name="moe_ffn",
    )(tile_expert, tile_valid, tiles_used, xs, wg, wu, wd)


def _moe_final_kernel(yg_ref, gk_ref, z_ref, mod_ref, ln2_ref, o_ref):
    tm = z_ref.shape[0]
    gk = gk_ref[...]
    routed = jnp.zeros(z_ref.shape, F32)
    for k in range(TOPK_EXPERTS):
        routed = routed + _unpack_pairs(_load_token_rows(yg_ref, tm, (k,))) * gk[:, k:k + 1]
    g2 = mod_ref[0][5:6, :]
    ln2 = ln2_ref[...]
    o_ref[...] = _ln(z_ref[...] + g2 * routed) * ln2[0:1, :] + ln2[1:2, :]


def _moe_final(yg, gk, z, mod, ln2, tm, tokens_per_batch, tok0):
    t, d = z.shape
    n_tok = yg.shape[1] // TILE_ROWS
    blocks_per_batch = tokens_per_batch // tm
    blk0 = tok0 // tm
    row = lambda i: (blk0 + i, 0)
    return pl.pallas_call(
        _moe_final_kernel,
        out_shape=jax.ShapeDtypeStruct((t, d), F32),
        grid=(n_tok // tm,),
        in_specs=[
            pl.BlockSpec((TOPK_EXPERTS, tm * TILE_ROWS, LANES), lambda i: (0, i, 0)),
            pl.BlockSpec((tm, LANES), row),
            pl.BlockSpec((tm, d), row),
            pl.BlockSpec((1, N_MOD, d), lambda i: ((blk0 + i) // blocks_per_batch, 0, 0)),
            pl.BlockSpec(ln2.shape, lambda i: (0, 0)),
        ],
        out_specs=pl.BlockSpec((tm, d), row),
        input_output_aliases={2: 0},
        compiler_params=_cparams(("arbitrary",)),
        name="moe_final",
    )(yg, gk, z, mod, ln2)


def _sorted_layout(expk, rankk, counts, n_tiles):
    e_ids = jnp.arange(N_EXPERTS, dtype=jnp.int32)[:, None, None]
    is_e = expk[None] == e_ids
    tiles = (counts + MOE_ROW_TILE - 1) // MOE_ROW_TILE
    tile_end = jnp.cumsum(tiles)
    first_row = (tile_end - tiles) * MOE_ROW_TILE
    dest = jnp.sum(jnp.where(is_e, first_row[:, None, None], 0), axis=0, dtype=jnp.int32) + rankk
    tile_ids = jnp.arange(n_tiles, dtype=jnp.int32)[:, None]
    done = tile_ids >= tile_end[None, :]
    owner = ~done & (tile_ids >= (tile_end - tiles)[None, :])
    tile_expert = jnp.minimum(jnp.sum(done, axis=1, dtype=jnp.int32), N_EXPERTS - 1)
    rows_before = (tile_ids[:, 0] - jnp.sum(jnp.where(done, tiles[None, :], 0), axis=1)) * MOE_ROW_TILE
    owner_count = jnp.sum(jnp.where(owner, counts[None, :], 0), axis=1)
    tile_valid = jnp.clip(owner_count - rows_before, 0, MOE_ROW_TILE)
    return dest, tile_expert, tile_valid, tile_end[-1:]


def _rope_tables(positions):
    inv_freq = ROPE_THETA ** (-jnp.arange(0, ROPE_DIM, 2, dtype=F32) / ROPE_DIM)
    ang = positions.astype(F32)[..., None] * inv_freq
    cos, sin = jnp.cos(ang), jnp.sin(ang)
    ones = jnp.ones(cos.shape[:-1] + (HEAD_DIM - ROPE_DIM,), F32)
    zeros = jnp.zeros_like(ones)
    zh = jnp.zeros_like(sin)
    c = jnp.concatenate([cos, cos, ones], axis=-1)
    s1 = jnp.concatenate([-sin, zh, zeros], axis=-1)
    s2 = jnp.concatenate([zh, sin, zeros], axis=-1)
    reps = LANES // HEAD_DIM
    return jnp.concatenate([jnp.tile(c, reps), jnp.tile(s1, reps), jnp.tile(s2, reps)], axis=-1)


def _band_bias_profile(rel_bias, tq):
    n_heads = rel_bias.shape[0]
    win = tq + BAND_PAD
    n_high = BAND_PAD - MAX_REL + 1
    n_ramp = 2 * MAX_REL - 1
    high = rel_bias[:, 2 * MAX_REL:]
    return jnp.concatenate([
        jnp.broadcast_to(high, (n_heads, n_high)),
        rel_bias[:, n_ramp:0:-1],
        jnp.broadcast_to(rel_bias[:, :1], (n_heads, win - n_high - n_ramp)),
        jnp.broadcast_to(high, (n_heads, tq)),
    ], axis=1)


def kernel(x, c, positions, w_ada, b_ada, w_in, rel_bias, idx_k_norm_g, idx_k_norm_b, w_branch_a,
           w_branch_b, w_out, ln1_g, ln1_b, w_router, router_bias, w_exp_gate, w_exp_up, w_exp_down,
           w_sh_gate, w_sh_up, w_sh_down, ln2_g, ln2_b):
    bsz, s, d = x.shape
    depth = w_ada.shape[0]
    alpha = (2.0 * depth) ** 0.25
    tm_proj = min(1024, s)
    tq_band = min(256, s)
    tq_dsa = min(256, s)
    tm_merge = min(512, s)
    tm_moe = min(512, s)

    rope_tab = _rope_tables(positions)
    n7 = 7 * WIDTH
    for l in range(depth):
        mod = _ada(c, w_ada[l], b_ada[l]).reshape(bsz, N_MOD, d)
        w_l = w_in[l]
        w7 = w_l[:, :n7].astype(BF16)
        n_kw = IDX_DIM + N_IDX_HEADS
        wkw = jnp.pad(w_l[:, n7:n7 + n_kw], ((0, 0), (0, LANES - n_kw))).astype(BF16)
        wg = w_l[:, n7 + n_kw:].astype(BF16)
        lnk = jnp.pad(jnp.stack([idx_k_norm_g[l], idx_k_norm_b[l]]), ((0, 0), (0, LANES - IDX_DIM)))
        qa, ka, va, qb, kb, vb, qi, ki, wi, ga, gb = _proj(x, mod, rope_tab, w7, wkw, wg, lnk, tm_proj)

        oa = _band(qa, ka, va, _band_bias_profile(rel_bias[l], tq_band), tq_band)
        ob = _dsa(qi, wi, qb, ki, kb, vb, tq_dsa)

        wr = jnp.pad(w_router[l], ((0, 0), (0, LANES - N_EXPERTS)))
        wr_hi = wr.astype(BF16)
        wr_lo = (wr - wr_hi.astype(F32)).astype(BF16)
        z, gk, u2_tiles, expk, rankk, seen = _merge(
            oa, ob, ga, gb, x, mod,
            w_branch_a[l].astype(BF16), w_branch_b[l].astype(BF16), w_out[l].astype(BF16),
            jnp.stack([ln1_g[l], ln1_b[l]]), jnp.concatenate([wr_hi, wr_lo], axis=1),
            router_bias[l].reshape(N_EXPERTS, 1),
            w_sh_gate[l].astype(BF16), w_sh_up[l].astype(BF16), w_sh_down[l].astype(BF16), tm_merge, alpha)

        t = bsz * s
        n_tiles = t * TOPK_EXPERTS // MOE_ROW_TILE + N_EXPERTS
        n_rows = n_tiles * MOE_ROW_TILE
        dest, tile_expert, tile_valid, tiles_used = _sorted_layout(
            expk, rankk, seen[:, 0].astype(jnp.int32), n_tiles)
        xs = _sc_dispatch(u2_tiles.reshape(t, TILE_ROWS, LANES), dest, n_rows)
        ys = _ffn_grouped(tile_expert, tile_valid, tiles_used, xs.reshape(n_rows * TILE_ROWS, LANES),
                          w_exp_gate[l], w_exp_up[l], w_exp_down[l], n_tiles)
        ys = ys.reshape(n_rows, TILE_ROWS, LANES)
        ln2 = jnp.stack([ln2_g[l], ln2_b[l]])
        chunk = t // MOE_COMBINE_CHUNKS
        out = z.reshape(t, d)
        for c in range(MOE_COMBINE_CHUNKS):
            idx = dest[:, c * chunk:(c + 1) * chunk].reshape(1, TOPK_EXPERTS * chunk)
            yg = _sc_gather(ys, idx).reshape(TOPK_EXPERTS, chunk * TILE_ROWS, LANES)
            out = _moe_final(yg, gk.reshape(t, LANES), out, mod, ln2, tm_moe, s, c * chunk)
        x = out.reshape(bsz, s, d)
    return x
```
